```python
import math
import jax, jax.numpy as jnp
from jax import lax
import numpy as np

D_MODEL = 1024
BATCH = 2
SEQ = 8192
DEPTH = 2

N_MEM = 256
N_MIXERS = 2
N_CONV_LAYERS = (DEPTH + 1) // 2
N_NSA_LAYERS = DEPTH // 2

CONV_WIDTH = 3

NSA_HEADS = 16
NSA_HEAD_DIM = D_MODEL // NSA_HEADS
NSA_KV_GROUPS = 4
NSA_HPG = NSA_HEADS // NSA_KV_GROUPS
CMP_BLOCK = 32
CMP_STRIDE = 16
CMP_HIDDEN = 2 * NSA_HEAD_DIM
SEL_BLOCK = 64
SEL_TOPK = 16
SEL_FORCE = 1.0e4
WINDOW = 512
Q_BLOCK = 128

XATTN_HEADS = 4
XATTN_HEAD_DIM = 128
XATTN_WIDTH = XATTN_HEADS * XATTN_HEAD_DIM
MIX_WIDTH = D_MODEL + XATTN_WIDTH

CONV_IN_WIDTH = 3 * D_MODEL + XATTN_WIDTH
NSA_IN_WIDTH = (NSA_HEADS * NSA_HEAD_DIM + 6 * NSA_KV_GROUPS * NSA_HEAD_DIM
                + 3 * NSA_HEADS + XATTN_WIDTH)

FFN_HIDDEN = ((8 * D_MODEL + 3 * 256 - 1) // (3 * 256)) * 256

REL_BUCKETS = 32
REL_MAX_DIST = 128
RMS_EPS = 1e-6
NEG = -1e30

kernel_name = "hybrid_shortconv_nsa_interleaved"


def rmsnorm(x, g):
    xf = x.astype(jnp.float32)
    y = xf * lax.rsqrt(jnp.mean(xf * xf, axis=-1, keepdims=True) + RMS_EPS)
    return (y * g.astype(jnp.float32)).astype(x.dtype)


def t5_bucket(dist):
    n = jnp.maximum(dist, 0)
    max_exact = REL_BUCKETS // 2
    nf = jnp.maximum(n, 1).astype(jnp.float32)
    large = max_exact + (jnp.log(nf / max_exact) / math.log(REL_MAX_DIST / max_exact)
                         * (REL_BUCKETS - max_exact)).astype(jnp.int32)
    large = jnp.minimum(large, REL_BUCKETS - 1)
    return jnp.where(n < max_exact, n, large)


def masked_softmax(logits, mask):
    p = jax.nn.softmax(jnp.where(mask, logits, NEG), axis=-1)
    return jnp.where(mask, p, 0.0)


def short_conv_mixer(u_b, u_c, u_x, conv_w):
    s = u_x.shape[1]
    v = u_c * u_x
    vp = jnp.pad(v, ((0, 0), (CONV_WIDTH - 1, 0), (0, 0)))
    conv = vp[:, 0:s] * conv_w[0]
    for k in range(1, CONV_WIDTH):
        conv = conv + vp[:, k:k + s] * conv_w[k]
    return u_b * conv


def memory_attention(xq, mem_n, w_mem_kv):
    b, s, _ = xq.shape
    m = mem_n.shape[1]
    kv = mem_n @ w_mem_kv
    k = kv[..., :XATTN_WIDTH].reshape(b, m, XATTN_HEADS, XATTN_HEAD_DIM)
    v = kv[..., XATTN_WIDTH:].reshape(b, m, XATTN_HEADS, XATTN_HEAD_DIM)
    q = xq.reshape(b, s, XATTN_HEADS, XATTN_HEAD_DIM)
    logits = jnp.einsum('bshd,bmhd->bhsm', q, k).astype(jnp.float32) * (XATTN_HEAD_DIM ** -0.5)
    p = jax.nn.softmax(logits, axis=-1).astype(v.dtype)
    o = jnp.einsum('bhsm,bmhd->bshd', p, v)
    return o.reshape(b, s, XATTN_WIDTH)


def nsa_mixer(proj, cmp_pos_k, cmp_w1_k, cmp_w2_k, cmp_pos_v, cmp_w1_v, cmp_w2_v, rel_bias):
    b, s, _ = proj.shape
    H, G, HPG, DK = NSA_HEADS, NSA_KV_GROUPS, NSA_HPG, NSA_HEAD_DIM
    qw = H * DK
    kvw = G * DK
    q = proj[..., :qw].reshape(b, s, G, HPG, DK).transpose(0, 2, 3, 1, 4)
    kv = proj[..., qw:qw + 6 * kvw].reshape(b, s, 6, G, DK).transpose(2, 0, 3, 1, 4)
    gates = jax.nn.sigmoid(proj[..., qw + 6 * kvw:].astype(jnp.float32)).reshape(b, s, 3, H)
    k_cmp_raw, v_cmp_raw, k_slc, v_slc, k_win, v_win = kv[0], kv[1], kv[2], kv[3], kv[4], kv[5]

    n_cmp = (s - CMP_BLOCK) // CMP_STRIDE + 1
    cmp_start = jnp.arange(n_cmp) * CMP_STRIDE
    cmp_idx = cmp_start[:, None] + jnp.arange(CMP_BLOCK)[None, :]

    def compress(raw, pos, w1, w2):
        blocks = raw[:, :, cmp_idx] + pos
        hmid = jax.nn.silu(blocks.reshape(b, G, n_cmp, CMP_BLOCK * DK) @ w1)
        return hmid @ w2

    k_cmp = compress(k_cmp_raw, cmp_pos_k, cmp_w1_k, cmp_w2_k)
    v_cmp = compress(v_cmp_raw, cmp_pos_v, cmp_w1_v, cmp_w2_v)
    cmp_last = cmp_start + CMP_BLOCK - 1

    n_sel = s // SEL_BLOCK
    top_k = min(SEL_TOPK, n_sel)
    sel_start = jnp.arange(n_sel) * SEL_BLOCK
    overlap = jnp.clip(jnp.minimum(cmp_start[:, None] + CMP_BLOCK, sel_start[None, :] + SEL_BLOCK)
                       - jnp.maximum(cmp_start[:, None], sel_start[None, :]), 0, None)
    overlap = overlap.astype(jnp.float32) / CMP_BLOCK

    bias_tab = rel_bias.astype(jnp.float32).T.reshape(G, HPG, REL_BUCKETS)
    g_ix = jnp.arange(G)[None, :, None, None, None]
    h_ix = jnp.arange(HPG)[None, None, :, None, None]
    k_win_p = jnp.pad(k_win, ((0, 0), (0, 0), (WINDOW, 0), (0, 0)))
    v_win_p = jnp.pad(v_win, ((0, 0), (0, 0), (WINDOW, 0), (0, 0)))
    band = WINDOW + Q_BLOCK
    scale = DK ** -0.5
    gather_rows = jax.vmap(jax.vmap(lambda a, i: a[i]))
    sel_offsets = jnp.arange(SEL_BLOCK)
    sel_ids = jnp.arange(n_sel)

    def block_fn(c):
        q0 = c * Q_BLOCK
        qb = lax.dynamic_slice_in_dim(q, q0, Q_BLOCK, axis=3)
        t = q0 + jnp.arange(Q_BLOCK)

        d_cmp = t[:, None] - cmp_last[None, :]
        lg = (jnp.einsum('bghqd,bgcd->bghqc', qb, k_cmp).astype(jnp.float32) * scale
              + bias_tab[:, :, t5_bucket(d_cmp)])
        p_cmp = masked_softmax(lg, d_cmp >= 0)
        o_cmp = jnp.einsum('bghqc,bgcd->bghqd', p_cmp.astype(v_cmp.dtype), v_cmp)

        imp = jnp.einsum('bghqc,cj->bgqj', p_cmp, overlap)
        cur = t // SEL_BLOCK
        forced = ((sel_ids[None, :] == 0) | (sel_ids[None, :] == cur[:, None])
                  | (sel_ids[None, :] == cur[:, None] - 1))
        valid = sel_start[None, :] <= t[:, None]
        imp = jnp.where(forced, SEL_FORCE, jnp.where(valid, imp, -SEL_FORCE))
        _, sel_idx = lax.top_k(imp, top_k)
        tok = (sel_idx[..., None] * SEL_BLOCK + sel_offsets).reshape(b, G, Q_BLOCK, top_k * SEL_BLOCK)
        k_s = gather_rows(k_slc, tok)
        v_s = gather_rows(v_slc, tok)
        d_s = t[None, None, :, None] - tok
        bias_s = bias_tab[g_ix, h_ix, t5_bucket(d_s)[:, :, None]]
        lg = jnp.einsum('bghqd,bgqtd->bghqt', qb, k_s).astype(jnp.float32) * scale + bias_s
        p_s = masked_softmax(lg, (d_s >= 0)[:, :, None])
        o_slc = jnp.einsum('bghqt,bgqtd->bghqd', p_s.astype(v_s.dtype), v_s)

        k_w = lax.dynamic_slice_in_dim(k_win_p, q0, band, axis=2)
        v_w = lax.dynamic_slice_in_dim(v_win_p, q0, band, axis=2)
        pos = q0 - WINDOW + jnp.arange(band)
        d_w = t[:, None] - pos[None, :]
        mask_w = (d_w >= 0) & (d_w < WINDOW) & (pos[None, :] >= 0)
        lg = (jnp.einsum('bghqd,bgkd->bghqk', qb, k_w).astype(jnp.float32) * scale
              + bias_tab[:, :, t5_bucket(d_w)])
        p_w = masked_softmax(lg, mask_w)
        o_win = jnp.einsum('bghqk,bgkd->bghqd', p_w.astype(v_w.dtype), v_w)
        return jnp.stack([o_cmp, o_slc, o_win], axis=0)

    outs = lax.map(block_fn, jnp.arange(s // Q_BLOCK))
    outs = outs.transpose(2, 0, 5, 1, 3, 4, 6).reshape(b, s, 3, H, DK)
    o = jnp.sum(gates[..., None].astype(outs.dtype) * outs, axis=2)
    return o.reshape(b, s, H * DK)


def setup_inputs(seed: int = 0) -> dict:
    key = jax.random.key(seed)
    ks = jax.random.split(key, 24)
    nrm = lambda k, shape, fan_in: jax.random.normal(k, shape, jnp.float32) * (fan_in ** -0.5)
    gain = lambda k, shape: 1.0 + 0.1 * jax.random.normal(k, shape, jnp.float32)
    return {
        "x": jax.random.normal(ks[0], (BATCH, SEQ, D_MODEL), jnp.float32),
        "mem": jax.random.normal(ks[1], (BATCH, N_MEM, D_MODEL), jnp.float32),
        "norm_mix_g": gain(ks[2], (DEPTH, D_MODEL)),
        "norm_mem_g": gain(ks[3], (DEPTH, D_MODEL)),
        "norm_ffn_g": gain(ks[4], (DEPTH, D_MODEL)),
        "norm_final_g": gain(ks[5], (D_MODEL,)),
        "w_in_conv": nrm(ks[6], (N_CONV_LAYERS, D_MODEL, CONV_IN_WIDTH), D_MODEL),
        "conv_w": nrm(ks[7], (N_CONV_LAYERS, CONV_WIDTH, D_MODEL), CONV_WIDTH),
        "w_in_nsa": nrm(ks[8], (N_NSA_LAYERS, D_MODEL, NSA_IN_WIDTH), D_MODEL),
        "cmp_pos_k": 0.1 * jax.random.normal(ks[9], (N_NSA_LAYERS, CMP_BLOCK, NSA_HEAD_DIM), jnp.float32),
        "cmp_w1_k": nrm(ks[10], (N_NSA_LAYERS, CMP_BLOCK * NSA_HEAD_DIM, CMP_HIDDEN), CMP_BLOCK * NSA_HEAD_DIM),
        "cmp_w2_k": nrm(ks[11], (N_NSA_LAYERS, CMP_HIDDEN, NSA_HEAD_DIM), CMP_HIDDEN),
        "cmp_pos_v": 0.1 * jax.random.normal(ks[12], (N_NSA_LAYERS, CMP_BLOCK, NSA_HEAD_DIM), jnp.float32),
        "cmp_w1_v": nrm(ks[13], (N_NSA_LAYERS, CMP_BLOCK * NSA_HEAD_DIM, CMP_HIDDEN), CMP_BLOCK * NSA_HEAD_DIM),
        "cmp_w2_v": nrm(ks[14], (N_NSA_LAYERS, CMP_HIDDEN, NSA_HEAD_DIM), CMP_HIDDEN),
        "rel_bias": 0.5 * jax.random.normal(ks[15], (REL_BUCKETS, NSA_HEADS), jnp.float32),
        "w_mem_kv": nrm(ks[16], (DEPTH, D_MODEL, 2 * XATTN_WIDTH), D_MODEL),
        "w_out": nrm(ks[17], (DEPTH, MIX_WIDTH, D_MODEL), MIX_WIDTH),
        "w_ffn_in": nrm(ks[18], (DEPTH, D_MODEL, 2 * FFN_HIDDEN), D_MODEL),
        "w_ffn_out": nrm(ks[19], (DEPTH, FFN_HIDDEN, D_MODEL), FFN_HIDDEN),
    }


def reference(x, mem, norm_mix_g, norm_mem_g, norm_ffn_g, norm_final_g, w_in_conv, conv_w,
              w_in_nsa, cmp_pos_k, cmp_w1_k, cmp_w2_k, cmp_pos_v, cmp_w1_v, cmp_w2_v, rel_bias,
              w_mem_kv, w_out, w_ffn_in, w_ffn_out):
    h = x
    for i in range(DEPTH):
        a = i // N_MIXERS
        hn = rmsnorm(h, norm_mix_g[i])
        mem_n = rmsnorm(mem, norm_mem_g[i])
        if i % N_MIXERS == 0:
            proj = hn @ w_in_conv[a]
            tok = short_conv_mixer(proj[..., :D_MODEL], proj[..., D_MODEL:2 * D_MODEL],
                                   proj[..., 2 * D_MODEL:3 * D_MODEL], conv_w[a])
            xq = proj[..., 3 * D_MODEL:]
        else:
            proj = hn @ w_in_nsa[a]
            tok = nsa_mixer(proj[..., :NSA_IN_WIDTH - XATTN_WIDTH], cmp_pos_k[a], cmp_w1_k[a],
                            cmp_w2_k[a], cmp_pos_v[a], cmp_w1_v[a], cmp_w2_v[a], rel_bias)
            xq = proj[..., NSA_IN_WIDTH - XATTN_WIDTH:]
        mo = memory_attention(xq, mem_n, w_mem_kv[i])
        h = h + jnp.concatenate([tok, mo], axis=-1) @ w_out[i]
        hn = rmsnorm(h, norm_ffn_g[i])
        gu = hn @ w_ffn_in[i]
        h = h + (jax.nn.silu(gu[..., :FFN_HIDDEN]) * gu[..., FFN_HIDDEN:]) @ w_ffn_out[i]
    return rmsnorm(h, norm_final_g)
```

```python
import functools

import numpy as np
import jax
import jax.numpy as jnp
from jax import lax
from jax.experimental import pallas as pl
from jax.experimental.pallas import tpu as pltpu

F32 = jnp.float32
BF16 = jnp.bfloat16

D_MODEL = 1024
N_MEM = 256
CONV_WIDTH = 3
NSA_HEADS = 16
NSA_HEAD_DIM = 64
NSA_KV_GROUPS = 4
NSA_HPG = NSA_HEADS // NSA_KV_GROUPS
CMP_BLOCK = 32
CMP_STRIDE = 16
CMP_HIDDEN = 2 * NSA_HEAD_DIM
SEL_BLOCK = 64
SEL_TOPK = 16
SEL_FORCE = 1.0e4
WINDOW = 512
XATTN_HEADS = 4
XATTN_HEAD_DIM = 128
XATTN_WIDTH = XATTN_HEADS * XATTN_HEAD_DIM
FFN_HIDDEN = 2816
REL_BUCKETS = 32
REL_MAX_DIST = 128
RMS_EPS = 1e-6
NEG = -1e30

V7X_LANES = 128
V7X_VMEM_BYTES = 64 * 1024 * 1024

ROW_TILE = 512
FFN_ROW_TILE = 256
FFN_CHUNK = 1408
Q_TILE = 256
KV_TILE = 256
N_CMP_PAD = 512
N_SEL = 128

_NT = (((1,), (1,)), ((), ()))


def _vmem_limit(nbytes):
    return int(min(max(nbytes, 16 * 1024 * 1024), V7X_VMEM_BYTES - 8 * 1024 * 1024))


def _rms(x, g):
    ms = jnp.mean(x * x, axis=-1, keepdims=True)
    return x * lax.rsqrt(ms + RMS_EPS) * g


def _dot(a, b):
    return jnp.dot(a, b, preferred_element_type=F32)


def _dot_nt(a, b):
    return lax.dot_general(a, b, _NT, preferred_element_type=F32)


def _sigmoid(x):
    return 1.0 / (1.0 + jnp.exp(-x))


def _memkv_kernel(mem_ref, g_ref, w_ref, o_ref):
    xn = _rms(mem_ref[0], g_ref[0]).astype(BF16)
    o_ref[0, 0] = _dot(xn, w_ref[0]).astype(BF16)


def _memkv(mem, gains, w):
    depth, batch = w.shape[0], mem.shape[0]
    return pl.pallas_call(
        _memkv_kernel,
        out_shape=jax.ShapeDtypeStruct((depth, batch, N_MEM, 2 * XATTN_WIDTH), BF16),
        grid=(depth, batch),
        in_specs=[
            pl.BlockSpec((1, N_MEM, D_MODEL), lambda l, b: (b, 0, 0)),
            pl.BlockSpec((1, 1, D_MODEL), lambda l, b: (l, 0, 0)),
            pl.BlockSpec((1, D_MODEL, 2 * XATTN_WIDTH), lambda l, b: (l, 0, 0)),
        ],
        out_specs=pl.BlockSpec((1, 1, N_MEM, 2 * XATTN_WIDTH), lambda l, b: (l, b, 0, 0)),
        compiler_params=pltpu.CompilerParams(
            dimension_semantics=("arbitrary", "arbitrary"),
            vmem_limit_bytes=_vmem_limit(24 * 1024 * 1024)),
        name="memkv",
    )(mem, gains.reshape(depth, 1, D_MODEL), w)


def _conv_kernel(h_ref, hp_ref, g_ref, w_ref, cw_ref, tok_ref, xq_ref):
    d = D_MODEL
    i = pl.program_id(1)
    g = g_ref[...]
    hn = _rms(h_ref[0], g).astype(BF16)
    u_b = _dot(hn, w_ref[:, 0:d])
    v = _dot(hn, w_ref[:, d:2 * d]) * _dot(hn, w_ref[:, 2 * d:3 * d])
    xq_ref[0] = _dot(hn, w_ref[:, 3 * d:]).astype(BF16)
    hpn = _rms(hp_ref[0], g).astype(BF16)
    vp = _dot(hpn, w_ref[:, d:2 * d]) * _dot(hpn, w_ref[:, 2 * d:3 * d])
    vp = jnp.where(i > 0, vp, 0.0)
    row = lax.broadcasted_iota(jnp.int32, v.shape, 0)
    v1 = jnp.where(row == 0, vp[7:8], pltpu.roll(v, 1, axis=0))
    v2 = jnp.where(row == 0, vp[6:7], jnp.where(row == 1, vp[7:8], pltpu.roll(v, 2, axis=0)))
    cw = cw_ref[...]
    conv = v2 * cw[0:1] + v1 * cw[1:2] + v * cw[2:3]
    tok_ref[0] = (u_b * conv).astype(BF16)


def _conv_front(h, g, w, cw):
    batch, seq, d = h.shape
    tm = ROW_TILE
    width = w.shape[1]
    halo = 8
    return pl.pallas_call(
        _conv_kernel,
        out_shape=(jax.ShapeDtypeStruct((batch, seq, d), BF16),
                   jax.ShapeDtypeStruct((batch, seq, XATTN_WIDTH), BF16)),
        grid=(batch, seq // tm),
        in_specs=[
            pl.BlockSpec((1, tm, d), lambda b, i: (b, i, 0)),
            pl.BlockSpec((1, halo, d), lambda b, i: (b, jnp.maximum(i * (tm // halo) - 1, 0), 0)),
            pl.BlockSpec((1, d), lambda b, i: (0, 0)),
            pl.BlockSpec((d, width), lambda b, i: (0, 0)),
            pl.BlockSpec((CONV_WIDTH, d), lambda b, i: (0, 0)),
        ],
        out_specs=(pl.BlockSpec((1, tm, d), lambda b, i: (b, i, 0)),
                   pl.BlockSpec((1, tm, XATTN_WIDTH), lambda b, i: (b, i, 0))),
        compiler_params=pltpu.CompilerParams(
            dimension_semantics=("arbitrary", "arbitrary"),
            vmem_limit_bytes=_vmem_limit(48 * 1024 * 1024)),
        name="conv_front",
    )(h, h, g.reshape(1, d), w, cw)


def _mixout_kernel(h_ref, tok_ref, xq_ref, kv_ref, w_ref, o_ref):
    d = D_MODEL
    acc = h_ref[0] + _dot(tok_ref[0], w_ref[0:d])
    xq = xq_ref[0]
    kv = kv_ref[0]
    scale = XATTN_HEAD_DIM ** -0.5
    outs = []
    for hh in range(XATTN_HEADS):
        lo = hh * XATTN_HEAD_DIM
        q = xq[:, lo:lo + XATTN_HEAD_DIM]
        k = kv[:, lo:lo + XATTN_HEAD_DIM]
        v = kv[:, XATTN_WIDTH + lo:XATTN_WIDTH + lo + XATTN_HEAD_DIM]
        s = _dot_nt(q, k) * scale
        e = jnp.exp(s - jnp.max(s, axis=-1, keepdims=True))
        p = e / jnp.sum(e, axis=-1, keepdims=True)
        outs.append(_dot(p.astype(BF16), v))
    mo = jnp.concatenate(outs, axis=-1).astype(BF16)
    o_ref[0] = acc + _dot(mo, w_ref[d:])


def _mixout(h, tok, xq, kv, w):
    batch, seq, d = h.shape
    tm = ROW_TILE
    return pl.pallas_call(
        _mixout_kernel,
        out_shape=jax.ShapeDtypeStruct((batch, seq, d), F32),
        grid=(batch, seq // tm),
        in_specs=[
            pl.BlockSpec((1, tm, d), lambda b, i: (b, i, 0)),
            pl.BlockSpec((1, tm, d), lambda b, i: (b, i, 0)),
            pl.BlockSpec((1, tm, XATTN_WIDTH), lambda b, i: (b, i, 0)),
            pl.BlockSpec((1, N_MEM, 2 * XATTN_WIDTH), lambda b, i: (b, 0, 0)),
            pl.BlockSpec((d + XATTN_WIDTH, d), lambda b, i: (0, 0)),
        ],
        out_specs=pl.BlockSpec((1, tm, d), lambda b, i: (b, i, 0)),
        compiler_params=pltpu.CompilerParams(
            dimension_semantics=("arbitrary", "arbitrary"),
            vmem_limit_bytes=_vmem_limit(40 * 1024 * 1024)),
        name="mixout",
    )(h, tok, xq, kv, w)


def _ffn_kernel(h_ref, g_ref, w1_ref, w2_ref, gf_ref, o_ref, *, final):
    h = h_ref[0]
    hn = _rms(h, g_ref[...]).astype(BF16)
    acc = h
    for c in range(FFN_HIDDEN // FFN_CHUNK):
        lo = c * FFN_CHUNK
        a = _dot(hn, w1_ref[:, lo:lo + FFN_CHUNK])
        u = _dot(hn, w1_ref[:, FFN_HIDDEN + lo:FFN_HIDDEN + lo + FFN_CHUNK])
        act = (a * _sigmoid(a) * u).astype(BF16)
        acc = acc + _dot(act, w2_ref[lo:lo + FFN_CHUNK])
    if final:
        acc = _rms(acc, gf_ref[...])
    o_ref[0] = acc


def _ffn(h, g, w1, w2, gf, final):
    batch, seq, d = h.shape
    tm = FFN_ROW_TILE
    return pl.pallas_call(
        functools.partial(_ffn_kernel, final=final),
        out_shape=jax.ShapeDtypeStruct((batch, seq, d), F32),
        grid=(batch, seq // tm),
        in_specs=[
            pl.BlockSpec((1, tm, d), lambda b, i: (b, i, 0)),
            pl.BlockSpec((1, d), lambda b, i: (0, 0)),
            pl.BlockSpec((d, 2 * FFN_HIDDEN), lambda b, i: (0, 0)),
            pl.BlockSpec((FFN_HIDDEN, d), lambda b, i: (0, 0)),
            pl.BlockSpec((1, d), lambda b, i: (0, 0)),
        ],
        out_specs=pl.BlockSpec((1, tm, d), lambda b, i: (b, i, 0)),
        compiler_params=pltpu.CompilerParams(
            dimension_semantics=("arbitrary", "arbitrary"),
            vmem_limit_bytes=_vmem_limit(52 * 1024 * 1024)),
        name="ffn",
    )(h, g.reshape(1, d), w1, w2, gf.reshape(1, d))


_QW = NSA_HEADS * NSA_HEAD_DIM
_KVW = NSA_KV_GROUPS * NSA_HEAD_DIM
_PAIR = 2 * NSA_HEAD_DIM
_C_Q = 0
_C_CMP = _C_Q + _QW
_C_SLC = _C_CMP + 2 * _KVW
_C_WIN = _C_SLC + NSA_KV_GROUPS * _PAIR
_C_GATE = _C_WIN + NSA_KV_GROUPS * _PAIR
_C_XQ = _C_GATE + V7X_LANES
_C_END = _C_XQ + XATTN_WIDTH


def _nsa_proj_kernel(h_ref, g_ref, w_ref, q_ref, cr_ref, slc_ref, win_ref, gt_ref, xq_ref):
    hn = _rms(h_ref[0], g_ref[...]).astype(BF16)
    q_ref[0] = (_dot(hn, w_ref[:, _C_Q:_C_CMP]) * (NSA_HEAD_DIM ** -0.5)).astype(BF16)
    cr = _dot(hn, w_ref[:, _C_CMP:_C_SLC])
    for p in range(2 * _KVW // V7X_LANES):
        cr_ref[0, p] = cr[:, p * V7X_LANES:(p + 1) * V7X_LANES]
    s = _dot(hn, w_ref[:, _C_SLC:_C_WIN]).astype(BF16)
    w = _dot(hn, w_ref[:, _C_WIN:_C_GATE]).astype(BF16)
    for g in range(NSA_KV_GROUPS):
        slc_ref[0, g] = s[:, g * _PAIR:(g + 1) * _PAIR]
        win_ref[0, g] = w[:, g * _PAIR:(g + 1) * _PAIR]
    gt_ref[0] = _sigmoid(_dot(hn, w_ref[:, _C_GATE:_C_XQ]))
    xq_ref[0] = _dot(hn, w_ref[:, _C_XQ:_C_END]).astype(BF16)


def _nsa_proj(h, g, w):
    batch, seq, d = h.shape
    tm = ROW_TILE
    grp = NSA_KV_GROUPS
    row = lambda b, i: (b, i, 0)
    return pl.pallas_call(
        _nsa_proj_kernel,
        out_shape=(jax.ShapeDtypeStruct((batch, seq, _QW), BF16),
                   jax.ShapeDtypeStruct((batch, 2 * _KVW // V7X_LANES, seq, V7X_LANES), F32),
                   jax.ShapeDtypeStruct((batch, grp, seq, _PAIR), BF16),
                   jax.ShapeDtypeStruct((batch, grp, seq, _PAIR), BF16),
                   jax.ShapeDtypeStruct((batch, seq, V7X_LANES), F32),
                   jax.ShapeDtypeStruct((batch, seq, XATTN_WIDTH), BF16)),
        grid=(batch, seq // tm),
        in_specs=[
            pl.BlockSpec((1, tm, d), row),
            pl.BlockSpec((1, d), lambda b, i: (0, 0)),
            pl.BlockSpec((d, _C_END), lambda b, i: (0, 0)),
        ],
        out_specs=(pl.BlockSpec((1, tm, _QW), row),
                   pl.BlockSpec((1, 2 * _KVW // V7X_LANES, tm, V7X_LANES), lambda b, i: (b, 0, i, 0)),
                   pl.BlockSpec((1, grp, tm, _PAIR), lambda b, i: (b, 0, i, 0)),
                   pl.BlockSpec((1, grp, tm, _PAIR), lambda b, i: (b, 0, i, 0)),
                   pl.BlockSpec((1, tm, V7X_LANES), row),
                   pl.BlockSpec((1, tm, XATTN_WIDTH), row)),
        compiler_params=pltpu.CompilerParams(
            dimension_semantics=("arbitrary", "arbitrary"),
            vmem_limit_bytes=_vmem_limit(48 * 1024 * 1024)),
        name="nsa_proj",
    )(h, g.reshape(1, d), w)


_CHUNKS = 512


def _compress_kernel(raw_ref, pos_ref, w1_ref, w2_ref, o_ref, h2_scr, *, transposed):
    dk = NSA_HEAD_DIM
    half = CMP_BLOCK // 2
    h1 = [jnp.zeros((_CHUNKS, CMP_HIDDEN), F32) for _ in range(NSA_KV_GROUPS)]
    h2 = [jnp.zeros((_CHUNKS, CMP_HIDDEN), F32) for _ in range(NSA_KV_GROUPS)]
    gpl = V7X_LANES // dk
    for l in range(half):
        for p in range(NSA_KV_GROUPS // gpl):
            lanes = slice(p * V7X_LANES, (p + 1) * V7X_LANES)
            rows = raw_ref[0, p, pl.ds(l, _CHUNKS, stride=CMP_STRIDE), :]
            a1 = (rows + pos_ref[0, l:l + 1, lanes]).astype(BF16)
            a2 = (rows + pos_ref[0, half + l:half + l + 1, lanes]).astype(BF16)
            for gg in range(gpl):
                g = p * gpl + gg
                h1[g] = h1[g] + _dot(a1[:, gg * dk:(gg + 1) * dk], w1_ref[0, l * dk:(l + 1) * dk, :])
                h2[g] = h2[g] + _dot(a2[:, gg * dk:(gg + 1) * dk],
                                     w1_ref[0, (half + l) * dk:(half + l + 1) * dk, :])
    h2_scr[_CHUNKS:_CHUNKS + 8, :] = jnp.zeros((8, CMP_HIDDEN), F32)
    for g in range(NSA_KV_GROUPS):
        h2_scr[0:_CHUNKS, :] = h2[g]
        hm = h1[g] + h2_scr[pl.ds(1, _CHUNKS), :]
        hm = (hm * _sigmoid(hm)).astype(BF16)
        if transposed:
            out_t = _dot_nt(w2_ref[0], hm)
            o_ref[0, g] = jnp.concatenate([jnp.zeros_like(out_t), out_t], axis=0).astype(BF16)
        else:
            out = _dot(hm, w2_ref[0])
            o_ref[0, g] = jnp.concatenate([out, jnp.zeros_like(out)], axis=1).astype(BF16)


def _compress(raw, pos4, w1, w2, which, transposed):
    batch, _, seq, _ = raw.shape
    grp = NSA_KV_GROUPS
    tiles = _KVW // V7X_LANES
    if transposed:
        out_shape = (batch, grp, _PAIR, N_CMP_PAD)
        w2_block = (1, NSA_HEAD_DIM, CMP_HIDDEN)
    else:
        out_shape = (batch, grp, N_CMP_PAD, _PAIR)
        w2_block = (1, CMP_HIDDEN, NSA_HEAD_DIM)
    return pl.pallas_call(
        functools.partial(_compress_kernel, transposed=transposed),
        out_shape=jax.ShapeDtypeStruct(out_shape, BF16),
        grid=(batch,),
        in_specs=[
            pl.BlockSpec((1, tiles, seq, V7X_LANES), lambda b: (b, which, 0, 0)),
            pl.BlockSpec((1, CMP_BLOCK, _KVW), lambda b: (which, 0, 0)),
            pl.BlockSpec((1, CMP_BLOCK * NSA_HEAD_DIM, CMP_HIDDEN), lambda b: (which, 0, 0)),
            pl.BlockSpec(w2_block, lambda b: (0, 0, 0)),
        ],
        out_specs=pl.BlockSpec((1,) + out_shape[1:], lambda b: (b, 0, 0, 0)),
        scratch_shapes=[pltpu.VMEM((_CHUNKS + 8, CMP_HIDDEN), F32)],
        compiler_params=pltpu.CompilerParams(
            dimension_semantics=("arbitrary",),
            vmem_limit_bytes=_vmem_limit(40 * 1024 * 1024)),
        name="compress_v" if transposed else "compress_k",
    )(raw, pos4, w1, w2)


def _softmax_rows(s):
    m = jnp.max(s, axis=-1, keepdims=True)
    p = jnp.exp(s - m)
    return m, p, jnp.sum(p, axis=-1, keepdims=True)


def _nsa_kernel(q_ref, kc_ref, vct_ref, slc_ref, win_ref, gt_ref, cbt_ref, tslc_ref, twin_ref,
                c31_ref, o_ref, ps_scr):
    tq = Q_TILE
    hpg = NSA_HPG
    dk = NSA_HEAD_DIM
    g = pl.program_id(1)
    i = pl.program_id(2)
    q0 = i * tq

    qb = q_ref[0]
    lane = lax.broadcasted_iota(jnp.int32, (tq, _PAIR), 1)
    bands = []
    for j in range(hpg):
        blk = qb[:, (j // 2) * _PAIR:(j // 2 + 1) * _PAIR].astype(F32)
        if j % 2:
            blk = pltpu.roll(blk, dk, axis=1)
        bands.append(jnp.where(lane < dk, blk, 0.0))
    qs = jnp.concatenate(bands, axis=0).astype(BF16)

    st = _dot_nt(kc_ref[0, 0], qs)
    st = st + cbt_ref[0, pl.ds(pl.multiple_of(N_CMP_PAD - (tq // CMP_STRIDE) * i, 8), N_CMP_PAD), :]
    mx = jnp.max(st, axis=0, keepdims=True)
    e = jnp.where(st > 0.5 * NEG, jnp.exp(st - mx), 0.0)
    den = jnp.sum(e, axis=0, keepdims=True)
    pt = e * jnp.where(den > 0.0, 1.0 / den, 0.0)
    o_cmp_t = _dot(vct_ref[0, 0], pt.astype(BF16))
    psum = pt[:, 0:tq]
    for j in range(1, hpg):
        psum = psum + pt[:, j * tq:(j + 1) * tq]

    ratio = SEL_BLOCK // CMP_STRIDE
    imp_tiles = []
    for c in range(tq // V7X_LANES):
        ps_scr[c, 0:8, :] = jnp.zeros((8, V7X_LANES), F32)
        ps_scr[c, 8:8 + N_CMP_PAD, :] = psum[:, c * V7X_LANES:(c + 1) * V7X_LANES]
        ps_scr[c, 8 + N_CMP_PAD:16 + N_CMP_PAD, :] = jnp.zeros((8, V7X_LANES), F32)
        taps = [ps_scr[c, pl.ds(7 + r, N_SEL, stride=ratio), :] for r in range(5)]
        imp_tiles.append(0.5 * taps[0] + taps[1] + taps[2] + taps[3] + 0.5 * taps[4])
    imp = jnp.concatenate(imp_tiles, axis=1)

    jrow = lax.broadcasted_iota(jnp.int32, (N_SEL, tq), 0)
    cur = (q0 + lax.broadcasted_iota(jnp.int32, (N_SEL, tq), 1)) // SEL_BLOCK
    imp = jnp.where(jrow <= cur, imp, -SEL_FORCE)
    for forced_block in (0, cur, cur - 1):
        imp = jnp.where(jrow == forced_block, SEL_FORCE, imp)
    sel_t = jnp.zeros((N_SEL, tq), F32)
    for _ in range(SEL_TOPK):
        top = jnp.max(imp, axis=0, keepdims=True)
        first = jnp.min(jnp.where(imp == top, jrow, N_SEL), axis=0, keepdims=True)
        pick = jrow == first
        sel_t = jnp.where(pick, 1.0, sel_t)
        imp = jnp.where(pick, -3.0e38, imp)
    sel = sel_t.T.astype(BF16)

    jb_n = lax.broadcasted_iota(jnp.int32, (N_SEL, 2 * tq), 0)
    kk_n = lax.broadcasted_iota(jnp.int32, (N_SEL, 2 * tq), 1)
    jb_f = lax.broadcasted_iota(jnp.int32, (N_SEL, KV_TILE), 0)
    kk_f = lax.broadcasted_iota(jnp.int32, (N_SEL, KV_TILE), 1)
    blocks_per_tile = tq // SEL_BLOCK

    def sel_mask(jb, kk, first_block):
        expand = jnp.where(jb == first_block + kk // SEL_BLOCK, 1.0, 0.0).astype(BF16)
        return (_dot(sel, expand) - 1.0) * (-NEG)

    def add_rows(s, extra):
        return jnp.concatenate([s[j * tq:(j + 1) * tq] + extra for j in range(hpg)], axis=0)

    prev0 = pl.multiple_of(jnp.maximum(q0 - tq, 0), tq)
    kv_n = jnp.concatenate([slc_ref[0, 0, pl.ds(prev0, tq), :],
                            slc_ref[0, 0, pl.ds(pl.multiple_of(q0, tq), tq), :]], axis=0)
    lane_n = lax.broadcasted_iota(jnp.int32, (tq, 2 * tq), 1)
    no_prev = jnp.where(jnp.logical_and(lane_n < tq, i == 0), NEG, 0.0)
    s = _dot_nt(qs, kv_n) + tslc_ref[0]
    s = add_rows(s, sel_mask(jb_n, kk_n, blocks_per_tile * (i - 1)) + no_prev)
    m, p, l = _softmax_rows(s)
    acc = _dot(p.astype(BF16), kv_n)

    c31 = c31_ref[0]
    c31_t = jnp.concatenate([c31] * (KV_TILE // V7X_LANES), axis=1)

    def far_body(kt, carry):
        m, l, acc = carry
        kv = slc_ref[0, 0, pl.ds(pl.multiple_of(kt * KV_TILE, KV_TILE), KV_TILE), :]
        s = _dot_nt(qs, kv) + c31_t
        s = add_rows(s, sel_mask(jb_f, kk_f, (KV_TILE // SEL_BLOCK) * kt))
        m_new = jnp.maximum(m, jnp.max(s, axis=-1, keepdims=True))
        alpha = jnp.exp(m - m_new)
        p = jnp.exp(s - m_new)
        l = alpha * l + jnp.sum(p, axis=-1, keepdims=True)
        acc = alpha * acc + _dot(p.astype(BF16), kv)
        return m_new, l, acc

    m, l, acc = lax.fori_loop(0, (q0 - tq) // KV_TILE, far_body, (m, l, acc))
    o_slc = acc / l

    tiles = []
    n_prev = WINDOW // tq
    for r in range(n_prev, 0, -1):
        tiles.append(win_ref[0, 0, pl.ds(pl.multiple_of(jnp.maximum(q0 - r * tq, 0), tq), tq), :])
    tiles.append(win_ref[0, 0, pl.ds(pl.multiple_of(q0, tq), tq), :])
    kv_w = jnp.concatenate(tiles, axis=0)
    lane_w = lax.broadcasted_iota(jnp.int32, (tq, (n_prev + 1) * tq), 1)
    missing = jnp.where(lane_w < (n_prev - i) * tq, NEG, 0.0)
    s = add_rows(_dot_nt(qs, kv_w) + twin_ref[0], missing)
    _, p, l = _softmax_rows(s)
    o_win = _dot(p.astype(BF16), kv_w) / l

    gt = gt_ref[0]
    heads = []
    for j in range(hpg):
        h = g * hpg + j
        oc = o_cmp_t[:, j * tq:(j + 1) * tq].T
        os_ = o_slc[j * tq:(j + 1) * tq]
        ow = o_win[j * tq:(j + 1) * tq]
        o = (_take_lane(gt, h) * oc + _take_lane(gt, NSA_HEADS + h) * os_
             + _take_lane(gt, 2 * NSA_HEADS + h) * ow)
        heads.append(o)
    lane_o = lax.broadcasted_iota(jnp.int32, (tq, _PAIR), 1)
    pairs = []
    for jj in range(hpg // 2):
        even = pltpu.roll(heads[2 * jj], dk, axis=1)
        pairs.append(jnp.where(lane_o < dk, even, heads[2 * jj + 1]))
    o_ref[0] = jnp.concatenate(pairs, axis=1).astype(BF16)


def _take_lane(x, idx):
    lane = lax.broadcasted_iota(jnp.int32, x.shape, 1)
    return jnp.sum(jnp.where(lane == idx, x, 0.0), axis=-1, keepdims=True)


def _nsa_attention(q, kc, vct, slc, win, gates, cbt, tslc, twin, c31):
    batch, seq, _ = q.shape
    grp = NSA_KV_GROUPS
    tq = Q_TILE
    rows = NSA_HPG * tq
    return pl.pallas_call(
        _nsa_kernel,
        out_shape=jax.ShapeDtypeStruct((batch, seq, _QW), BF16),
        grid=(batch, grp, seq // tq),
        in_specs=[
            pl.BlockSpec((1, tq, NSA_HPG * NSA_HEAD_DIM), lambda b, g, i: (b, i, g)),
            pl.BlockSpec((1, 1, N_CMP_PAD, _PAIR), lambda b, g, i: (b, g, 0, 0)),
            pl.BlockSpec((1, 1, _PAIR, N_CMP_PAD), lambda b, g, i: (b, g, 0, 0)),
            pl.BlockSpec((1, 1, seq, _PAIR), lambda b, g, i: (b, g, 0, 0)),
            pl.BlockSpec((1, 1, seq, _PAIR), lambda b, g, i: (b, g, 0, 0)),
            pl.BlockSpec((1, tq, V7X_LANES), lambda b, g, i: (b, i, 0)),
            pl.BlockSpec((1, 2 * N_CMP_PAD, rows), lambda b, g, i: (g, 0, 0)),
            pl.BlockSpec((1, rows, 2 * tq), lambda b, g, i: (g, 0, 0)),
            pl.BlockSpec((1, rows, WINDOW + tq), lambda b, g, i: (g, 0, 0)),
            pl.BlockSpec((1, rows, V7X_LANES), lambda b, g, i: (g, 0, 0)),
        ],
        out_specs=pl.BlockSpec((1, tq, NSA_HPG * NSA_HEAD_DIM), lambda b, g, i: (b, i, g)),
        scratch_shapes=[pltpu.VMEM((tq // V7X_LANES, N_CMP_PAD + 16, V7X_LANES), F32)],
        compiler_params=pltpu.CompilerParams(
            dimension_semantics=("arbitrary", "arbitrary", "arbitrary"),
            vmem_limit_bytes=_vmem_limit(56 * 1024 * 1024)),
        name="nsa_attention",
    )(q, kc, vct, slc, win, gates, cbt, tslc, twin, c31)


def _t5_bucket(dist):
    n = jnp.maximum(dist, 0)
    max_exact = REL_BUCKETS // 2
    nf = jnp.maximum(n, 1).astype(F32)
    large = max_exact + (jnp.log(nf / max_exact) / np.log(REL_MAX_DIST / max_exact)
                         * (REL_BUCKETS - max_exact)).astype(jnp.int32)
    large = jnp.minimum(large, REL_BUCKETS - 1)
    return jnp.where(n < max_exact, n, large)


def _bias_tables(rel_bias):
    tq = Q_TILE
    hpg = NSA_HPG
    by_dist = rel_bias.astype(F32)[_t5_bucket(jnp.arange(REL_MAX_DIST + 1))]
    by_dist = by_dist.T.reshape(NSA_KV_GROUPS, hpg, REL_MAX_DIST + 1)

    def table(dist, ok):
        idx = np.clip(dist, 0, REL_MAX_DIST)
        vals = by_dist[:, :, idx]
        return jnp.where(ok, vals, NEG)

    qi = np.arange(tq)
    u = np.arange(2 * N_CMP_PAD)
    d = qi[None, :] - (CMP_BLOCK - 1) - CMP_STRIDE * (u[:, None] - N_CMP_PAD)
    cbt = table(d, d >= 0)
    cbt = cbt.transpose(0, 2, 1, 3).reshape(NSA_KV_GROUPS, 2 * N_CMP_PAD, hpg * tq)
    u = np.arange(2 * tq)
    d = qi[:, None] - u[None, :] + tq
    tslc = table(d, d >= 0).reshape(NSA_KV_GROUPS, hpg * tq, 2 * tq)
    u = np.arange(WINDOW + tq)
    d = qi[:, None] - u[None, :] + WINDOW
    twin = table(d, (d >= 0) & (d < WINDOW)).reshape(NSA_KV_GROUPS, hpg * tq, WINDOW + tq)
    far = jnp.broadcast_to(by_dist[:, :, REL_MAX_DIST][:, :, None, None],
                           (NSA_KV_GROUPS, hpg, tq, V7X_LANES))
    c31 = far.reshape(NSA_KV_GROUPS, hpg * tq, V7X_LANES)
    return cbt, tslc, twin, c31


def _nsa_weight(w):
    d = w.shape[0]
    q = w[:, :_QW]
    kv = w[:, _QW:_QW + 6 * _KVW].reshape(d, 6, NSA_KV_GROUPS, NSA_HEAD_DIM)
    cmp_raw = kv[:, 0:2].reshape(d, 2 * _KVW)
    slc = jnp.stack([kv[:, 2], kv[:, 3]], axis=2).reshape(d, NSA_KV_GROUPS * _PAIR)
    win = jnp.stack([kv[:, 4], kv[:, 5]], axis=2).reshape(d, NSA_KV_GROUPS * _PAIR)
    gates = w[:, _QW + 6 * _KVW:_QW + 6 * _KVW + 3 * NSA_HEADS]
    gates = jnp.pad(gates, ((0, 0), (0, V7X_LANES - 3 * NSA_HEADS)))
    xq = w[:, _QW + 6 * _KVW + 3 * NSA_HEADS:]
    return jnp.concatenate([q, cmp_raw, slc, win, gates, xq], axis=1).astype(BF16)


def kernel(x, mem, norm_mix_g, norm_mem_g, norm_ffn_g, norm_final_g, w_in_conv, conv_w, w_in_nsa,
           cmp_pos_k, cmp_w1_k, cmp_w2_k, cmp_pos_v, cmp_w1_v, cmp_w2_v, rel_bias, w_mem_kv, w_out,
           w_ffn_in, w_ffn_out):
    assert x.shape[1] % ROW_TILE == 0 and x.shape[1] % Q_TILE == 0 and x.shape[2] == D_MODEL
    assert x.shape[1] == CMP_STRIDE * _CHUNKS and x.shape[1] == SEL_BLOCK * N_SEL
    memkv = _memkv(mem, norm_mem_g, w_mem_kv.astype(BF16))
    w_out_b = w_out.astype(BF16)
    w_ffn_in_b = w_ffn_in.astype(BF16)
    w_ffn_out_b = w_ffn_out.astype(BF16)

    tok, xq = _conv_front(x, norm_mix_g[0], w_in_conv[0].astype(BF16), conv_w[0])
    h = _mixout(x, tok, xq, memkv[0], w_out_b[0])
    h = _ffn(h, norm_ffn_g[0], w_ffn_in_b[0], w_ffn_out_b[0], norm_final_g, final=False)

    q, cmp_raw, slc, win, gates, xq = _nsa_proj(h, norm_mix_g[1], _nsa_weight(w_in_nsa[0]))
    pos4 = jnp.stack([jnp.tile(cmp_pos_k[0], (1, NSA_KV_GROUPS)),
                      jnp.tile(cmp_pos_v[0], (1, NSA_KV_GROUPS))])
    w1 = jnp.stack([cmp_w1_k[0], cmp_w1_v[0]]).astype(BF16)
    kc = _compress(cmp_raw, pos4, w1, cmp_w2_k[0].astype(BF16)[None], 0, transposed=False)
    vct = _compress(cmp_raw, pos4, w1, cmp_w2_v[0].T.astype(BF16)[None], 1, transposed=True)
    tok = _nsa_attention(q, kc, vct, slc, win, gates, *_bias_tables(rel_bias))
    h = _mixout(h, tok, xq, memkv[1], w_out_b[1])
    return _ffn(h, norm_ffn_g[1], w_ffn_in_b[1], w_ffn_out_b[1], norm_final_g, final=True)
```

```python
import functools

import numpy as np
import jax
import jax.numpy as jnp
from jax import lax
from jax.experimental import pallas as pl
from jax.experimental.pallas import tpu as pltpu

F32 = jnp.float32
BF16 = jnp.bfloat16

D_MODEL = 1024
N_MEM = 256
CONV_WIDTH = 3
NSA_HEADS = 16
NSA_HEAD_DIM = 64
NSA_KV_GROUPS = 4
NSA_HPG = NSA_HEADS // NSA_KV_GROUPS
CMP_BLOCK = 32
CMP_STRIDE = 16
CMP_HIDDEN = 2 * NSA_HEAD_DIM
SEL_BLOCK = 64
SEL_TOPK = 16
SEL_FORCE = 1.0e4
WINDOW = 512
XATTN_HEADS = 4
XATTN_HEAD_DIM = 128
XATTN_WIDTH = XATTN_HEADS * XATTN_HEAD_DIM
FFN_HIDDEN = 2816
REL_BUCKETS = 32
REL_MAX_DIST = 128
RMS_EPS = 1e-6
NEG = -1e30

V7X_LANES = 128
V7X_VMEM_BYTES = 64 * 1024 * 1024

ROW_TILE = 512
FFN_ROW_TILE = 256
FFN_CHUNK = 1408
Q_TILE = 256
KV_TILE = 256
N_CMP_PAD = 512
N_SEL = 128

_NT = (((1,), (1,)), ((), ()))


def _vmem_limit(nbytes):
    return int(min(max(nbytes, 16 * 1024 * 1024), V7X_VMEM_BYTES - 8 * 1024 * 1024))


def _rms(x, g):
    ms = jnp.mean(x * x, axis=-1, keepdims=True)
    return x * lax.rsqrt(ms + RMS_EPS) * g


def _dot(a, b):
    return jnp.dot(a, b, preferred_element_type=F32)


def _dot_nt(a, b):
    return lax.dot_general(a, b, _NT, preferred_element_type=F32)


def _sigmoid(x):
    return 1.0 / (1.0 + jnp.exp(-x))


def _resident(block_shape, index_map):
    return pl.BlockSpec(block_shape, index_map, pipeline_mode=pl.Buffered(1))


def _memkv_kernel(mem_ref, g_ref, w_ref, o_ref):
    xn = _rms(mem_ref[0], g_ref[0]).astype(BF16)
    o_ref[0, 0] = _dot(xn, w_ref[0]).astype(BF16)


def _memkv(mem, gains, w):
    depth, batch = w.shape[0], mem.shape[0]
    return pl.pallas_call(
        _memkv_kernel,
        out_shape=jax.ShapeDtypeStruct((depth, batch, N_MEM, 2 * XATTN_WIDTH), BF16),
        grid=(depth, batch),
        in_specs=[
            pl.BlockSpec((1, N_MEM, D_MODEL), lambda l, b: (b, 0, 0)),
            pl.BlockSpec((1, 1, D_MODEL), lambda l, b: (l, 0, 0)),
            pl.BlockSpec((1, D_MODEL, 2 * XATTN_WIDTH), lambda l, b: (l, 0, 0)),
        ],
        out_specs=pl.BlockSpec((1, 1, N_MEM, 2 * XATTN_WIDTH), lambda l, b: (l, b, 0, 0)),
        compiler_params=pltpu.CompilerParams(
            dimension_semantics=("arbitrary", "arbitrary"),
            vmem_limit_bytes=_vmem_limit(24 * 1024 * 1024)),
        name="memkv",
    )(mem, gains.reshape(depth, 1, D_MODEL), w)


def _conv_kernel(h_ref, hp_ref, g_ref, w_ref, cw_ref, tok_ref, xq_ref):
    d = D_MODEL
    i = pl.program_id(1)
    g = g_ref[...]
    hn = _rms(h_ref[0], g).astype(BF16)
    u_b = _dot(hn, w_ref[:, 0:d])
    v = _dot(hn, w_ref[:, d:2 * d]) * _dot(hn, w_ref[:, 2 * d:3 * d])
    xq_ref[0] = _dot(hn, w_ref[:, 3 * d:]).astype(BF16)
    hpn = _rms(hp_ref[0], g).astype(BF16)
    vp = _dot(hpn, w_ref[:, d:2 * d]) * _dot(hpn, w_ref[:, 2 * d:3 * d])
    vp = jnp.where(i > 0, vp, 0.0)
    row = lax.broadcasted_iota(jnp.int32, v.shape, 0)
    v1 = jnp.where(row == 0, vp[7:8], pltpu.roll(v, 1, axis=0))
    v2 = jnp.where(row == 0, vp[6:7], jnp.where(row == 1, vp[7:8], pltpu.roll(v, 2, axis=0)))
    cw = cw_ref[...]
    conv = v2 * cw[0:1] + v1 * cw[1:2] + v * cw[2:3]
    tok_ref[0] = (u_b * conv).astype(BF16)


def _conv_front(h, g, w, cw):
    batch, seq, d = h.shape
    tm = ROW_TILE
    width = w.shape[1]
    halo = 8
    return pl.pallas_call(
        _conv_kernel,
        out_shape=(jax.ShapeDtypeStruct((batch, seq, d), BF16),
                   jax.ShapeDtypeStruct((batch, seq, XATTN_WIDTH), BF16)),
        grid=(batch, seq // tm),
        in_specs=[
            pl.BlockSpec((1, tm, d), lambda b, i: (b, i, 0)),
            pl.BlockSpec((1, halo, d), lambda b, i: (b, jnp.maximum(i * (tm // halo) - 1, 0), 0)),
            pl.BlockSpec((1, d), lambda b, i: (0, 0)),
            pl.BlockSpec((d, width), lambda b, i: (0, 0)),
            pl.BlockSpec((CONV_WIDTH, d), lambda b, i: (0, 0)),
        ],
        out_specs=(pl.BlockSpec((1, tm, d), lambda b, i: (b, i, 0)),
                   pl.BlockSpec((1, tm, XATTN_WIDTH), lambda b, i: (b, i, 0))),
        compiler_params=pltpu.CompilerParams(
            dimension_semantics=("arbitrary", "arbitrary"),
            vmem_limit_bytes=_vmem_limit(48 * 1024 * 1024)),
        name="conv_front",
    )(h, h, g.reshape(1, d), w, cw)


def _mixout_kernel(h_ref, tok_ref, xq_ref, kv_ref, w_ref, o_ref):
    d = D_MODEL
    acc = h_ref[0] + _dot(tok_ref[0], w_ref[0:d])
    xq = xq_ref[0]
    kv = kv_ref[0]
    scale = XATTN_HEAD_DIM ** -0.5
    outs = []
    for hh in range(XATTN_HEADS):
        lo = hh * XATTN_HEAD_DIM
        q = xq[:, lo:lo + XATTN_HEAD_DIM]
        k = kv[:, lo:lo + XATTN_HEAD_DIM]
        v = kv[:, XATTN_WIDTH + lo:XATTN_WIDTH + lo + XATTN_HEAD_DIM]
        s = _dot_nt(q, k) * scale
        e = jnp.exp(s - jnp.max(s, axis=-1, keepdims=True))
        p = e / jnp.sum(e, axis=-1, keepdims=True)
        outs.append(_dot(p.astype(BF16), v))
    mo = jnp.concatenate(outs, axis=-1).astype(BF16)
    o_ref[0] = acc + _dot(mo, w_ref[d:])


def _mixout(h, tok, xq, kv, w):
    batch, seq, d = h.shape
    tm = ROW_TILE
    return pl.pallas_call(
        _mixout_kernel,
        out_shape=jax.ShapeDtypeStruct((batch, seq, d), F32),
        grid=(batch, seq // tm),
        in_specs=[
            pl.BlockSpec((1, tm, d), lambda b, i: (b, i, 0)),
            pl.BlockSpec((1, tm, d), lambda b, i: (b, i, 0)),
            pl.BlockSpec((1, tm, XATTN_WIDTH), lambda b, i: (b, i, 0)),
            pl.BlockSpec((1, N_MEM, 2 * XATTN_WIDTH), lambda b, i: (b, 0, 0)),
            pl.BlockSpec((d + XATTN_WIDTH, d), lambda b, i: (0, 0)),
        ],
        out_specs=pl.BlockSpec((1, tm, d), lambda b, i: (b, i, 0)),
        compiler_params=pltpu.CompilerParams(
            dimension_semantics=("arbitrary", "arbitrary"),
            vmem_limit_bytes=_vmem_limit(40 * 1024 * 1024)),
        name="mixout",
    )(h, tok, xq, kv, w)


def _ffn_kernel(h_ref, g_ref, w1_ref, w2_ref, gf_ref, o_ref, *, final):
    h = h_ref[0]
    hn = _rms(h, g_ref[...]).astype(BF16)
    acc = h
    for c in range(FFN_HIDDEN // FFN_CHUNK):
        lo = c * FFN_CHUNK
        a = _dot(hn, w1_ref[:, lo:lo + FFN_CHUNK])
        u = _dot(hn, w1_ref[:, FFN_HIDDEN + lo:FFN_HIDDEN + lo + FFN_CHUNK])
        act = (a * _sigmoid(a) * u).astype(BF16)
        acc = acc + _dot(act, w2_ref[lo:lo + FFN_CHUNK])
    if final:
        acc = _rms(acc, gf_ref[...])
    o_ref[0] = acc


def _ffn(h, g, w1, w2, gf, final):
    batch, seq, d = h.shape
    tm = FFN_ROW_TILE
    return pl.pallas_call(
        functools.partial(_ffn_kernel, final=final),
        out_shape=jax.ShapeDtypeStruct((batch, seq, d), F32),
        grid=(batch, seq // tm),
        in_specs=[
            pl.BlockSpec((1, tm, d), lambda b, i: (b, i, 0)),
            pl.BlockSpec((1, d), lambda b, i: (0, 0)),
            pl.BlockSpec((d, 2 * FFN_HIDDEN), lambda b, i: (0, 0)),
            pl.BlockSpec((FFN_HIDDEN, d), lambda b, i: (0, 0)),
            pl.BlockSpec((1, d), lambda b, i: (0, 0)),
        ],
        out_specs=pl.BlockSpec((1, tm, d), lambda b, i: (b, i, 0)),
        compiler_params=pltpu.CompilerParams(
            dimension_semantics=("arbitrary", "arbitrary"),
            vmem_limit_bytes=_vmem_limit(52 * 1024 * 1024)),
        name="ffn",
    )(h, g.reshape(1, d), w1, w2, gf.reshape(1, d))


_QW = NSA_HEADS * NSA_HEAD_DIM
_KVW = NSA_KV_GROUPS * NSA_HEAD_DIM
_PAIR = 2 * NSA_HEAD_DIM
_C_Q = 0
_C_CMP = _C_Q + _QW
_C_SLC = _C_CMP + 2 * _KVW
_C_WIN = _C_SLC + NSA_KV_GROUPS * _PAIR
_C_GATE = _C_WIN + NSA_KV_GROUPS * _PAIR
_C_XQ = _C_GATE + V7X_LANES
_C_END = _C_XQ + XATTN_WIDTH


def _ones_and_v(kv):
    lane = lax.broadcasted_iota(jnp.int32, kv.shape, 1)
    return jnp.where(lane < NSA_HEAD_DIM, 1.0, kv).astype(BF16)


def _nsa_proj_kernel(h_ref, g_ref, w_ref, q_ref, cr_ref, slc_ref, slc1_ref, win_ref, win1_ref,
                     gt_ref, xq_ref):
    hn = _rms(h_ref[0], g_ref[...]).astype(BF16)
    q_ref[0] = (_dot(hn, w_ref[:, _C_Q:_C_CMP]) * (NSA_HEAD_DIM ** -0.5)).astype(BF16)
    cr = _dot(hn, w_ref[:, _C_CMP:_C_SLC])
    for p in range(2 * _KVW // V7X_LANES):
        cr_ref[0, p] = cr[:, p * V7X_LANES:(p + 1) * V7X_LANES]
    s = _dot(hn, w_ref[:, _C_SLC:_C_WIN])
    w = _dot(hn, w_ref[:, _C_WIN:_C_GATE])
    for g in range(NSA_KV_GROUPS):
        sg = s[:, g * _PAIR:(g + 1) * _PAIR]
        wg = w[:, g * _PAIR:(g + 1) * _PAIR]
        slc_ref[0, g] = sg.astype(BF16)
        slc1_ref[0, g] = _ones_and_v(sg)
        win_ref[0, g] = wg.astype(BF16)
        win1_ref[0, g] = _ones_and_v(wg)
    gt_ref[0] = _sigmoid(_dot(hn, w_ref[:, _C_GATE:_C_XQ]))
    xq_ref[0] = _dot(hn, w_ref[:, _C_XQ:_C_END]).astype(BF16)


def _nsa_proj(h, g, w):
    batch, seq, d = h.shape
    tm = ROW_TILE
    grp = NSA_KV_GROUPS
    row = lambda b, i: (b, i, 0)
    pair_shape = jax.ShapeDtypeStruct((batch, grp, seq, _PAIR), BF16)
    pair_spec = pl.BlockSpec((1, grp, tm, _PAIR), lambda b, i: (b, 0, i, 0))
    return pl.pallas_call(
        _nsa_proj_kernel,
        out_shape=(jax.ShapeDtypeStruct((batch, seq, _QW), BF16),
                   jax.ShapeDtypeStruct((batch, 2 * _KVW // V7X_LANES, seq, V7X_LANES), F32),
                   pair_shape, pair_shape, pair_shape, pair_shape,
                   jax.ShapeDtypeStruct((batch, seq, V7X_LANES), F32),
                   jax.ShapeDtypeStruct((batch, seq, XATTN_WIDTH), BF16)),
        grid=(batch, seq // tm),
        in_specs=[
            pl.BlockSpec((1, tm, d), row),
            pl.BlockSpec((1, d), lambda b, i: (0, 0)),
            pl.BlockSpec((d, _C_END), lambda b, i: (0, 0)),
        ],
        out_specs=(pl.BlockSpec((1, tm, _QW), row),
                   pl.BlockSpec((1, 2 * _KVW // V7X_LANES, tm, V7X_LANES), lambda b, i: (b, 0, i, 0)),
                   pair_spec, pair_spec, pair_spec, pair_spec,
                   pl.BlockSpec((1, tm, V7X_LANES), row),
                   pl.BlockSpec((1, tm, XATTN_WIDTH), row)),
        compiler_params=pltpu.CompilerParams(
            dimension_semantics=("arbitrary", "arbitrary"),
            vmem_limit_bytes=_vmem_limit(48 * 1024 * 1024)),
        name="nsa_proj",
    )(h, g.reshape(1, d), w)


_CHUNKS = 512


def _compress_kernel(raw_ref, pos_ref, w1_ref, w2_ref, o_ref, h2_scr, *, transposed):
    dk = NSA_HEAD_DIM
    half = CMP_BLOCK // 2
    h1 = [jnp.zeros((_CHUNKS, CMP_HIDDEN), F32) for _ in range(NSA_KV_GROUPS)]
    h2 = [jnp.zeros((_CHUNKS, CMP_HIDDEN), F32) for _ in range(NSA_KV_GROUPS)]
    gpl = V7X_LANES // dk
    for l in range(half):
        for p in range(NSA_KV_GROUPS // gpl):
            lanes = slice(p * V7X_LANES, (p + 1) * V7X_LANES)
            rows = raw_ref[0, p, pl.ds(l, _CHUNKS, stride=CMP_STRIDE), :]
            a1 = (rows + pos_ref[0, l:l + 1, lanes]).astype(BF16)
            a2 = (rows + pos_ref[0, half + l:half + l + 1, lanes]).astype(BF16)
            for gg in range(gpl):
                g = p * gpl + gg
                h1[g] = h1[g] + _dot(a1[:, gg * dk:(gg + 1) * dk], w1_ref[0, l * dk:(l + 1) * dk, :])
                h2[g] = h2[g] + _dot(a2[:, gg * dk:(gg + 1) * dk],
                                     w1_ref[0, (half + l) * dk:(half + l + 1) * dk, :])
    h2_scr[_CHUNKS:_CHUNKS + 8, :] = jnp.zeros((8, CMP_HIDDEN), F32)
    for g in range(NSA_KV_GROUPS):
        h2_scr[0:_CHUNKS, :] = h2[g]
        hm = h1[g] + h2_scr[pl.ds(1, _CHUNKS), :]
        hm = (hm * _sigmoid(hm)).astype(BF16)
        if transposed:
            out_t = _dot_nt(w2_ref[0], hm)
            o_ref[0, g] = jnp.concatenate([jnp.zeros_like(out_t), out_t], axis=0).astype(BF16)
        else:
            out = _dot(hm, w2_ref[0])
            o_ref[0, g] = jnp.concatenate([out, jnp.zeros_like(out)], axis=1).astype(BF16)


def _compress(raw, pos4, w1, w2, which, transposed):
    batch, _, seq, _ = raw.shape
    grp = NSA_KV_GROUPS
    tiles = _KVW // V7X_LANES
    if transposed:
        out_shape = (batch, grp, _PAIR, N_CMP_PAD)
        w2_block = (1, NSA_HEAD_DIM, CMP_HIDDEN)
    else:
        out_shape = (batch, grp, N_CMP_PAD, _PAIR)
        w2_block = (1, CMP_HIDDEN, NSA_HEAD_DIM)
    return pl.pallas_call(
        functools.partial(_compress_kernel, transposed=transposed),
        out_shape=jax.ShapeDtypeStruct(out_shape, BF16),
        grid=(batch,),
        in_specs=[
            pl.BlockSpec((1, tiles, seq, V7X_LANES), lambda b: (b, which, 0, 0)),
            pl.BlockSpec((1, CMP_BLOCK, _KVW), lambda b: (which, 0, 0)),
            pl.BlockSpec((1, CMP_BLOCK * NSA_HEAD_DIM, CMP_HIDDEN), lambda b: (which, 0, 0)),
            pl.BlockSpec(w2_block, lambda b: (0, 0, 0)),
        ],
        out_specs=pl.BlockSpec((1,) + out_shape[1:], lambda b: (b, 0, 0, 0)),
        scratch_shapes=[pltpu.VMEM((_CHUNKS + 8, CMP_HIDDEN), F32)],
        compiler_params=pltpu.CompilerParams(
            dimension_semantics=("arbitrary",),
            vmem_limit_bytes=_vmem_limit(40 * 1024 * 1024)),
        name="compress_v" if transposed else "compress_k",
    )(raw, pos4, w1, w2)


def _t5_bucket(dist):
    n = jnp.maximum(dist, 0)
    max_exact = REL_BUCKETS // 2
    nf = jnp.maximum(n, 1).astype(F32)
    large = max_exact + (jnp.log(nf / max_exact) / np.log(REL_MAX_DIST / max_exact)
                         * (REL_BUCKETS - max_exact)).astype(jnp.int32)
    large = jnp.minimum(large, REL_BUCKETS - 1)
    return jnp.where(n < max_exact, n, large)


def _table_kernel(thr_ref, rb_ref, cbt_ref, tslc_ref, twin_ref):
    tq = Q_TILE
    g = pl.program_id(0)

    def shifted_bias(dist, h):
        val = jnp.full(dist.shape, rb_ref[0, h], F32)
        for k in range(1, REL_BUCKETS):
            val = jnp.where(dist >= thr_ref[k], rb_ref[k, h], val)
        return val - rb_ref[REL_BUCKETS - 1, h]

    def iota2(shape, dim):
        return lax.broadcasted_iota(jnp.int32, shape, dim)

    shape = (2 * N_CMP_PAD, tq)
    d_cmp = iota2(shape, 1) - (CMP_BLOCK - 1) - CMP_STRIDE * (iota2(shape, 0) - N_CMP_PAD)
    shape = (tq, 2 * tq)
    d_slc = iota2(shape, 0) - iota2(shape, 1) + tq
    shape = (tq, WINDOW + tq)
    d_win = iota2(shape, 0) - iota2(shape, 1) + WINDOW
    for j in range(NSA_HPG):
        h = g * NSA_HPG + j
        cbt_ref[0, :, j * tq:(j + 1) * tq] = jnp.where(d_cmp >= 0, shifted_bias(d_cmp, h), NEG)
        tslc_ref[0, j * tq:(j + 1) * tq, :] = jnp.where(d_slc >= 0, shifted_bias(d_slc, h), NEG)
        ok = jnp.where(d_win >= 0, d_win, WINDOW) < WINDOW
        twin_ref[0, j * tq:(j + 1) * tq, :] = jnp.where(ok, shifted_bias(d_win, h), NEG)


def _bias_tables(rel_bias):
    tq = Q_TILE
    rows = NSA_HPG * tq
    buckets = _t5_bucket(jnp.arange(REL_MAX_DIST + 1))
    thr = jnp.sum(buckets[None, :] < jnp.arange(REL_BUCKETS)[:, None], axis=1).astype(jnp.int32)
    smem = pl.BlockSpec(memory_space=pltpu.SMEM)
    return pl.pallas_call(
        _table_kernel,
        out_shape=(jax.ShapeDtypeStruct((NSA_KV_GROUPS, 2 * N_CMP_PAD, rows), F32),
                   jax.ShapeDtypeStruct((NSA_KV_GROUPS, rows, 2 * tq), F32),
                   jax.ShapeDtypeStruct((NSA_KV_GROUPS, rows, WINDOW + tq), F32)),
        grid=(NSA_KV_GROUPS,),
        in_specs=[smem, smem],
        out_specs=(pl.BlockSpec((1, 2 * N_CMP_PAD, rows), lambda g: (g, 0, 0)),
                   pl.BlockSpec((1, rows, 2 * tq), lambda g: (g, 0, 0)),
                   pl.BlockSpec((1, rows, WINDOW + tq), lambda g: (g, 0, 0))),
        compiler_params=pltpu.CompilerParams(
            dimension_semantics=("arbitrary",),
            vmem_limit_bytes=_vmem_limit(40 * 1024 * 1024)),
        name="bias_tables",
    )(thr, rel_bias.astype(F32))


def _block_mask_rows(seq):
    pos = jnp.arange(seq, dtype=jnp.int32)[:, None]
    col = jnp.arange(N_SEL, dtype=jnp.int32)[None, :]
    onehot = jnp.where(col == pos // SEL_BLOCK, NEG, 0.0)
    return jnp.concatenate([onehot, jnp.full((Q_TILE, N_SEL), NEG), jnp.zeros((Q_TILE, N_SEL))],
                           axis=0).astype(BF16)


def _band_probs(s, m_rep):
    return jnp.exp(s - jnp.concatenate([m_rep] * (s.shape[1] // V7X_LANES), axis=1))


def _row_max_rep(s):
    return jnp.broadcast_to(jnp.max(s, axis=-1, keepdims=True), (s.shape[0], V7X_LANES))


def _normalise(acc):
    return acc * pltpu.roll(1.0 / acc, NSA_HEAD_DIM, axis=1)


def _take_lane(x, idx):
    lane = lax.broadcasted_iota(jnp.int32, x.shape, 1)
    return jnp.sum(jnp.where(lane == idx, x, 0.0), axis=-1, keepdims=True)


def _nsa_kernel(q_ref, kc_ref, vct_ref, slc_ref, slc1_ref, win_ref, win1_ref, bm_ref, gt_ref,
                cbt_ref, tslc_ref, twin_ref, o_ref, ps_scr, lhs_scr, m_scr, acc_scr):
    tq = Q_TILE
    hpg = NSA_HPG
    dk = NSA_HEAD_DIM
    seq = slc_ref.shape[2]
    g = pl.program_id(1)
    i = pl.program_id(2)
    q0 = pl.multiple_of(i * tq, tq)
    bands = [slice(j * tq, (j + 1) * tq) for j in range(hpg)]

    qb = q_ref[0]
    lane = lax.broadcasted_iota(jnp.int32, (tq, _PAIR), 1)
    for j in range(hpg):
        blk = qb[:, (j // 2) * _PAIR:(j // 2 + 1) * _PAIR].astype(F32)
        if j % 2:
            blk = pltpu.roll(blk, dk, axis=1)
        lhs_scr[bands[j], 0:_PAIR] = jnp.where(lane < dk, blk, 0.0).astype(BF16)
    qs = lhs_scr[:, 0:_PAIR]

    st = _dot_nt(kc_ref[0, 0], qs)
    st = st + cbt_ref[0, pl.ds(pl.multiple_of(N_CMP_PAD - (tq // CMP_STRIDE) * i, 8), N_CMP_PAD), :]
    mx = jnp.max(st, axis=0, keepdims=True)
    e = jnp.where(st > 0.5 * NEG, jnp.exp(st - mx), 0.0)
    den = jnp.sum(e, axis=0, keepdims=True)
    pt = e * jnp.where(den > 0.0, 1.0 / den, 0.0)
    o_cmp_t = _dot(vct_ref[0, 0], pt.astype(BF16))
    psum = pt[:, bands[0]]
    for j in range(1, hpg):
        psum = psum + pt[:, bands[j]]

    ratio = SEL_BLOCK // CMP_STRIDE
    imp_tiles = []
    for c in range(tq // V7X_LANES):
        ps_scr[c, 0:8, :] = jnp.zeros((8, V7X_LANES), F32)
        ps_scr[c, 8:8 + N_CMP_PAD, :] = psum[:, c * V7X_LANES:(c + 1) * V7X_LANES]
        ps_scr[c, 8 + N_CMP_PAD:16 + N_CMP_PAD, :] = jnp.zeros((8, V7X_LANES), F32)
        taps = [ps_scr[c, pl.ds(7 + r, N_SEL, stride=ratio), :] for r in range(5)]
        imp_tiles.append(0.5 * taps[0] + taps[1] + taps[2] + taps[3] + 0.5 * taps[4])
    imp = jnp.concatenate(imp_tiles, axis=1)

    jrow = lax.broadcasted_iota(jnp.int32, (N_SEL, tq), 0)
    cur = (q0 + lax.broadcasted_iota(jnp.int32, (N_SEL, tq), 1)) // SEL_BLOCK
    imp = jnp.where(jrow <= cur, imp, -SEL_FORCE)
    for forced_block in (0, cur, cur - 1):
        imp = jnp.where(jrow == forced_block, SEL_FORCE, imp)
    not_sel_t = jnp.ones((N_SEL, tq), F32)
    for _ in range(SEL_TOPK):
        top = jnp.max(imp, axis=0, keepdims=True)
        first = jnp.min(jnp.where(imp == top, jrow, N_SEL), axis=0, keepdims=True)
        pick = jrow == first
        not_sel_t = jnp.where(pick, 0.0, not_sel_t)
        imp = jnp.where(pick, -3.0e38, imp)
    not_sel = not_sel_t.T.astype(BF16)
    for j in range(hpg):
        lhs_scr[bands[j], _PAIR:2 * _PAIR] = not_sel
    lhs = lhs_scr[...]

    def key_tile(kv_ref, start, mask_start):
        return jnp.concatenate([kv_ref[0, 0, pl.ds(start, tq), :], bm_ref[pl.ds(mask_start, tq), :]],
                               axis=1)

    prev = pl.multiple_of(jnp.maximum(q0 - tq, 0), tq)
    prev_mask = pl.multiple_of(jnp.where(i == 0, seq, q0 - tq), tq)
    rhs = jnp.concatenate([key_tile(slc_ref, prev, prev_mask), key_tile(slc_ref, q0, q0)], axis=0)
    v1 = jnp.concatenate([slc1_ref[0, 0, pl.ds(prev, tq), :], slc1_ref[0, 0, pl.ds(q0, tq), :]], axis=0)
    s = _dot_nt(lhs, rhs) + tslc_ref[0]
    for j in range(hpg):
        m_rep = _row_max_rep(s[bands[j]])
        p = _band_probs(s[bands[j]], m_rep)
        m_scr[bands[j], :] = m_rep
        acc_scr[bands[j], :] = _dot(p.astype(BF16), v1)

    def far_body(kt, carry):
        start = pl.multiple_of(kt * KV_TILE, KV_TILE)
        rhs = jnp.concatenate([slc_ref[0, 0, pl.ds(start, KV_TILE), :],
                               bm_ref[pl.ds(start, KV_TILE), :]], axis=1)
        v1 = slc1_ref[0, 0, pl.ds(start, KV_TILE), :]
        s = _dot_nt(lhs_scr[...], rhs)
        for j in range(hpg):
            m_old = m_scr[bands[j], :]
            m_new = jnp.maximum(m_old, _row_max_rep(s[bands[j]]))
            p = _band_probs(s[bands[j]], m_new)
            acc_scr[bands[j], :] = (jnp.exp(m_old - m_new) * acc_scr[bands[j], :]
                                    + _dot(p.astype(BF16), v1))
            m_scr[bands[j], :] = m_new
        return carry

    lax.fori_loop(0, jnp.maximum(i - 1, 0) * (tq // KV_TILE), far_body, 0)
    o_slc = _normalise(acc_scr[...])

    n_prev = WINDOW // tq
    tiles, v1s = [], []
    for r in range(n_prev, 0, -1):
        start = pl.multiple_of(jnp.maximum(q0 - r * tq, 0), tq)
        mask_start = pl.multiple_of(jnp.where(i < r, seq, seq + tq), tq)
        tiles.append(key_tile(win_ref, start, mask_start))
        v1s.append(win1_ref[0, 0, pl.ds(start, tq), :])
    tiles.append(key_tile(win_ref, q0, seq + tq))
    v1s.append(win1_ref[0, 0, pl.ds(q0, tq), :])
    rhs = jnp.concatenate(tiles, axis=0)
    v1 = jnp.concatenate(v1s, axis=0)
    s = _dot_nt(lhs, rhs) + twin_ref[0]
    accs = []
    for j in range(hpg):
        p = _band_probs(s[bands[j]], _row_max_rep(s[bands[j]]))
        accs.append(_dot(p.astype(BF16), v1))
    o_win = _normalise(jnp.concatenate(accs, axis=0))

    gt = gt_ref[0]
    heads = []
    for j in range(hpg):
        h = g * hpg + j
        oc = o_cmp_t[:, bands[j]].T
        heads.append(_take_lane(gt, h) * oc + _take_lane(gt, NSA_HEADS + h) * o_slc[bands[j]]
                     + _take_lane(gt, 2 * NSA_HEADS + h) * o_win[bands[j]])
    pairs = []
    for jj in range(hpg // 2):
        even = pltpu.roll(heads[2 * jj], dk, axis=1)
        pairs.append(jnp.where(lane < dk, even, heads[2 * jj + 1]))
    o_ref[0] = jnp.concatenate(pairs, axis=1).astype(BF16)


def _nsa_attention(q, kc, vct, slc, slc1, win, win1, block_mask, gates, cbt, tslc, twin):
    batch, seq, _ = q.shape
    grp = NSA_KV_GROUPS
    tq = Q_TILE
    rows = NSA_HPG * tq
    per_group = lambda b, g, i: (b, g, 0, 0)
    return pl.pallas_call(
        _nsa_kernel,
        out_shape=jax.ShapeDtypeStruct((batch, seq, _QW), BF16),
        grid=(batch, grp, seq // tq),
        in_specs=[
            pl.BlockSpec((1, tq, NSA_HPG * NSA_HEAD_DIM), lambda b, g, i: (b, i, g)),
            _resident((1, 1, N_CMP_PAD, _PAIR), per_group),
            _resident((1, 1, _PAIR, N_CMP_PAD), per_group),
            _resident((1, 1, seq, _PAIR), per_group),
            _resident((1, 1, seq, _PAIR), per_group),
            _resident((1, 1, seq, _PAIR), per_group),
            _resident((1, 1, seq, _PAIR), per_group),
            _resident((seq + 2 * tq, N_SEL), lambda b, g, i: (0, 0)),
            pl.BlockSpec((1, tq, V7X_LANES), lambda b, g, i: (b, i, 0)),
            _resident((1, 2 * N_CMP_PAD, rows), lambda b, g, i: (g, 0, 0)),
            _resident((1, rows, 2 * tq), lambda b, g, i: (g, 0, 0)),
            _resident((1, rows, WINDOW + tq), lambda b, g, i: (g, 0, 0)),
        ],
        out_specs=pl.BlockSpec((1, tq, NSA_HPG * NSA_HEAD_DIM), lambda b, g, i: (b, i, g)),
        scratch_shapes=[pltpu.VMEM((tq // V7X_LANES, N_CMP_PAD + 16, V7X_LANES), F32),
                        pltpu.VMEM((rows, 2 * _PAIR), BF16),
                        pltpu.VMEM((rows, V7X_LANES), F32),
                        pltpu.VMEM((rows, V7X_LANES), F32)],
        compiler_params=pltpu.CompilerParams(
            dimension_semantics=("arbitrary", "arbitrary", "arbitrary"),
            vmem_limit_bytes=_vmem_limit(52 * 1024 * 1024)),
        name="nsa_attention",
    )(q, kc, vct, slc, slc1, win, win1, block_mask, gates, cbt, tslc, twin)


def _nsa_weight(w):
    d = w.shape[0]
    q = w[:, :_QW]
    kv = w[:, _QW:_QW + 6 * _KVW].reshape(d, 6, NSA_KV_GROUPS, NSA_HEAD_DIM)
    cmp_raw = kv[:, 0:2].reshape(d, 2 * _KVW)
    slc = jnp.stack([kv[:, 2], kv[:, 3]], axis=2).reshape(d, NSA_KV_GROUPS * _PAIR)
    win = jnp.stack([kv[:, 4], kv[:, 5]], axis=2).reshape(d, NSA_KV_GROUPS * _PAIR)
    gates = w[:, _QW + 6 * _KVW:_QW + 6 * _KVW + 3 * NSA_HEADS]
    gates = jnp.pad(gates, ((0, 0), (0, V7X_LANES - 3 * NSA_HEADS)))
    xq = w[:, _QW + 6 * _KVW + 3 * NSA_HEADS:]
    return jnp.concatenate([q, cmp_raw, slc, win, gates, xq], axis=1).astype(BF16)


def kernel(x, mem, norm_mix_g, norm_mem_g, norm_ffn_g, norm_final_g, w_in_conv, conv_w, w_in_nsa,
           cmp_pos_k, cmp_w1_k, cmp_w2_k, cmp_pos_v, cmp_w1_v, cmp_w2_v, rel_bias, w_mem_kv, w_out,
           w_ffn_in, w_ffn_out):
    seq = x.shape[1]
    assert seq % ROW_TILE == 0 and seq % Q_TILE == 0 and x.shape[2] == D_MODEL
    assert seq == CMP_STRIDE * _CHUNKS and seq == SEL_BLOCK * N_SEL and Q_TILE == KV_TILE
    memkv = _memkv(mem, norm_mem_g, w_mem_kv.astype(BF16))
    w_out_b = w_out.astype(BF16)
    w_ffn_in_b = w_ffn_in.astype(BF16)
    w_ffn_out_b = w_ffn_out.astype(BF16)

    tok, xq = _conv_front(x, norm_mix_g[0], w_in_conv[0].astype(BF16), conv_w[0])
    h = _mixout(x, tok, xq, memkv[0], w_out_b[0])
    h = _ffn(h, norm_ffn_g[0], w_ffn_in_b[0], w_ffn_out_b[0], norm_final_g, final=False)

    q, cmp_raw, slc, slc1, win, win1, gates, xq = _nsa_proj(h, norm_mix_g[1], _nsa_weight(w_in_nsa[0]))
    pos4 = jnp.stack([jnp.tile(cmp_pos_k[0], (1, NSA_KV_GROUPS)),
                      jnp.tile(cmp_pos_v[0], (1, NSA_KV_GROUPS))])
    w1 = jnp.stack([cmp_w1_k[0], cmp_w1_v[0]]).astype(BF16)
    kc = _compress(cmp_raw, pos4, w1, cmp_w2_k[0].astype(BF16)[None], 0, transposed=False)
    vct = _compress(cmp_raw, pos4, w1, cmp_w2_v[0].T.astype(BF16)[None], 1, transposed=True)
    tok = _nsa_attention(q, kc, vct, slc, slc1, win, win1, _block_mask_rows(seq), gates,
                         *_bias_tables(rel_bias))
    h = _mixout(h, tok, xq, memkv[1], w_out_b[1])
    return _ffn(h, norm_ffn_g[1], w_ffn_in_b[1], w_ffn_out_b[1], norm_final_g, final=True)
```

```python
import functools

import numpy as np
import jax
import jax.numpy as jnp
from jax import lax
from jax.experimental import pallas as pl
from jax.experimental.pallas import tpu as pltpu

F32 = jnp.float32
BF16 = jnp.bfloat16

D_MODEL = 1024
N_MEM = 256
CONV_WIDTH = 3
NSA_HEADS = 16
NSA_HEAD_DIM = 64
NSA_KV_GROUPS = 4
NSA_HPG = NSA_HEADS // NSA_KV_GROUPS
CMP_BLOCK = 32
CMP_STRIDE = 16
CMP_HIDDEN = 2 * NSA_HEAD_DIM
SEL_BLOCK = 64
SEL_TOPK = 16
SEL_FORCE = 1.0e4
WINDOW = 512
XATTN_HEADS = 4
XATTN_HEAD_DIM = 128
XATTN_WIDTH = XATTN_HEADS * XATTN_HEAD_DIM
FFN_HIDDEN = 2816
REL_BUCKETS = 32
REL_MAX_DIST = 128
RMS_EPS = 1e-6
NEG = -1e30

V7X_LANES = 128
V7X_VMEM_BYTES = 64 * 1024 * 1024

ROW_TILE = 512
FFN_ROW_TILE = 256
FFN_CHUNK = 1408
Q_TILE = 256
KV_TILE = 256
N_CMP_PAD = 512
N_SEL = 128

_NT = (((1,), (1,)), ((), ()))


def _vmem_limit(nbytes):
    return int(min(max(nbytes, 16 * 1024 * 1024), V7X_VMEM_BYTES - 8 * 1024 * 1024))


def _rms(x, g):
    ms = jnp.mean(x * x, axis=-1, keepdims=True)
    return x * lax.rsqrt(ms + RMS_EPS) * g


def _dot(a, b):
    return jnp.dot(a, b, preferred_element_type=F32)


def _dot_nt(a, b):
    return lax.dot_general(a, b, _NT, preferred_element_type=F32)


def _sigmoid(x):
    return 1.0 / (1.0 + jnp.exp(-x))


def _resident(block_shape, index_map):
    return pl.BlockSpec(block_shape, index_map, pipeline_mode=pl.Buffered(1))


def _memkv_kernel(mem_ref, g_ref, w_ref, o_ref):
    xn = _rms(mem_ref[0], g_ref[0]).astype(BF16)
    o_ref[0, 0] = _dot(xn, w_ref[0]).astype(BF16)


def _memkv(mem, gains, w):
    depth, batch = w.shape[0], mem.shape[0]
    return pl.pallas_call(
        _memkv_kernel,
        out_shape=jax.ShapeDtypeStruct((depth, batch, N_MEM, 2 * XATTN_WIDTH), BF16),
        grid=(depth, batch),
        in_specs=[
            pl.BlockSpec((1, N_MEM, D_MODEL), lambda l, b: (b, 0, 0)),
            pl.BlockSpec((1, 1, D_MODEL), lambda l, b: (l, 0, 0)),
            pl.BlockSpec((1, D_MODEL, 2 * XATTN_WIDTH), lambda l, b: (l, 0, 0)),
        ],
        out_specs=pl.BlockSpec((1, 1, N_MEM, 2 * XATTN_WIDTH), lambda l, b: (l, b, 0, 0)),
        compiler_params=pltpu.CompilerParams(
            dimension_semantics=("arbitrary", "arbitrary"),
            vmem_limit_bytes=_vmem_limit(24 * 1024 * 1024)),
        name="memkv",
    )(mem, gains.reshape(depth, 1, D_MODEL), w)


def _conv_kernel(h_ref, hp_ref, g_ref, w_ref, cw_ref, tok_ref, xq_ref):
    d = D_MODEL
    i = pl.program_id(1)
    g = g_ref[...]
    hn = _rms(h_ref[0], g).astype(BF16)
    u_b = _dot(hn, w_ref[:, 0:d])
    v = _dot(hn, w_ref[:, d:2 * d]) * _dot(hn, w_ref[:, 2 * d:3 * d])
    xq_ref[0] = _dot(hn, w_ref[:, 3 * d:]).astype(BF16)
    hpn = _rms(hp_ref[0], g).astype(BF16)
    vp = _dot(hpn, w_ref[:, d:2 * d]) * _dot(hpn, w_ref[:, 2 * d:3 * d])
    vp = jnp.where(i > 0, vp, 0.0)
    row = lax.broadcasted_iota(jnp.int32, v.shape, 0)
    v1 = jnp.where(row == 0, vp[7:8], pltpu.roll(v, 1, axis=0))
    v2 = jnp.where(row == 0, vp[6:7], jnp.where(row == 1, vp[7:8], pltpu.roll(v, 2, axis=0)))
    cw = cw_ref[...]
    conv = v2 * cw[0:1] + v1 * cw[1:2] + v * cw[2:3]
    tok_ref[0] = (u_b * conv).astype(BF16)


def _conv_front(h, g, w, cw):
    batch, seq, d = h.shape
    tm = ROW_TILE
    width = w.shape[1]
    halo = 8
    return pl.pallas_call(
        _conv_kernel,
        out_shape=(jax.ShapeDtypeStruct((batch, seq, d), BF16),
                   jax.ShapeDtypeStruct((batch, seq, XATTN_WIDTH), BF16)),
        grid=(batch, seq // tm),
        in_specs=[
            pl.BlockSpec((1, tm, d), lambda b, i: (b, i, 0)),
            pl.BlockSpec((1, halo, d), lambda b, i: (b, jnp.maximum(i * (tm // halo) - 1, 0), 0)),
            pl.BlockSpec((1, d), lambda b, i: (0, 0)),
            pl.BlockSpec((d, width), lambda b, i: (0, 0)),
            pl.BlockSpec((CONV_WIDTH, d), lambda b, i: (0, 0)),
        ],
        out_specs=(pl.BlockSpec((1, tm, d), lambda b, i: (b, i, 0)),
                   pl.BlockSpec((1, tm, XATTN_WIDTH), lambda b, i: (b, i, 0))),
        compiler_params=pltpu.CompilerParams(
            dimension_semantics=("arbitrary", "arbitrary"),
            vmem_limit_bytes=_vmem_limit(48 * 1024 * 1024)),
        name="conv_front",
    )(h, h, g.reshape(1, d), w, cw)


def _mixout_kernel(h_ref, tok_ref, xq_ref, kv_ref, w_ref, o_ref):
    d = D_MODEL
    acc = h_ref[0] + _dot(tok_ref[0], w_ref[0:d])
    xq = xq_ref[0]
    kv = kv_ref[0]
    scale = XATTN_HEAD_DIM ** -0.5
    outs = []
    for hh in range(XATTN_HEADS):
        lo = hh * XATTN_HEAD_DIM
        q = xq[:, lo:lo + XATTN_HEAD_DIM]
        k = kv[:, lo:lo + XATTN_HEAD_DIM]
        v = kv[:, XATTN_WIDTH + lo:XATTN_WIDTH + lo + XATTN_HEAD_DIM]
        s = _dot_nt(q, k) * scale
        e = jnp.exp(s - jnp.max(s, axis=-1, keepdims=True))
        p = e / jnp.sum(e, axis=-1, keepdims=True)
        outs.append(_dot(p.astype(BF16), v))
    mo = jnp.concatenate(outs, axis=-1).astype(BF16)
    o_ref[0] = acc + _dot(mo, w_ref[d:])


def _mixout(h, tok, xq, kv, w):
    batch, seq, d = h.shape
    tm = ROW_TILE
    return pl.pallas_call(
        _mixout_kernel,
        out_shape=jax.ShapeDtypeStruct((batch, seq, d), F32),
        grid=(batch, seq // tm),
        in_specs=[
            pl.BlockSpec((1, tm, d), lambda b, i: (b, i, 0)),
            pl.BlockSpec((1, tm, d), lambda b, i: (b, i, 0)),
            pl.BlockSpec((1, tm, XATTN_WIDTH), lambda b, i: (b, i, 0)),
            pl.BlockSpec((1, N_MEM, 2 * XATTN_WIDTH), lambda b, i: (b, 0, 0)),
            pl.BlockSpec((d + XATTN_WIDTH, d), lambda b, i: (0, 0)),
        ],
        out_specs=pl.BlockSpec((1, tm, d), lambda b, i: (b, i, 0)),
        compiler_params=pltpu.CompilerParams(
            dimension_semantics=("arbitrary", "arbitrary"),
            vmem_limit_bytes=_vmem_limit(40 * 1024 * 1024)),
        name="mixout",
    )(h, tok, xq, kv, w)


def _ffn_kernel(h_ref, g_ref, w1_ref, w2_ref, gf_ref, o_ref, *, final):
    h = h_ref[0]
    hn = _rms(h, g_ref[...]).astype(BF16)
    acc = h
    for c in range(FFN_HIDDEN // FFN_CHUNK):
        lo = c * FFN_CHUNK
        a = _dot(hn, w1_ref[:, lo:lo + FFN_CHUNK])
        u = _dot(hn, w1_ref[:, FFN_HIDDEN + lo:FFN_HIDDEN + lo + FFN_CHUNK])
        act = (a * _sigmoid(a) * u).astype(BF16)
        acc = acc + _dot(act, w2_ref[lo:lo + FFN_CHUNK])
    if final:
        acc = _rms(acc, gf_ref[...])
    o_ref[0] = acc


def _ffn(h, g, w1, w2, gf, final):
    batch, seq, d = h.shape
    tm = FFN_ROW_TILE
    return pl.pallas_call(
        functools.partial(_ffn_kernel, final=final),
        out_shape=jax.ShapeDtypeStruct((batch, seq, d), F32),
        grid=(batch, seq // tm),
        in_specs=[
            pl.BlockSpec((1, tm, d), lambda b, i: (b, i, 0)),
            pl.BlockSpec((1, d), lambda b, i: (0, 0)),
            pl.BlockSpec((d, 2 * FFN_HIDDEN), lambda b, i: (0, 0)),
            pl.BlockSpec((FFN_HIDDEN, d), lambda b, i: (0, 0)),
            pl.BlockSpec((1, d), lambda b, i: (0, 0)),
        ],
        out_specs=pl.BlockSpec((1, tm, d), lambda b, i: (b, i, 0)),
        compiler_params=pltpu.CompilerParams(
            dimension_semantics=("arbitrary", "arbitrary"),
            vmem_limit_bytes=_vmem_limit(52 * 1024 * 1024)),
        name="ffn",
    )(h, g.reshape(1, d), w1, w2, gf.reshape(1, d))


_QW = NSA_HEADS * NSA_HEAD_DIM
_KVW = NSA_KV_GROUPS * NSA_HEAD_DIM
_PAIR = 2 * NSA_HEAD_DIM
_C_Q = 0
_C_CMP = _C_Q + _QW
_C_SLC = _C_CMP + 2 * _KVW
_C_WIN = _C_SLC + NSA_KV_GROUPS * _PAIR
_C_GATE = _C_WIN + NSA_KV_GROUPS * _PAIR
_C_XQ = _C_GATE + V7X_LANES
_C_END = _C_XQ + XATTN_WIDTH


def _ones_and_vt(kv):
    t = kv.T
    row = lax.broadcasted_iota(jnp.int32, t.shape, 0)
    return jnp.where(row < NSA_HEAD_DIM, 1.0, t).astype(BF16)


def _nsa_proj_kernel(h_ref, g_ref, w_ref, q_ref, cr_ref, slc_ref, slc1_ref, win_ref, win1_ref,
                     gt_ref, xq_ref):
    hn = _rms(h_ref[0], g_ref[...]).astype(BF16)
    q_ref[0] = (_dot(hn, w_ref[:, _C_Q:_C_CMP]) * (NSA_HEAD_DIM ** -0.5)).astype(BF16)
    cr = _dot(hn, w_ref[:, _C_CMP:_C_SLC])
    for p in range(2 * _KVW // V7X_LANES):
        cr_ref[0, p] = cr[:, p * V7X_LANES:(p + 1) * V7X_LANES]
    s = _dot(hn, w_ref[:, _C_SLC:_C_WIN])
    w = _dot(hn, w_ref[:, _C_WIN:_C_GATE])
    for g in range(NSA_KV_GROUPS):
        sg = s[:, g * _PAIR:(g + 1) * _PAIR]
        wg = w[:, g * _PAIR:(g + 1) * _PAIR]
        slc_ref[0, g] = sg.astype(BF16)
        win_ref[0, g] = wg.astype(BF16)
        for c in range(ROW_TILE // KV_TILE):
            rows = slice(c * KV_TILE, (c + 1) * KV_TILE)
            slc1_ref[0, g, c] = _ones_and_vt(sg[rows])
            win1_ref[0, g, c] = _ones_and_vt(wg[rows])
    gt_ref[0] = _sigmoid(_dot(hn, w_ref[:, _C_GATE:_C_XQ]))
    xq_ref[0] = _dot(hn, w_ref[:, _C_XQ:_C_END]).astype(BF16)


def _nsa_proj(h, g, w):
    batch, seq, d = h.shape
    tm = ROW_TILE
    grp = NSA_KV_GROUPS
    row = lambda b, i: (b, i, 0)
    pair_shape = jax.ShapeDtypeStruct((batch, grp, seq, _PAIR), BF16)
    pair_spec = pl.BlockSpec((1, grp, tm, _PAIR), lambda b, i: (b, 0, i, 0))
    vt_shape = jax.ShapeDtypeStruct((batch, grp, seq // KV_TILE, _PAIR, KV_TILE), BF16)
    vt_spec = pl.BlockSpec((1, grp, tm // KV_TILE, _PAIR, KV_TILE), lambda b, i: (b, 0, i, 0, 0))
    return pl.pallas_call(
        _nsa_proj_kernel,
        out_shape=(jax.ShapeDtypeStruct((batch, seq, _QW), BF16),
                   jax.ShapeDtypeStruct((batch, 2 * _KVW // V7X_LANES, seq, V7X_LANES), F32),
                   pair_shape, vt_shape, pair_shape, vt_shape,
                   jax.ShapeDtypeStruct((batch, seq, V7X_LANES), F32),
                   jax.ShapeDtypeStruct((batch, seq, XATTN_WIDTH), BF16)),
        grid=(batch, seq // tm),
        in_specs=[
            pl.BlockSpec((1, tm, d), row),
            pl.BlockSpec((1, d), lambda b, i: (0, 0)),
            pl.BlockSpec((d, _C_END), lambda b, i: (0, 0)),
        ],
        out_specs=(pl.BlockSpec((1, tm, _QW), row),
                   pl.BlockSpec((1, 2 * _KVW // V7X_LANES, tm, V7X_LANES), lambda b, i: (b, 0, i, 0)),
                   pair_spec, vt_spec, pair_spec, vt_spec,
                   pl.BlockSpec((1, tm, V7X_LANES), row),
                   pl.BlockSpec((1, tm, XATTN_WIDTH), row)),
        compiler_params=pltpu.CompilerParams(
            dimension_semantics=("arbitrary", "arbitrary"),
            vmem_limit_bytes=_vmem_limit(48 * 1024 * 1024)),
        name="nsa_proj",
    )(h, g.reshape(1, d), w)


_CHUNKS = 512


def _compress_kernel(raw_ref, pos_ref, w1_ref, w2_ref, o_ref, h2_scr, *, transposed):
    dk = NSA_HEAD_DIM
    half = CMP_BLOCK // 2
    h1 = [jnp.zeros((_CHUNKS, CMP_HIDDEN), F32) for _ in range(NSA_KV_GROUPS)]
    h2 = [jnp.zeros((_CHUNKS, CMP_HIDDEN), F32) for _ in range(NSA_KV_GROUPS)]
    gpl = V7X_LANES // dk
    for l in range(half):
        for p in range(NSA_KV_GROUPS // gpl):
            lanes = slice(p * V7X_LANES, (p + 1) * V7X_LANES)
            rows = raw_ref[0, p, pl.ds(l, _CHUNKS, stride=CMP_STRIDE), :]
            a1 = (rows + pos_ref[0, l:l + 1, lanes]).astype(BF16)
            a2 = (rows + pos_ref[0, half + l:half + l + 1, lanes]).astype(BF16)
            for gg in range(gpl):
                g = p * gpl + gg
                h1[g] = h1[g] + _dot(a1[:, gg * dk:(gg + 1) * dk], w1_ref[0, l * dk:(l + 1) * dk, :])
                h2[g] = h2[g] + _dot(a2[:, gg * dk:(gg + 1) * dk],
                                     w1_ref[0, (half + l) * dk:(half + l + 1) * dk, :])
    h2_scr[_CHUNKS:_CHUNKS + 8, :] = jnp.zeros((8, CMP_HIDDEN), F32)
    for g in range(NSA_KV_GROUPS):
        h2_scr[0:_CHUNKS, :] = h2[g]
        hm = h1[g] + h2_scr[pl.ds(1, _CHUNKS), :]
        hm = (hm * _sigmoid(hm)).astype(BF16)
        if transposed:
            out_t = _dot_nt(w2_ref[0], hm)
            o_ref[0, g] = jnp.concatenate([jnp.zeros_like(out_t), out_t], axis=0).astype(BF16)
        else:
            out = _dot(hm, w2_ref[0])
            o_ref[0, g] = jnp.concatenate([out, jnp.zeros_like(out)], axis=1).astype(BF16)


def _compress(raw, pos4, w1, w2, which, transposed):
    batch, _, seq, _ = raw.shape
    grp = NSA_KV_GROUPS
    tiles = _KVW // V7X_LANES
    if transposed:
        out_shape = (batch, grp, _PAIR, N_CMP_PAD)
        w2_block = (1, NSA_HEAD_DIM, CMP_HIDDEN)
    else:
        out_shape = (batch, grp, N_CMP_PAD, _PAIR)
        w2_block = (1, CMP_HIDDEN, NSA_HEAD_DIM)
    return pl.pallas_call(
        functools.partial(_compress_kernel, transposed=transposed),
        out_shape=jax.ShapeDtypeStruct(out_shape, BF16),
        grid=(batch,),
        in_specs=[
            pl.BlockSpec((1, tiles, seq, V7X_LANES), lambda b: (b, which, 0, 0)),
            pl.BlockSpec((1, CMP_BLOCK, _KVW), lambda b: (which, 0, 0)),
            pl.BlockSpec((1, CMP_BLOCK * NSA_HEAD_DIM, CMP_HIDDEN), lambda b: (which, 0, 0)),
            pl.BlockSpec(w2_block, lambda b: (0, 0, 0)),
        ],
        out_specs=pl.BlockSpec((1,) + out_shape[1:], lambda b: (b, 0, 0, 0)),
        scratch_shapes=[pltpu.VMEM((_CHUNKS + 8, CMP_HIDDEN), F32)],
        compiler_params=pltpu.CompilerParams(
            dimension_semantics=("arbitrary",),
            vmem_limit_bytes=_vmem_limit(40 * 1024 * 1024)),
        name="compress_v" if transposed else "compress_k",
    )(raw, pos4, w1, w2)


def _t5_bucket(dist):
    n = jnp.maximum(dist, 0)
    max_exact = REL_BUCKETS // 2
    nf = jnp.maximum(n, 1).astype(F32)
    large = max_exact + (jnp.log(nf / max_exact) / np.log(REL_MAX_DIST / max_exact)
                         * (REL_BUCKETS - max_exact)).astype(jnp.int32)
    large = jnp.minimum(large, REL_BUCKETS - 1)
    return jnp.where(n < max_exact, n, large)


def _table_kernel(thr_ref, rb_ref, cbt_ref, tslc_ref, twin_ref):
    tq = Q_TILE
    g = pl.program_id(0)

    def shifted_bias(dist, h):
        val = jnp.full(dist.shape, rb_ref[0, h], F32)
        for k in range(1, REL_BUCKETS):
            val = jnp.where(dist >= thr_ref[k], rb_ref[k, h], val)
        return val - rb_ref[REL_BUCKETS - 1, h]

    def iota2(shape, dim):
        return lax.broadcasted_iota(jnp.int32, shape, dim)

    shape = (2 * N_CMP_PAD, tq)
    d_cmp = iota2(shape, 1) - (CMP_BLOCK - 1) - CMP_STRIDE * (iota2(shape, 0) - N_CMP_PAD)
    shape = (2 * tq, tq)
    d_slc = iota2(shape, 1) - iota2(shape, 0) + tq
    shape = (WINDOW + tq, tq)
    d_win = iota2(shape, 1) - iota2(shape, 0) + WINDOW
    for j in range(NSA_HPG):
        h = g * NSA_HPG + j
        band = slice(j * tq, (j + 1) * tq)
        cbt_ref[0, :, band] = jnp.where(d_cmp >= 0, shifted_bias(d_cmp, h), NEG)
        tslc_ref[0, :, band] = jnp.where(d_slc >= 0, shifted_bias(d_slc, h), NEG)
        ok = jnp.where(d_win >= 0, d_win, WINDOW) < WINDOW
        twin_ref[0, :, band] = jnp.where(ok, shifted_bias(d_win, h), NEG)


def _bias_tables(rel_bias):
    tq = Q_TILE
    rows = NSA_HPG * tq
    buckets = _t5_bucket(jnp.arange(REL_MAX_DIST + 1))
    thr = jnp.sum(buckets[None, :] < jnp.arange(REL_BUCKETS)[:, None], axis=1).astype(jnp.int32)
    smem = pl.BlockSpec(memory_space=pltpu.SMEM)
    return pl.pallas_call(
        _table_kernel,
        out_shape=(jax.ShapeDtypeStruct((NSA_KV_GROUPS, 2 * N_CMP_PAD, rows), F32),
                   jax.ShapeDtypeStruct((NSA_KV_GROUPS, 2 * tq, rows), F32),
                   jax.ShapeDtypeStruct((NSA_KV_GROUPS, WINDOW + tq, rows), F32)),
        grid=(NSA_KV_GROUPS,),
        in_specs=[smem, smem],
        out_specs=(pl.BlockSpec((1, 2 * N_CMP_PAD, rows), lambda g: (g, 0, 0)),
                   pl.BlockSpec((1, 2 * tq, rows), lambda g: (g, 0, 0)),
                   pl.BlockSpec((1, WINDOW + tq, rows), lambda g: (g, 0, 0))),
        compiler_params=pltpu.CompilerParams(
            dimension_semantics=("arbitrary",),
            vmem_limit_bytes=_vmem_limit(40 * 1024 * 1024)),
        name="bias_tables",
    )(thr, rel_bias.astype(F32))


def _block_mask_rows(seq):
    pos = jnp.arange(seq, dtype=jnp.int32)[:, None]
    col = jnp.arange(N_SEL, dtype=jnp.int32)[None, :]
    onehot = jnp.where(col == pos // SEL_BLOCK, NEG, 0.0)
    return jnp.concatenate([onehot, jnp.full((Q_TILE, N_SEL), NEG), jnp.zeros((Q_TILE, N_SEL))],
                           axis=0).astype(BF16)


def _col_softmax(s):
    m = jnp.max(s, axis=0, keepdims=True)
    return m, jnp.exp(s - m)


def _normalise_t(acc):
    return acc[NSA_HEAD_DIM:] * (1.0 / acc[0:1])


def _nsa_kernel(q_ref, kc_ref, vct_ref, slc_ref, slc1_ref, win_ref, win1_ref, bm_ref, gt_ref,
                cbt_ref, tslc_ref, twin_ref, o_ref, ps_scr, lhs_scr, m_scr, acc_scr, gt_scr, s_scr):
    tq = Q_TILE
    hpg = NSA_HPG
    dk = NSA_HEAD_DIM
    seq = slc_ref.shape[2]
    g = pl.program_id(1)
    i = pl.program_id(2)
    q0 = pl.multiple_of(i * tq, tq)
    bands = [slice(j * tq, (j + 1) * tq) for j in range(hpg)]

    q_t = q_ref[0].astype(F32).T
    for j in range(hpg):
        lhs_scr[0:dk, bands[j]] = q_t[j * dk:(j + 1) * dk].astype(BF16)
    lhs_scr[dk:_PAIR, :] = jnp.zeros((_PAIR - dk, hpg * tq), BF16)
    qs_t = lhs_scr[0:_PAIR, :]

    st = _dot(kc_ref[0, 0], qs_t)
    st = st + cbt_ref[0, pl.ds(pl.multiple_of(N_CMP_PAD - (tq // CMP_STRIDE) * i, 8), N_CMP_PAD), :]
    mx = jnp.maximum(jnp.max(st, axis=0, keepdims=True), 0.1 * NEG)
    e = jnp.exp(st - mx)
    den = jnp.sum(e, axis=0, keepdims=True)
    pt = e * jnp.where(den > 0.0, 1.0 / den, 0.0)
    o_cmp_t = _dot(vct_ref[0, 0], pt.astype(BF16))
    psum = pt[:, bands[0]]
    for j in range(1, hpg):
        psum = psum + pt[:, bands[j]]

    ratio = SEL_BLOCK // CMP_STRIDE
    imp_tiles = []
    for c in range(tq // V7X_LANES):
        ps_scr[c, 0:8, :] = jnp.zeros((8, V7X_LANES), F32)
        ps_scr[c, 8:8 + N_CMP_PAD, :] = psum[:, c * V7X_LANES:(c + 1) * V7X_LANES]
        ps_scr[c, 8 + N_CMP_PAD:16 + N_CMP_PAD, :] = jnp.zeros((8, V7X_LANES), F32)
        taps = [ps_scr[c, pl.ds(7 + r, N_SEL, stride=ratio), :] for r in range(5)]
        imp_tiles.append(0.5 * taps[0] + taps[1] + taps[2] + taps[3] + 0.5 * taps[4])
    imp = jnp.concatenate(imp_tiles, axis=1)

    jrow = lax.broadcasted_iota(jnp.int32, (N_SEL, tq), 0)
    cur = (q0 + lax.broadcasted_iota(jnp.int32, (N_SEL, tq), 1)) // SEL_BLOCK
    imp = jnp.where(jrow <= cur, imp, -SEL_FORCE)
    for forced_block in (0, cur, cur - 1):
        imp = jnp.where(jrow == forced_block, SEL_FORCE, imp)
    taken = -3.0e38
    for _ in range(SEL_TOPK):
        top = jnp.max(imp, axis=0, keepdims=True)
        first = jnp.min(jnp.where(imp == top, jrow, N_SEL), axis=0, keepdims=True)
        imp = jnp.where(jrow == first, taken, imp)
    not_sel = jnp.where(imp == taken, 0.0, 1.0).astype(BF16)
    for j in range(hpg):
        lhs_scr[_PAIR:2 * _PAIR, bands[j]] = not_sel
    lhs = lhs_scr[...]

    def key_tile(kv_ref, tile, mask_tile):
        return jnp.concatenate([kv_ref[0, 0, pl.ds(pl.multiple_of(tile * tq, tq), tq), :],
                                bm_ref[pl.ds(pl.multiple_of(mask_tile * tq, tq), tq), :]], axis=1)

    all_masked = seq // tq
    unmasked = seq // tq + 1

    prev = jnp.maximum(i - 1, 0)
    keys = jnp.concatenate([key_tile(slc_ref, prev, jnp.where(i == 0, all_masked, i - 1)),
                            key_tile(slc_ref, i, i)], axis=0)
    v1t = jnp.concatenate([slc1_ref[0, 0, prev], slc1_ref[0, 0, i]], axis=1)
    m, p = _col_softmax(_dot(keys, lhs) + tslc_ref[0])
    m_scr[0:1, :] = m
    acc_scr[...] = _dot(v1t, p.astype(BF16))

    n_far = i - 1

    def far_logits(kt):
        valid = kt < n_far
        return _dot(key_tile(slc_ref, jnp.where(valid, kt, 0), jnp.where(valid, kt, all_masked)),
                    lhs_scr[...])

    def absorb(slot, kt):
        s = s_scr[slot]
        m_old = m_scr[0:1, :]
        m_new = jnp.maximum(m_old, jnp.max(s, axis=0, keepdims=True))
        p = jnp.exp(s - m_new)
        v1t = slc1_ref[0, 0, jnp.where(kt < n_far, kt, 0)]
        acc_scr[...] = jnp.exp(m_old - m_new) * acc_scr[...] + _dot(v1t, p.astype(BF16))
        m_scr[0:1, :] = m_new

    s_scr[0] = far_logits(0)

    def far_body(k2, carry):
        s_scr[1] = far_logits(2 * k2 + 1)
        absorb(0, 2 * k2)
        s_scr[0] = far_logits(2 * k2 + 2)
        absorb(1, 2 * k2 + 1)
        return carry

    lax.fori_loop(0, (n_far + 1) // 2, far_body, 0)
    o_slc_t = _normalise_t(acc_scr[...])

    n_prev = WINDOW // tq
    tiles, v1ts = [], []
    for r in range(n_prev, 0, -1):
        tile = jnp.maximum(i - r, 0)
        tiles.append(key_tile(win_ref, tile, jnp.where(i < r, all_masked, unmasked)))
        v1ts.append(win1_ref[0, 0, tile])
    tiles.append(key_tile(win_ref, i, unmasked))
    v1ts.append(win1_ref[0, 0, i])
    _, p = _col_softmax(_dot(jnp.concatenate(tiles, axis=0), lhs) + twin_ref[0])
    o_win_t = _normalise_t(_dot(jnp.concatenate(v1ts, axis=1), p.astype(BF16)))

    gt_scr[...] = gt_ref[0].T
    merged = []
    for j in range(hpg):
        h = g * hpg + j
        gate = [gt_scr[pl.ds(br * NSA_HEADS + h, 1), :] for br in range(3)]
        merged.append(gate[0] * o_cmp_t[dk:, bands[j]] + gate[1] * o_slc_t[:, bands[j]]
                      + gate[2] * o_win_t[:, bands[j]])
    o_ref[0] = jnp.concatenate(merged, axis=0).T.astype(BF16)


def _nsa_attention(q, kc, vct, slc, slc1, win, win1, block_mask, gates, cbt, tslc, twin):
    batch, seq, _ = q.shape
    grp = NSA_KV_GROUPS
    tq = Q_TILE
    rows = NSA_HPG * tq
    per_group = lambda b, g, i: (b, g, 0, 0)
    per_group_t = lambda b, g, i: (b, g, 0, 0, 0)
    return pl.pallas_call(
        _nsa_kernel,
        out_shape=jax.ShapeDtypeStruct((batch, seq, _QW), BF16),
        grid=(batch, grp, seq // tq),
        in_specs=[
            pl.BlockSpec((1, tq, NSA_HPG * NSA_HEAD_DIM), lambda b, g, i: (b, i, g)),
            _resident((1, 1, N_CMP_PAD, _PAIR), per_group),
            _resident((1, 1, _PAIR, N_CMP_PAD), per_group),
            _resident((1, 1, seq, _PAIR), per_group),
            _resident((1, 1, seq // tq, _PAIR, tq), per_group_t),
            _resident((1, 1, seq, _PAIR), per_group),
            _resident((1, 1, seq // tq, _PAIR, tq), per_group_t),
            _resident((seq + 2 * tq, N_SEL), lambda b, g, i: (0, 0)),
            pl.BlockSpec((1, tq, V7X_LANES), lambda b, g, i: (b, i, 0)),
            _resident((1, 2 * N_CMP_PAD, rows), lambda b, g, i: (g, 0, 0)),
            _resident((1, 2 * tq, rows), lambda b, g, i: (g, 0, 0)),
            _resident((1, WINDOW + tq, rows), lambda b, g, i: (g, 0, 0)),
        ],
        out_specs=pl.BlockSpec((1, tq, NSA_HPG * NSA_HEAD_DIM), lambda b, g, i: (b, i, g)),
        scratch_shapes=[pltpu.VMEM((tq // V7X_LANES, N_CMP_PAD + 16, V7X_LANES), F32),
                        pltpu.VMEM((2 * _PAIR, rows), BF16),
                        pltpu.VMEM((8, rows), F32),
                        pltpu.VMEM((_PAIR, rows), F32),
                        pltpu.VMEM((V7X_LANES, tq), F32),
                        pltpu.VMEM((2, tq, rows), F32)],
        compiler_params=pltpu.CompilerParams(
            dimension_semantics=("arbitrary", "arbitrary", "arbitrary"),
            vmem_limit_bytes=_vmem_limit(52 * 1024 * 1024)),
        name="nsa_attention",
    )(q, kc, vct, slc, slc1, win, win1, block_mask, gates, cbt, tslc, twin)


def _nsa_weight(w):
    d = w.shape[0]
    q = w[:, :_QW]
    kv = w[:, _QW:_QW + 6 * _KVW].reshape(d, 6, NSA_KV_GROUPS, NSA_HEAD_DIM)
    cmp_raw = kv[:, 0:2].reshape(d, 2 * _KVW)
    slc = jnp.stack([kv[:, 2], kv[:, 3]], axis=2).reshape(d, NSA_KV_GROUPS * _PAIR)
    win = jnp.stack([kv[:, 4], kv[:, 5]], axis=2).reshape(d, NSA_KV_GROUPS * _PAIR)
    gates = w[:, _QW + 6 * _KVW:_QW + 6 * _KVW + 3 * NSA_HEADS]
    gates = jnp.pad(gates, ((0, 0), (0, V7X_LANES - 3 * NSA_HEADS)))
    xq = w[:, _QW + 6 * _KVW + 3 * NSA_HEADS:]
    return jnp.concatenate([q, cmp_raw, slc, win, gates, xq], axis=1).astype(BF16)


def kernel(x, mem, norm_mix_g, norm_mem_g, norm_ffn_g, norm_final_g, w_in_conv, conv_w, w_in_nsa,
           cmp_pos_k, cmp_w1_k, cmp_w2_k, cmp_pos_v, cmp_w1_v, cmp_w2_v, rel_bias, w_mem_kv, w_out,
           w_ffn_in, w_ffn_out):
    seq = x.shape[1]
    assert seq % ROW_TILE == 0 and seq % Q_TILE == 0 and x.shape[2] == D_MODEL
    assert seq == CMP_STRIDE * _CHUNKS and seq == SEL_BLOCK * N_SEL and Q_TILE == KV_TILE
    memkv = _memkv(mem, norm_mem_g, w_mem_kv.astype(BF16))
    w_out_b = w_out.astype(BF16)
    w_ffn_in_b = w_ffn_in.astype(BF16)
    w_ffn_out_b = w_ffn_out.astype(BF16)

    tok, xq = _conv_front(x, norm_mix_g[0], w_in_conv[0].astype(BF16), conv_w[0])
    h = _mixout(x, tok, xq, memkv[0], w_out_b[0])
    h = _ffn(h, norm_ffn_g[0], w_ffn_in_b[0], w_ffn_out_b[0], norm_final_g, final=False)

    q, cmp_raw, slc, slc1, win, win1, gates, xq = _nsa_proj(h, norm_mix_g[1], _nsa_weight(w_in_nsa[0]))
    pos4 = jnp.stack([jnp.tile(cmp_pos_k[0], (1, NSA_KV_GROUPS)),
                      jnp.tile(cmp_pos_v[0], (1, NSA_KV_GROUPS))])
    w1 = jnp.stack([cmp_w1_k[0], cmp_w1_v[0]]).astype(BF16)
    kc = _compress(cmp_raw, pos4, w1, cmp_w2_k[0].astype(BF16)[None], 0, transposed=False)
    vct = _compress(cmp_raw, pos4, w1, cmp_w2_v[0].T.astype(BF16)[None], 1, transposed=True)
    tok = _nsa_attention(q, kc, vct, slc, slc1, win, win1, _block_mask_rows(seq), gates,
                         *_bias_tables(rel_bias))
    h = _mixout(h, tok, xq, memkv[1], w_out_b[1])
    return _ffn(h, norm_ffn_g[1], w_ffn_in_b[1], w_ffn_out_b[1], norm_final_g, final=True)
```

```python
import functools

import numpy as np
import jax
import jax.numpy as jnp
from jax import lax
from jax.experimental import pallas as pl
from jax.experimental.pallas import tpu as pltpu

F32 = jnp.float32
BF16 = jnp.bfloat16

D_MODEL = 1024
N_MEM = 256
CONV_WIDTH = 3
NSA_HEADS = 16
NSA_HEAD_DIM = 64
NSA_KV_GROUPS = 4
NSA_HPG = NSA_HEADS // NSA_KV_GROUPS
CMP_BLOCK = 32
CMP_STRIDE = 16
CMP_HIDDEN = 2 * NSA_HEAD_DIM
SEL_BLOCK = 64
SEL_TOPK = 16
SEL_FORCE = 1.0e4
WINDOW = 512
XATTN_HEADS = 4
XATTN_HEAD_DIM = 128
XATTN_WIDTH = XATTN_HEADS * XATTN_HEAD_DIM
FFN_HIDDEN = 2816
REL_BUCKETS = 32
REL_MAX_DIST = 128
RMS_EPS = 1e-6
NEG = -1e30
LOG2E = 1.4426950408889634

V7X_LANES = 128
V7X_VMEM_BYTES = 64 * 1024 * 1024

ROW_TILE = 512
FFN_ROW_TILE = 256
FFN_CHUNK = 1408
Q_TILE = 256
KV_TILE = 256
N_CMP_PAD = 512
N_SEL = 128

_NT = (((1,), (1,)), ((), ()))


def _vmem_limit(nbytes):
    return int(min(max(nbytes, 16 * 1024 * 1024), V7X_VMEM_BYTES - 8 * 1024 * 1024))


def _rms(x, g):
    ms = jnp.mean(x * x, axis=-1, keepdims=True)
    return x * lax.rsqrt(ms + RMS_EPS) * g


def _dot(a, b):
    return jnp.dot(a, b, preferred_element_type=F32)


def _dot_nt(a, b):
    return lax.dot_general(a, b, _NT, preferred_element_type=F32)


def _sigmoid(x):
    return 1.0 / (1.0 + jnp.exp(-x))


def _resident(block_shape, index_map):
    return pl.BlockSpec(block_shape, index_map, pipeline_mode=pl.Buffered(1))


def _memkv_kernel(mem_ref, g_ref, w_ref, o_ref):
    xn = _rms(mem_ref[0], g_ref[0]).astype(BF16)
    o_ref[0, 0] = _dot(xn, w_ref[0]).astype(BF16)


def _memkv(mem, gains, w):
    depth, batch = w.shape[0], mem.shape[0]
    return pl.pallas_call(
        _memkv_kernel,
        out_shape=jax.ShapeDtypeStruct((depth, batch, N_MEM, 2 * XATTN_WIDTH), BF16),
        grid=(depth, batch),
        in_specs=[
            pl.BlockSpec((1, N_MEM, D_MODEL), lambda l, b: (b, 0, 0)),
            pl.BlockSpec((1, 1, D_MODEL), lambda l, b: (l, 0, 0)),
            pl.BlockSpec((1, D_MODEL, 2 * XATTN_WIDTH), lambda l, b: (l, 0, 0)),
        ],
        out_specs=pl.BlockSpec((1, 1, N_MEM, 2 * XATTN_WIDTH), lambda l, b: (l, b, 0, 0)),
        compiler_params=pltpu.CompilerParams(
            dimension_semantics=("arbitrary", "arbitrary"),
            vmem_limit_bytes=_vmem_limit(24 * 1024 * 1024)),
        name="memkv",
    )(mem, gains.reshape(depth, 1, D_MODEL), w)


def _conv_kernel(h_ref, hp_ref, g_ref, w_ref, cw_ref, tok_ref, xq_ref):
    d = D_MODEL
    i = pl.program_id(1)
    g = g_ref[...]
    hn = _rms(h_ref[0], g).astype(BF16)
    u_b = _dot(hn, w_ref[:, 0:d])
    v = _dot(hn, w_ref[:, d:2 * d]) * _dot(hn, w_ref[:, 2 * d:3 * d])
    xq_ref[0] = _dot(hn, w_ref[:, 3 * d:]).astype(BF16)
    hpn = _rms(hp_ref[0], g).astype(BF16)
    vp = _dot(hpn, w_ref[:, d:2 * d]) * _dot(hpn, w_ref[:, 2 * d:3 * d])
    vp = jnp.where(i > 0, vp, 0.0)
    row = lax.broadcasted_iota(jnp.int32, v.shape, 0)
    v1 = jnp.where(row == 0, vp[7:8], pltpu.roll(v, 1, axis=0))
    v2 = jnp.where(row == 0, vp[6:7], jnp.where(row == 1, vp[7:8], pltpu.roll(v, 2, axis=0)))
    cw = cw_ref[...]
    conv = v2 * cw[0:1] + v1 * cw[1:2] + v * cw[2:3]
    tok_ref[0] = (u_b * conv).astype(BF16)


def _conv_front(h, g, w, cw):
    batch, seq, d = h.shape
    tm = ROW_TILE
    width = w.shape[1]
    halo = 8
    return pl.pallas_call(
        _conv_kernel,
        out_shape=(jax.ShapeDtypeStruct((batch, seq, d), BF16),
                   jax.ShapeDtypeStruct((batch, seq, XATTN_WIDTH), BF16)),
        grid=(batch, seq // tm),
        in_specs=[
            pl.BlockSpec((1, tm, d), lambda b, i: (b, i, 0)),
            pl.BlockSpec((1, halo, d), lambda b, i: (b, jnp.maximum(i * (tm // halo) - 1, 0), 0)),
            pl.BlockSpec((1, d), lambda b, i: (0, 0)),
            pl.BlockSpec((d, width), lambda b, i: (0, 0)),
            pl.BlockSpec((CONV_WIDTH, d), lambda b, i: (0, 0)),
        ],
        out_specs=(pl.BlockSpec((1, tm, d), lambda b, i: (b, i, 0)),
                   pl.BlockSpec((1, tm, XATTN_WIDTH), lambda b, i: (b, i, 0))),
        compiler_params=pltpu.CompilerParams(
            dimension_semantics=("arbitrary", "arbitrary"),
            vmem_limit_bytes=_vmem_limit(48 * 1024 * 1024)),
        name="conv_front",
    )(h, h, g.reshape(1, d), w, cw)


def _mixout_kernel(h_ref, tok_ref, xq_ref, kv_ref, w_ref, o_ref):
    d = D_MODEL
    acc = h_ref[0] + _dot(tok_ref[0], w_ref[0:d])
    xq = xq_ref[0]
    kv = kv_ref[0]
    scale = XATTN_HEAD_DIM ** -0.5 * LOG2E
    outs = []
    for hh in range(XATTN_HEADS):
        lo = hh * XATTN_HEAD_DIM
        q = xq[:, lo:lo + XATTN_HEAD_DIM]
        k = kv[:, lo:lo + XATTN_HEAD_DIM]
        v = kv[:, XATTN_WIDTH + lo:XATTN_WIDTH + lo + XATTN_HEAD_DIM]
        s = _dot_nt(q, k) * scale
        e = jnp.exp2(s - jnp.max(s, axis=-1, keepdims=True))
        p = e / jnp.sum(e, axis=-1, keepdims=True)
        outs.append(_dot(p.astype(BF16), v))
    mo = jnp.concatenate(outs, axis=-1).astype(BF16)
    o_ref[0] = acc + _dot(mo, w_ref[d:])


def _mixout(h, tok, xq, kv, w):
    batch, seq, d = h.shape
    tm = ROW_TILE
    return pl.pallas_call(
        _mixout_kernel,
        out_shape=jax.ShapeDtypeStruct((batch, seq, d), F32),
        grid=(batch, seq // tm),
        in_specs=[
            pl.BlockSpec((1, tm, d), lambda b, i: (b, i, 0)),
            pl.BlockSpec((1, tm, d), lambda b, i: (b, i, 0)),
            pl.BlockSpec((1, tm, XATTN_WIDTH), lambda b, i: (b, i, 0)),
            pl.BlockSpec((1, N_MEM, 2 * XATTN_WIDTH), lambda b, i: (b, 0, 0)),
            pl.BlockSpec((d + XATTN_WIDTH, d), lambda b, i: (0, 0)),
        ],
        out_specs=pl.BlockSpec((1, tm, d), lambda b, i: (b, i, 0)),
        compiler_params=pltpu.CompilerParams(
            dimension_semantics=("arbitrary", "arbitrary"),
            vmem_limit_bytes=_vmem_limit(40 * 1024 * 1024)),
        name="mixout",
    )(h, tok, xq, kv, w)


def _ffn_kernel(h_ref, g_ref, w1_ref, w2_ref, gf_ref, o_ref, *, final):
    h = h_ref[0]
    hn = _rms(h, g_ref[...]).astype(BF16)
    acc = h
    for c in range(FFN_HIDDEN // FFN_CHUNK):
        lo = c * FFN_CHUNK
        a = _dot(hn, w1_ref[:, lo:lo + FFN_CHUNK])
        u = _dot(hn, w1_ref[:, FFN_HIDDEN + lo:FFN_HIDDEN + lo + FFN_CHUNK])
        act = (a * _sigmoid(a) * u).astype(BF16)
        acc = acc + _dot(act, w2_ref[lo:lo + FFN_CHUNK])
    if final:
        acc = _rms(acc, gf_ref[...])
    o_ref[0] = acc


def _ffn(h, g, w1, w2, gf, final):
    batch, seq, d = h.shape
    tm = FFN_ROW_TILE
    return pl.pallas_call(
        functools.partial(_ffn_kernel, final=final),
        out_shape=jax.ShapeDtypeStruct((batch, seq, d), F32),
        grid=(batch, seq // tm),
        in_specs=[
            pl.BlockSpec((1, tm, d), lambda b, i: (b, i, 0)),
            pl.BlockSpec((1, d), lambda b, i: (0, 0)),
            pl.BlockSpec((d, 2 * FFN_HIDDEN), lambda b, i: (0, 0)),
            pl.BlockSpec((FFN_HIDDEN, d), lambda b, i: (0, 0)),
            pl.BlockSpec((1, d), lambda b, i: (0, 0)),
        ],
        out_specs=pl.BlockSpec((1, tm, d), lambda b, i: (b, i, 0)),
        compiler_params=pltpu.CompilerParams(
            dimension_semantics=("arbitrary", "arbitrary"),
            vmem_limit_bytes=_vmem_limit(52 * 1024 * 1024)),
        name="ffn",
    )(h, g.reshape(1, d), w1, w2, gf.reshape(1, d))


_QW = NSA_HEADS * NSA_HEAD_DIM
_KVW = NSA_KV_GROUPS * NSA_HEAD_DIM
_PAIR = 2 * NSA_HEAD_DIM
_C_Q = 0
_C_CMP = _C_Q + _QW
_C_SLC = _C_CMP + 2 * _KVW
_C_WIN = _C_SLC + NSA_KV_GROUPS * _PAIR
_C_GATE = _C_WIN + NSA_KV_GROUPS * _PAIR
_C_XQ = _C_GATE + V7X_LANES
_C_END = _C_XQ + XATTN_WIDTH


_SUM_ROWS = 16
_VT_ROWS = _SUM_ROWS + NSA_HEAD_DIM


def _ones_and_vt(kv):
    v_t = kv.T[NSA_HEAD_DIM:]
    return jnp.concatenate([jnp.ones((_SUM_ROWS, kv.shape[0]), F32), v_t], axis=0).astype(BF16)


def _nsa_proj_kernel(h_ref, g_ref, w_ref, q_ref, cr_ref, slc_ref, slc1_ref, win_ref, win1_ref,
                     gt_ref, xq_ref):
    hn = _rms(h_ref[0], g_ref[...]).astype(BF16)
    q_ref[0] = (_dot(hn, w_ref[:, _C_Q:_C_CMP]) * (NSA_HEAD_DIM ** -0.5 * LOG2E)).astype(BF16)
    cr = _dot(hn, w_ref[:, _C_CMP:_C_SLC])
    for p in range(2 * _KVW // V7X_LANES):
        cr_ref[0, p] = cr[:, p * V7X_LANES:(p + 1) * V7X_LANES]
    s = _dot(hn, w_ref[:, _C_SLC:_C_WIN])
    w = _dot(hn, w_ref[:, _C_WIN:_C_GATE])
    for g in range(NSA_KV_GROUPS):
        sg = s[:, g * _PAIR:(g + 1) * _PAIR]
        wg = w[:, g * _PAIR:(g + 1) * _PAIR]
        slc_ref[0, g] = sg.astype(BF16)
        win_ref[0, g] = wg.astype(BF16)
        for c in range(ROW_TILE // KV_TILE):
            rows = slice(c * KV_TILE, (c + 1) * KV_TILE)
            slc1_ref[0, g, c] = _ones_and_vt(sg[rows])
            win1_ref[0, g, c] = _ones_and_vt(wg[rows])
    gt_ref[0] = _sigmoid(_dot(hn, w_ref[:, _C_GATE:_C_XQ]))
    xq_ref[0] = _dot(hn, w_ref[:, _C_XQ:_C_END]).astype(BF16)


def _nsa_proj(h, g, w):
    batch, seq, d = h.shape
    tm = ROW_TILE
    grp = NSA_KV_GROUPS
    row = lambda b, i: (b, i, 0)
    pair_shape = jax.ShapeDtypeStruct((batch, grp, seq, _PAIR), BF16)
    pair_spec = pl.BlockSpec((1, grp, tm, _PAIR), lambda b, i: (b, 0, i, 0))
    vt_shape = jax.ShapeDtypeStruct((batch, grp, seq // KV_TILE, _VT_ROWS, KV_TILE), BF16)
    vt_spec = pl.BlockSpec((1, grp, tm // KV_TILE, _VT_ROWS, KV_TILE), lambda b, i: (b, 0, i, 0, 0))
    return pl.pallas_call(
        _nsa_proj_kernel,
        out_shape=(jax.ShapeDtypeStruct((batch, seq, _QW), BF16),
                   jax.ShapeDtypeStruct((batch, 2 * _KVW // V7X_LANES, seq, V7X_LANES), F32),
                   pair_shape, vt_shape, pair_shape, vt_shape,
                   jax.ShapeDtypeStruct((batch, seq, V7X_LANES), F32),
                   jax.ShapeDtypeStruct((batch, seq, XATTN_WIDTH), BF16)),
        grid=(batch, seq // tm),
        in_specs=[
            pl.BlockSpec((1, tm, d), row),
            pl.BlockSpec((1, d), lambda b, i: (0, 0)),
            pl.BlockSpec((d, _C_END), lambda b, i: (0, 0)),
        ],
        out_specs=(pl.BlockSpec((1, tm, _QW), row),
                   pl.BlockSpec((1, 2 * _KVW // V7X_LANES, tm, V7X_LANES), lambda b, i: (b, 0, i, 0)),
                   pair_spec, vt_spec, pair_spec, vt_spec,
                   pl.BlockSpec((1, tm, V7X_LANES), row),
                   pl.BlockSpec((1, tm, XATTN_WIDTH), row)),
        compiler_params=pltpu.CompilerParams(
            dimension_semantics=("arbitrary", "arbitrary"),
            vmem_limit_bytes=_vmem_limit(48 * 1024 * 1024)),
        name="nsa_proj",
    )(h, g.reshape(1, d), w)


_CHUNKS = 512


def _compress_kernel(raw_ref, pos_ref, w1_ref, w2_ref, o_ref, h2_scr, *, transposed):
    dk = NSA_HEAD_DIM
    half = CMP_BLOCK // 2
    h1 = [jnp.zeros((_CHUNKS, CMP_HIDDEN), F32) for _ in range(NSA_KV_GROUPS)]
    h2 = [jnp.zeros((_CHUNKS, CMP_HIDDEN), F32) for _ in range(NSA_KV_GROUPS)]
    gpl = V7X_LANES // dk
    for l in range(half):
        for p in range(NSA_KV_GROUPS // gpl):
            lanes = slice(p * V7X_LANES, (p + 1) * V7X_LANES)
            rows = raw_ref[0, p, pl.ds(l, _CHUNKS, stride=CMP_STRIDE), :]
            a1 = (rows + pos_ref[0, l:l + 1, lanes]).astype(BF16)
            a2 = (rows + pos_ref[0, half + l:half + l + 1, lanes]).astype(BF16)
            for gg in range(gpl):
                g = p * gpl + gg
                h1[g] = h1[g] + _dot(a1[:, gg * dk:(gg + 1) * dk], w1_ref[0, l * dk:(l + 1) * dk, :])
                h2[g] = h2[g] + _dot(a2[:, gg * dk:(gg + 1) * dk],
                                     w1_ref[0, (half + l) * dk:(half + l + 1) * dk, :])
    h2_scr[_CHUNKS:_CHUNKS + 8, :] = jnp.zeros((8, CMP_HIDDEN), F32)
    for g in range(NSA_KV_GROUPS):
        h2_scr[0:_CHUNKS, :] = h2[g]
        hm = h1[g] + h2_scr[pl.ds(1, _CHUNKS), :]
        hm = (hm * _sigmoid(hm)).astype(BF16)
        if transposed:
            o_ref[0, g] = _dot_nt(w2_ref[0], hm).astype(BF16)
        else:
            out = _dot(hm, w2_ref[0])
            o_ref[0, g] = jnp.concatenate([out, jnp.zeros_like(out)], axis=1).astype(BF16)


def _compress(raw, pos4, w1, w2, which, transposed):
    batch, _, seq, _ = raw.shape
    grp = NSA_KV_GROUPS
    tiles = _KVW // V7X_LANES
    if transposed:
        out_shape = (batch, grp, NSA_HEAD_DIM, N_CMP_PAD)
        w2_block = (1, NSA_HEAD_DIM, CMP_HIDDEN)
    else:
        out_shape = (batch, grp, N_CMP_PAD, _PAIR)
        w2_block = (1, CMP_HIDDEN, NSA_HEAD_DIM)
    return pl.pallas_call(
        functools.partial(_compress_kernel, transposed=transposed),
        out_shape=jax.ShapeDtypeStruct(out_shape, BF16),
        grid=(batch,),
        in_specs=[
            pl.BlockSpec((1, tiles, seq, V7X_LANES), lambda b: (b, which, 0, 0)),
            pl.BlockSpec((1, CMP_BLOCK, _KVW), lambda b: (which, 0, 0)),
            pl.BlockSpec((1, CMP_BLOCK * NSA_HEAD_DIM, CMP_HIDDEN), lambda b: (which, 0, 0)),
            pl.BlockSpec(w2_block, lambda b: (0, 0, 0)),
        ],
        out_specs=pl.BlockSpec((1,) + out_shape[1:], lambda b: (b, 0, 0, 0)),
        scratch_shapes=[pltpu.VMEM((_CHUNKS + 8, CMP_HIDDEN), F32)],
        compiler_params=pltpu.CompilerParams(
            dimension_semantics=("arbitrary",),
            vmem_limit_bytes=_vmem_limit(40 * 1024 * 1024)),
        name="compress_v" if transposed else "compress_k",
    )(raw, pos4, w1, w2)


def _t5_bucket(dist):
    n = jnp.maximum(dist, 0)
    max_exact = REL_BUCKETS // 2
    nf = jnp.maximum(n, 1).astype(F32)
    large = max_exact + (jnp.log(nf / max_exact) / np.log(REL_MAX_DIST / max_exact)
                         * (REL_BUCKETS - max_exact)).astype(jnp.int32)
    large = jnp.minimum(large, REL_BUCKETS - 1)
    return jnp.where(n < max_exact, n, large)


def _table_kernel(thr_ref, rb_ref, cbt_ref, tslc_ref, twin_ref):
    tq = Q_TILE
    g = pl.program_id(0)

    def entries(dist, h, limit):
        val = jnp.full(dist.shape, rb_ref[0, h], F32)
        for k in range(1, REL_BUCKETS):
            val = jnp.where(dist >= thr_ref[k], rb_ref[k, h], val)
        val = (val - rb_ref[REL_BUCKETS - 1, h]) * LOG2E
        return jnp.where(jnp.where(dist >= 0, dist, limit) < limit, val, NEG)

    def far_entries(dist, limit):
        return jnp.where(jnp.where(dist >= 0, dist, limit) < limit, 0.0, NEG)

    def fill(ref, band, dist_of_rows, n_rows, limit, h):
        step = 64
        for r0 in range(0, n_rows, step):
            shape = (step, tq)
            dist = dist_of_rows(r0 + lax.broadcasted_iota(jnp.int32, shape, 0),
                                lax.broadcasted_iota(jnp.int32, shape, 1))
            lo, hi = dist_of_rows(r0 + step - 1, 0), dist_of_rows(r0, tq - 1)
            near = hi >= 0 and lo < REL_MAX_DIST
            ref[0, r0:r0 + step, band] = entries(dist, h, limit) if near else far_entries(dist, limit)

    no_limit = 1 << 30
    for j in range(NSA_HPG):
        h = g * NSA_HPG + j
        band = slice(j * tq, (j + 1) * tq)
        fill(cbt_ref, band, lambda u, q: q - (CMP_BLOCK - 1) - CMP_STRIDE * (u - N_CMP_PAD),
             2 * N_CMP_PAD, no_limit, h)
        fill(tslc_ref, band, lambda u, q: q - u + tq, 2 * tq, no_limit, h)
        fill(twin_ref, band, lambda u, q: q - u + WINDOW, WINDOW + tq, WINDOW, h)


def _bias_tables(rel_bias):
    tq = Q_TILE
    rows = NSA_HPG * tq
    buckets = _t5_bucket(jnp.arange(REL_MAX_DIST + 1))
    thr = jnp.sum(buckets[None, :] < jnp.arange(REL_BUCKETS)[:, None], axis=1).astype(jnp.int32)
    smem = pl.BlockSpec(memory_space=pltpu.SMEM)
    return pl.pallas_call(
        _table_kernel,
        out_shape=(jax.ShapeDtypeStruct((NSA_KV_GROUPS, 2 * N_CMP_PAD, rows), F32),
                   jax.ShapeDtypeStruct((NSA_KV_GROUPS, 2 * tq, rows), F32),
                   jax.ShapeDtypeStruct((NSA_KV_GROUPS, WINDOW + tq, rows), F32)),
        grid=(NSA_KV_GROUPS,),
        in_specs=[smem, smem],
        out_specs=(pl.BlockSpec((1, 2 * N_CMP_PAD, rows), lambda g: (g, 0, 0)),
                   pl.BlockSpec((1, 2 * tq, rows), lambda g: (g, 0, 0)),
                   pl.BlockSpec((1, WINDOW + tq, rows), lambda g: (g, 0, 0))),
        compiler_params=pltpu.CompilerParams(
            dimension_semantics=("arbitrary",),
            vmem_limit_bytes=_vmem_limit(40 * 1024 * 1024)),
        name="bias_tables",
    )(thr, rel_bias.astype(F32))


def _block_mask_rows(seq):
    pos = jnp.arange(seq, dtype=jnp.int32)[:, None]
    col = jnp.arange(N_SEL, dtype=jnp.int32)[None, :]
    onehot = jnp.where(col == pos // SEL_BLOCK, NEG, 0.0)
    return jnp.concatenate([onehot, jnp.full((Q_TILE, N_SEL), NEG), jnp.zeros((Q_TILE, N_SEL))],
                           axis=0).astype(BF16)


def _col_softmax(s):
    m = jnp.max(s, axis=0, keepdims=True)
    return m, jnp.exp2(s - m)


def _normalise_t(acc):
    return acc[_SUM_ROWS:] * (1.0 / acc[0:1])


def _nsa_kernel(q_ref, kc_ref, vct_ref, slc_ref, slc1_ref, win_ref, win1_ref, bm_ref, gt_ref,
                cbt_ref, tslc_ref, twin_ref, o_ref, ps_scr, lhs_scr, m_scr, acc_scr, gt_scr, s_scr):
    tq = Q_TILE
    hpg = NSA_HPG
    dk = NSA_HEAD_DIM
    seq = slc_ref.shape[2]
    g = pl.program_id(1)
    i = pl.program_id(2)
    q0 = pl.multiple_of(i * tq, tq)
    bands = [slice(j * tq, (j + 1) * tq) for j in range(hpg)]

    q_t = q_ref[0].astype(F32).T
    for j in range(hpg):
        lhs_scr[0:dk, bands[j]] = q_t[j * dk:(j + 1) * dk].astype(BF16)
    lhs_scr[dk:_PAIR, :] = jnp.zeros((_PAIR - dk, hpg * tq), BF16)
    qs_t = lhs_scr[0:_PAIR, :]

    st = _dot(kc_ref[0, 0], qs_t)
    st = st + cbt_ref[0, pl.ds(pl.multiple_of(N_CMP_PAD - (tq // CMP_STRIDE) * i, 8), N_CMP_PAD), :]
    mx = jnp.maximum(jnp.max(st, axis=0, keepdims=True), 0.1 * NEG)
    e = jnp.exp2(st - mx)
    den = jnp.sum(e, axis=0, keepdims=True)
    inv = jnp.where(den > 0.0, 1.0 / den, 0.0)
    o_cmp_t = _dot(vct_ref[0, 0], e.astype(BF16)) * inv
    psum = e[:, bands[0]] * inv[:, bands[0]]
    for j in range(1, hpg):
        psum = psum + e[:, bands[j]] * inv[:, bands[j]]

    ratio = SEL_BLOCK // CMP_STRIDE
    imp_tiles = []
    for c in range(tq // V7X_LANES):
        ps_scr[c, 0:8, :] = jnp.zeros((8, V7X_LANES), F32)
        ps_scr[c, 8:8 + N_CMP_PAD, :] = psum[:, c * V7X_LANES:(c + 1) * V7X_LANES]
        ps_scr[c, 8 + N_CMP_PAD:16 + N_CMP_PAD, :] = jnp.zeros((8, V7X_LANES), F32)
        taps = [ps_scr[c, pl.ds(7 + r, N_SEL, stride=ratio), :] for r in range(5)]
        imp_tiles.append(0.5 * taps[0] + taps[1] + taps[2] + taps[3] + 0.5 * taps[4])
    imp = jnp.concatenate(imp_tiles, axis=1)

    jrow = lax.broadcasted_iota(jnp.int32, (N_SEL, tq), 0)
    cur = (q0 + lax.broadcasted_iota(jnp.int32, (N_SEL, tq), 1)) // SEL_BLOCK
    imp = jnp.where(jrow <= cur, imp, -SEL_FORCE)
    for forced_block in (0, cur, cur - 1):
        imp = jnp.where(jrow == forced_block, SEL_FORCE, imp)
    taken = -3.0e38
    for _ in range(SEL_TOPK):
        top = jnp.max(imp, axis=0, keepdims=True)
        first = jnp.min(jnp.where(imp == top, jrow, N_SEL), axis=0, keepdims=True)
        imp = jnp.where(jrow == first, taken, imp)
    not_sel = jnp.where(imp == taken, 0.0, 1.0).astype(BF16)
    for j in range(hpg):
        lhs_scr[_PAIR:2 * _PAIR, bands[j]] = not_sel
    lhs = lhs_scr[...]

    def key_tile(kv_ref, tile, mask_tile):
        return jnp.concatenate([kv_ref[0, 0, pl.ds(pl.multiple_of(tile * tq, tq), tq), :],
                                bm_ref[pl.ds(pl.multiple_of(mask_tile * tq, tq), tq), :]], axis=1)

    all_masked = seq // tq
    unmasked = seq // tq + 1

    prev = jnp.maximum(i - 1, 0)
    keys = jnp.concatenate([key_tile(slc_ref, prev, jnp.where(i == 0, all_masked, i - 1)),
                            key_tile(slc_ref, i, i)], axis=0)
    v1t = jnp.concatenate([slc1_ref[0, 0, prev], slc1_ref[0, 0, i]], axis=1)
    m, p = _col_softmax(_dot(keys, lhs) + tslc_ref[0])
    m_scr[0:1, :] = m
    acc_scr[...] = _dot(v1t, p.astype(BF16))

    n_far = jnp.maximum(i - 1, 0)

    def far_logits(kt):
        valid = kt < n_far
        return _dot(key_tile(slc_ref, jnp.where(valid, kt, 0), jnp.where(valid, kt, all_masked)),
                    lhs_scr[...])

    def absorb(slot, kt):
        s = s_scr[slot]
        m_old = m_scr[0:1, :]
        m_new = jnp.maximum(m_old, jnp.max(s, axis=0, keepdims=True))
        p = jnp.exp2(s - m_new)
        v1t = slc1_ref[0, 0, jnp.where(kt < n_far, kt, 0)]
        acc_scr[...] = jnp.exp2(m_old - m_new) * acc_scr[...] + _dot(v1t, p.astype(BF16))
        m_scr[0:1, :] = m_new

    def far_trip(base, tiles):
        for t in range(tiles):
            s_scr[(t + 1) % 2] = far_logits(base + t + 1)
            absorb(t % 2, base + t)

    s_scr[0] = far_logits(0)
    long_trip, short_trip = 4, 2
    n_long = n_far // long_trip

    def long_body(k, carry):
        far_trip(long_trip * k, long_trip)
        return carry

    def short_body(k, carry):
        far_trip(long_trip * n_long + short_trip * k, short_trip)
        return carry

    lax.fori_loop(0, n_long, long_body, 0)
    lax.fori_loop(0, (n_far - long_trip * n_long + short_trip - 1) // short_trip, short_body, 0)
    o_slc_t = _normalise_t(acc_scr[...])

    n_prev = WINDOW // tq
    tiles, v1ts = [], []
    for r in range(n_prev, 0, -1):
        tile = jnp.maximum(i - r, 0)
        tiles.append(key_tile(win_ref, tile, jnp.where(i < r, all_masked, unmasked)))
        v1ts.append(win1_ref[0, 0, tile])
    tiles.append(key_tile(win_ref, i, unmasked))
    v1ts.append(win1_ref[0, 0, i])
    _, p = _col_softmax(_dot(jnp.concatenate(tiles, axis=0), lhs) + twin_ref[0])
    o_win_t = _normalise_t(_dot(jnp.concatenate(v1ts, axis=1), p.astype(BF16)))

    gt_scr[...] = gt_ref[0].T
    merged = []
    for j in range(hpg):
        h = g * hpg + j
        gate = [gt_scr[pl.ds(br * NSA_HEADS + h, 1), :] for br in range(3)]
        merged.append(gate[0] * o_cmp_t[:, bands[j]] + gate[1] * o_slc_t[:, bands[j]]
                      + gate[2] * o_win_t[:, bands[j]])
    o_ref[0] = jnp.concatenate(merged, axis=0).T.astype(BF16)


def _nsa_attention(q, kc, vct, slc, slc1, win, win1, block_mask, gates, cbt, tslc, twin):
    batch, seq, _ = q.shape
    grp = NSA_KV_GROUPS
    tq = Q_TILE
    rows = NSA_HPG * tq
    per_group = lambda b, g, i: (b, g, 0, 0)
    per_group_t = lambda b, g, i: (b, g, 0, 0, 0)
    return pl.pallas_call(
        _nsa_kernel,
        out_shape=jax.ShapeDtypeStruct((batch, seq, _QW), BF16),
        grid=(batch, grp, seq // tq),
        in_specs=[
            pl.BlockSpec((1, tq, NSA_HPG * NSA_HEAD_DIM), lambda b, g, i: (b, i, g)),
            _resident((1, 1, N_CMP_PAD, _PAIR), per_group),
            _resident((1, 1, NSA_HEAD_DIM, N_CMP_PAD), per_group),
            _resident((1, 1, seq, _PAIR), per_group),
            _resident((1, 1, seq // tq, _VT_ROWS, tq), per_group_t),
            _resident((1, 1, seq, _PAIR), per_group),
            _resident((1, 1, seq // tq, _VT_ROWS, tq), per_group_t),
            _resident((seq + 2 * tq, N_SEL), lambda b, g, i: (0, 0)),
            pl.BlockSpec((1, tq, V7X_LANES), lambda b, g, i: (b, i, 0)),
            _resident((1, 2 * N_CMP_PAD, rows), lambda b, g, i: (g, 0, 0)),
            _resident((1, 2 * tq, rows), lambda b, g, i: (g, 0, 0)),
            _resident((1, WINDOW + tq, rows), lambda b, g, i: (g, 0, 0)),
        ],
        out_specs=pl.BlockSpec((1, tq, NSA_HPG * NSA_HEAD_DIM), lambda b, g, i: (b, i, g)),
        scratch_shapes=[pltpu.VMEM((tq // V7X_LANES, N_CMP_PAD + 16, V7X_LANES), F32),
                        pltpu.VMEM((2 * _PAIR, rows), BF16),
                        pltpu.VMEM((8, rows), F32),
                        pltpu.VMEM((_VT_ROWS, rows), F32),
                        pltpu.VMEM((V7X_LANES, tq), F32),
                        pltpu.VMEM((2, tq, rows), F32)],
        compiler_params=pltpu.CompilerParams(
            dimension_semantics=("arbitrary", "arbitrary", "arbitrary"),
            vmem_limit_bytes=_vmem_limit(52 * 1024 * 1024)),
        name="nsa_attention",
    )(q, kc, vct, slc, slc1, win, win1, block_mask, gates, cbt, tslc, twin)


def _nsa_weight(w):
    d = w.shape[0]
    q = w[:, :_QW]
    kv = w[:, _QW:_QW + 6 * _KVW].reshape(d, 6, NSA_KV_GROUPS, NSA_HEAD_DIM)
    cmp_raw = kv[:, 0:2].reshape(d, 2 * _KVW)
    slc = jnp.stack([kv[:, 2], kv[:, 3]], axis=2).reshape(d, NSA_KV_GROUPS * _PAIR)
    win = jnp.stack([kv[:, 4], kv[:, 5]], axis=2).reshape(d, NSA_KV_GROUPS * _PAIR)
    gates = w[:, _QW + 6 * _KVW:_QW + 6 * _KVW + 3 * NSA_HEADS]
    gates = jnp.pad(gates, ((0, 0), (0, V7X_LANES - 3 * NSA_HEADS)))
    xq = w[:, _QW + 6 * _KVW + 3 * NSA_HEADS:]
    return jnp.concatenate([q, cmp_raw, slc, win, gates, xq], axis=1).astype(BF16)


def kernel(x, mem, norm_mix_g, norm_mem_g, norm_ffn_g, norm_final_g, w_in_conv, conv_w, w_in_nsa,
           cmp_pos_k, cmp_w1_k, cmp_w2_k, cmp_pos_v, cmp_w1_v, cmp_w2_v, rel_bias, w_mem_kv, w_out,
           w_ffn_in, w_ffn_out):
    seq = x.shape[1]
    assert seq % ROW_TILE == 0 and seq % Q_TILE == 0 and x.shape[2] == D_MODEL
    assert seq == CMP_STRIDE * _CHUNKS and seq == SEL_BLOCK * N_SEL and Q_TILE == KV_TILE
    memkv = _memkv(mem, norm_mem_g, w_mem_kv.astype(BF16))
    w_out_b = w_out.astype(BF16)
    w_ffn_in_b = w_ffn_in.astype(BF16)
    w_ffn_out_b = w_ffn_out.astype(BF16)

    tok, xq = _conv_front(x, norm_mix_g[0], w_in_conv[0].astype(BF16), conv_w[0])
    h = _mixout(x, tok, xq, memkv[0], w_out_b[0])
    h = _ffn(h, norm_ffn_g[0], w_ffn_in_b[0], w_ffn_out_b[0], norm_final_g, final=False)

    q, cmp_raw, slc, slc1, win, win1, gates, xq = _nsa_proj(h, norm_mix_g[1], _nsa_weight(w_in_nsa[0]))
    pos4 = jnp.stack([jnp.tile(cmp_pos_k[0], (1, NSA_KV_GROUPS)),
                      jnp.tile(cmp_pos_v[0], (1, NSA_KV_GROUPS))])
    w1 = jnp.stack([cmp_w1_k[0], cmp_w1_v[0]]).astype(BF16)
    kc = _compress(cmp_raw, pos4, w1, cmp_w2_k[0].astype(BF16)[None], 0, transposed=False)
    vct = _compress(cmp_raw, pos4, w1, cmp_w2_v[0].T.astype(BF16)[None], 1, transposed=True)
    tok = _nsa_attention(q, kc, vct, slc, slc1, win, win1, _block_mask_rows(seq), gates,
                         *_bias_tables(rel_bias))
    h = _mixout(h, tok, xq, memkv[1], w_out_b[1])
    return _ffn(h, norm_ffn_g[1], w_ffn_in_b[1], w_ffn_out_b[1], norm_final_g, final=True)
```

```python
import functools

import numpy as np
import jax
import jax.numpy as jnp
from jax import lax
from jax.experimental import pallas as pl
from jax.experimental.pallas import tpu as pltpu

F32 = jnp.float32
BF16 = jnp.bfloat16

D_MODEL = 1024
N_MEM = 256
CONV_WIDTH = 3
NSA_HEADS = 16
NSA_HEAD_DIM = 64
NSA_KV_GROUPS = 4
NSA_HPG = NSA_HEADS // NSA_KV_GROUPS
CMP_BLOCK = 32
CMP_STRIDE = 16
CMP_HIDDEN = 2 * NSA_HEAD_DIM
SEL_BLOCK = 64
SEL_TOPK = 16
SEL_FORCE = 1.0e4
WINDOW = 512
XATTN_HEADS = 4
XATTN_HEAD_DIM = 128
XATTN_WIDTH = XATTN_HEADS * XATTN_HEAD_DIM
FFN_HIDDEN = 2816
REL_BUCKETS = 32
REL_MAX_DIST = 128
RMS_EPS = 1e-6
NEG = -1e30
LOG2E = 1.4426950408889634

V7X_LANES = 128
V7X_VMEM_BYTES = 64 * 1024 * 1024

ROW_TILE = 512
FFN_ROW_TILE = 256
FFN_CHUNK = 1408
Q_TILE = 256
KV_TILE = 256
FAR_GROUP = 2
CMP_VARIANTS = 4
N_CMP_PAD = 512
N_SEL = 128

_NT = (((1,), (1,)), ((), ()))


def _vmem_limit(nbytes):
    return int(min(max(nbytes, 16 * 1024 * 1024), V7X_VMEM_BYTES - 8 * 1024 * 1024))


def _rms(x, g):
    ms = jnp.mean(x * x, axis=-1, keepdims=True)
    return x * lax.rsqrt(ms + RMS_EPS) * g


def _dot(a, b):
    return jnp.dot(a, b, preferred_element_type=F32)


def _dot_nt(a, b):
    return lax.dot_general(a, b, _NT, preferred_element_type=F32)


def _sigmoid(x):
    return 1.0 / (1.0 + jnp.exp(-x))


def _resident(block_shape, index_map):
    return pl.BlockSpec(block_shape, index_map, pipeline_mode=pl.Buffered(1))


def _memkv_kernel(mem_ref, g_ref, w_ref, o_ref):
    xn = _rms(mem_ref[0], g_ref[0]).astype(BF16)
    o_ref[0, 0] = _dot(xn, w_ref[0]).astype(BF16)


def _memkv(mem, gains, w):
    depth, batch = w.shape[0], mem.shape[0]
    return pl.pallas_call(
        _memkv_kernel,
        out_shape=jax.ShapeDtypeStruct((depth, batch, N_MEM, 2 * XATTN_WIDTH), BF16),
        grid=(depth, batch),
        in_specs=[
            pl.BlockSpec((1, N_MEM, D_MODEL), lambda l, b: (b, 0, 0)),
            pl.BlockSpec((1, 1, D_MODEL), lambda l, b: (l, 0, 0)),
            pl.BlockSpec((1, D_MODEL, 2 * XATTN_WIDTH), lambda l, b: (l, 0, 0)),
        ],
        out_specs=pl.BlockSpec((1, 1, N_MEM, 2 * XATTN_WIDTH), lambda l, b: (l, b, 0, 0)),
        compiler_params=pltpu.CompilerParams(
            dimension_semantics=("arbitrary", "arbitrary"),
            vmem_limit_bytes=_vmem_limit(24 * 1024 * 1024)),
        name="memkv",
    )(mem, gains.reshape(depth, 1, D_MODEL), w)


def _conv_kernel(h_ref, hp_ref, g_ref, w_ref, cw_ref, tok_ref, xq_ref):
    d = D_MODEL
    i = pl.program_id(1)
    g = g_ref[...]
    hn = _rms(h_ref[0], g).astype(BF16)
    u_b = _dot(hn, w_ref[:, 0:d])
    v = _dot(hn, w_ref[:, d:2 * d]) * _dot(hn, w_ref[:, 2 * d:3 * d])
    xq_ref[0] = _dot(hn, w_ref[:, 3 * d:]).astype(BF16)
    hpn = _rms(hp_ref[0], g).astype(BF16)
    vp = _dot(hpn, w_ref[:, d:2 * d]) * _dot(hpn, w_ref[:, 2 * d:3 * d])
    vp = jnp.where(i > 0, vp, 0.0)
    row = lax.broadcasted_iota(jnp.int32, v.shape, 0)
    v1 = jnp.where(row == 0, vp[7:8], pltpu.roll(v, 1, axis=0))
    v2 = jnp.where(row == 0, vp[6:7], jnp.where(row == 1, vp[7:8], pltpu.roll(v, 2, axis=0)))
    cw = cw_ref[...]
    conv = v2 * cw[0:1] + v1 * cw[1:2] + v * cw[2:3]
    tok_ref[0] = (u_b * conv).astype(BF16)


def _conv_front(h, g, w, cw):
    batch, seq, d = h.shape
    tm = ROW_TILE
    width = w.shape[1]
    halo = 8
    return pl.pallas_call(
        _conv_kernel,
        out_shape=(jax.ShapeDtypeStruct((batch, seq, d), BF16),
                   jax.ShapeDtypeStruct((batch, seq, XATTN_WIDTH), BF16)),
        grid=(batch, seq // tm),
        in_specs=[
            pl.BlockSpec((1, tm, d), lambda b, i: (b, i, 0)),
            pl.BlockSpec((1, halo, d), lambda b, i: (b, jnp.maximum(i * (tm // halo) - 1, 0), 0)),
            pl.BlockSpec((1, d), lambda b, i: (0, 0)),
            pl.BlockSpec((d, width), lambda b, i: (0, 0)),
            pl.BlockSpec((CONV_WIDTH, d), lambda b, i: (0, 0)),
        ],
        out_specs=(pl.BlockSpec((1, tm, d), lambda b, i: (b, i, 0)),
                   pl.BlockSpec((1, tm, XATTN_WIDTH), lambda b, i: (b, i, 0))),
        compiler_params=pltpu.CompilerParams(
            dimension_semantics=("arbitrary", "arbitrary"),
            vmem_limit_bytes=_vmem_limit(48 * 1024 * 1024)),
        name="conv_front",
    )(h, h, g.reshape(1, d), w, cw)


def _mixout_kernel(h_ref, tok_ref, xq_ref, kv_ref, w_ref, o_ref):
    d = D_MODEL
    acc = h_ref[0] + _dot(tok_ref[0], w_ref[0, 0:d])
    xq = xq_ref[0]
    kv = kv_ref[0, 0]
    scale = XATTN_HEAD_DIM ** -0.5 * LOG2E
    outs = []
    for hh in range(XATTN_HEADS):
        lo = hh * XATTN_HEAD_DIM
        q = xq[:, lo:lo + XATTN_HEAD_DIM]
        k = kv[:, lo:lo + XATTN_HEAD_DIM]
        v = kv[:, XATTN_WIDTH + lo:XATTN_WIDTH + lo + XATTN_HEAD_DIM]
        s = _dot_nt(q, k) * scale
        e = jnp.exp2(s - jnp.max(s, axis=-1, keepdims=True))
        p = e / jnp.sum(e, axis=-1, keepdims=True)
        outs.append(_dot(p.astype(BF16), v))
    mo = jnp.concatenate(outs, axis=-1).astype(BF16)
    o_ref[0] = acc + _dot(mo, w_ref[0, d:])


def _mixout(h, tok, xq, kv, w, layer):
    batch, seq, d = h.shape
    tm = ROW_TILE
    return pl.pallas_call(
        _mixout_kernel,
        out_shape=jax.ShapeDtypeStruct((batch, seq, d), F32),
        grid=(batch, seq // tm),
        in_specs=[
            pl.BlockSpec((1, tm, d), lambda b, i: (b, i, 0)),
            pl.BlockSpec((1, tm, d), lambda b, i: (b, i, 0)),
            pl.BlockSpec((1, tm, XATTN_WIDTH), lambda b, i: (b, i, 0)),
            pl.BlockSpec((1, 1, N_MEM, 2 * XATTN_WIDTH), lambda b, i: (layer, b, 0, 0)),
            pl.BlockSpec((1, d + XATTN_WIDTH, d), lambda b, i: (layer, 0, 0)),
        ],
        out_specs=pl.BlockSpec((1, tm, d), lambda b, i: (b, i, 0)),
        compiler_params=pltpu.CompilerParams(
            dimension_semantics=("arbitrary", "arbitrary"),
            vmem_limit_bytes=_vmem_limit(40 * 1024 * 1024)),
        name="mixout",
    )(h, tok, xq, kv, w)


def _ffn_kernel(h_ref, g_ref, w1_ref, w2_ref, gf_ref, o_ref, *, final):
    h = h_ref[0]
    hn = _rms(h, g_ref[...]).astype(BF16)
    acc = h
    for c in range(FFN_HIDDEN // FFN_CHUNK):
        lo = c * FFN_CHUNK
        a = _dot(hn, w1_ref[0, :, lo:lo + FFN_CHUNK])
        u = _dot(hn, w1_ref[0, :, FFN_HIDDEN + lo:FFN_HIDDEN + lo + FFN_CHUNK])
        act = (a * _sigmoid(a) * u).astype(BF16)
        acc = acc + _dot(act, w2_ref[0, lo:lo + FFN_CHUNK])
    if final:
        acc = _rms(acc, gf_ref[...])
    o_ref[0] = acc


def _ffn(h, g, w1, w2, gf, layer, final):
    batch, seq, d = h.shape
    tm = FFN_ROW_TILE
    return pl.pallas_call(
        functools.partial(_ffn_kernel, final=final),
        out_shape=jax.ShapeDtypeStruct((batch, seq, d), F32),
        grid=(batch, seq // tm),
        in_specs=[
            pl.BlockSpec((1, tm, d), lambda b, i: (b, i, 0)),
            pl.BlockSpec((1, d), lambda b, i: (0, 0)),
            pl.BlockSpec((1, d, 2 * FFN_HIDDEN), lambda b, i: (layer, 0, 0)),
            pl.BlockSpec((1, FFN_HIDDEN, d), lambda b, i: (layer, 0, 0)),
            pl.BlockSpec((1, d), lambda b, i: (0, 0)),
        ],
        out_specs=pl.BlockSpec((1, tm, d), lambda b, i: (b, i, 0)),
        compiler_params=pltpu.CompilerParams(
            dimension_semantics=("arbitrary", "arbitrary"),
            vmem_limit_bytes=_vmem_limit(52 * 1024 * 1024)),
        name="ffn",
    )(h, g.reshape(1, d), w1, w2, gf.reshape(1, d))


_QW = NSA_HEADS * NSA_HEAD_DIM
_KVW = NSA_KV_GROUPS * NSA_HEAD_DIM
_PAIR = 2 * NSA_HEAD_DIM
_C_Q = 0
_C_CMP = _C_Q + _QW
_C_SLC = _C_CMP + 2 * _KVW
_C_WIN = _C_SLC + NSA_KV_GROUPS * _PAIR
_C_GATE = _C_WIN + NSA_KV_GROUPS * _PAIR
_C_XQ = _C_GATE + V7X_LANES
_C_END = _C_XQ + XATTN_WIDTH


_SUM_ROWS = 16
_VT_ROWS = _SUM_ROWS + NSA_HEAD_DIM


def _ones_and_vt(kv):
    v_t = kv.T[NSA_HEAD_DIM:]
    return jnp.concatenate([jnp.ones((_SUM_ROWS, kv.shape[0]), F32), v_t], axis=0).astype(BF16)


def _nsa_proj_kernel(h_ref, g_ref, w_ref, q_ref, cr_ref, slc_ref, slc1_ref, win_ref, win1_ref,
                     gt_ref, xq_ref):
    hn = _rms(h_ref[0], g_ref[...]).astype(BF16)
    q_ref[0] = (_dot(hn, w_ref[:, _C_Q:_C_CMP]) * (NSA_HEAD_DIM ** -0.5 * LOG2E)).astype(BF16)
    cr = _dot(hn, w_ref[:, _C_CMP:_C_SLC])
    for p in range(2 * _KVW // V7X_LANES):
        cr_ref[0, p] = cr[:, p * V7X_LANES:(p + 1) * V7X_LANES]
    s = _dot(hn, w_ref[:, _C_SLC:_C_WIN])
    w = _dot(hn, w_ref[:, _C_WIN:_C_GATE])
    k_lanes = lax.broadcasted_iota(jnp.int32, (s.shape[0], _PAIR), 1) < NSA_HEAD_DIM
    for g in range(NSA_KV_GROUPS):
        sg = s[:, g * _PAIR:(g + 1) * _PAIR]
        wg = w[:, g * _PAIR:(g + 1) * _PAIR]
        slc_ref[0, g] = jnp.where(k_lanes, sg, 0.0).astype(BF16)
        win_ref[0, g] = jnp.where(k_lanes, wg, 0.0).astype(BF16)
        for c in range(ROW_TILE // KV_TILE):
            rows = slice(c * KV_TILE, (c + 1) * KV_TILE)
            slc1_ref[0, g, c] = _ones_and_vt(sg[rows])
            win1_ref[0, g, c] = _ones_and_vt(wg[rows])
    gt_ref[0] = _sigmoid(_dot(hn, w_ref[:, _C_GATE:_C_XQ]))
    xq_ref[0] = _dot(hn, w_ref[:, _C_XQ:_C_END]).astype(BF16)


def _nsa_proj(h, g, w):
    batch, seq, d = h.shape
    tm = ROW_TILE
    grp = NSA_KV_GROUPS
    row = lambda b, i: (b, i, 0)
    pair_shape = jax.ShapeDtypeStruct((batch, grp, seq, _PAIR), BF16)
    pair_spec = pl.BlockSpec((1, grp, tm, _PAIR), lambda b, i: (b, 0, i, 0))
    vt_shape = jax.ShapeDtypeStruct((batch, grp, seq // KV_TILE, _VT_ROWS, KV_TILE), BF16)
    vt_spec = pl.BlockSpec((1, grp, tm // KV_TILE, _VT_ROWS, KV_TILE), lambda b, i: (b, 0, i, 0, 0))
    return pl.pallas_call(
        _nsa_proj_kernel,
        out_shape=(jax.ShapeDtypeStruct((batch, seq, _QW), BF16),
                   jax.ShapeDtypeStruct((batch, 2 * _KVW // V7X_LANES, seq, V7X_LANES), F32),
                   pair_shape, vt_shape, pair_shape, vt_shape,
                   jax.ShapeDtypeStruct((batch, seq, V7X_LANES), F32),
                   jax.ShapeDtypeStruct((batch, seq, XATTN_WIDTH), BF16)),
        grid=(batch, seq // tm),
        in_specs=[
            pl.BlockSpec((1, tm, d), row),
            pl.BlockSpec((1, d), lambda b, i: (0, 0)),
            pl.BlockSpec((d, _C_END), lambda b, i: (0, 0)),
        ],
        out_specs=(pl.BlockSpec((1, tm, _QW), row),
                   pl.BlockSpec((1, 2 * _KVW // V7X_LANES, tm, V7X_LANES), lambda b, i: (b, 0, i, 0)),
                   pair_spec, vt_spec, pair_spec, vt_spec,
                   pl.BlockSpec((1, tm, V7X_LANES), row),
                   pl.BlockSpec((1, tm, XATTN_WIDTH), row)),
        compiler_params=pltpu.CompilerParams(
            dimension_semantics=("arbitrary", "arbitrary"),
            vmem_limit_bytes=_vmem_limit(48 * 1024 * 1024)),
        name="nsa_proj",
    )(h, g.reshape(1, d), w)


_CHUNKS = 512


def _compress_kernel(raw_ref, pos_ref, w1_ref, w2_ref, o_ref, h2_scr, *, transposed):
    dk = NSA_HEAD_DIM
    half = CMP_BLOCK // 2
    h1 = [jnp.zeros((_CHUNKS, CMP_HIDDEN), F32) for _ in range(NSA_KV_GROUPS)]
    h2 = [jnp.zeros((_CHUNKS, CMP_HIDDEN), F32) for _ in range(NSA_KV_GROUPS)]
    gpl = V7X_LANES // dk
    for l in range(half):
        for p in range(NSA_KV_GROUPS // gpl):
            lanes = slice(p * V7X_LANES, (p + 1) * V7X_LANES)
            rows = raw_ref[0, p, pl.ds(l, _CHUNKS, stride=CMP_STRIDE), :]
            a1 = (rows + pos_ref[0, l:l + 1, lanes]).astype(BF16)
            a2 = (rows + pos_ref[0, half + l:half + l + 1, lanes]).astype(BF16)
            for gg in range(gpl):
                g = p * gpl + gg
                h1[g] = h1[g] + _dot(a1[:, gg * dk:(gg + 1) * dk], w1_ref[0, l * dk:(l + 1) * dk, :])
                h2[g] = h2[g] + _dot(a2[:, gg * dk:(gg + 1) * dk],
                                     w1_ref[0, (half + l) * dk:(half + l + 1) * dk, :])
    h2_scr[_CHUNKS:_CHUNKS + 8, :] = jnp.zeros((8, CMP_HIDDEN), F32)
    for g in range(NSA_KV_GROUPS):
        h2_scr[0:_CHUNKS, :] = h2[g]
        hm = h1[g] + h2_scr[pl.ds(1, _CHUNKS), :]
        hm = (hm * _sigmoid(hm)).astype(BF16)
        if transposed:
            o_ref[0, g] = _dot_nt(w2_ref[0], hm).astype(BF16)
        else:
            out = _dot(hm, w2_ref[0])
            o_ref[0, g] = jnp.concatenate([out, jnp.zeros_like(out)], axis=1).astype(BF16)


def _compress(raw, pos4, w1, w2, which, transposed):
    batch, _, seq, _ = raw.shape
    grp = NSA_KV_GROUPS
    tiles = _KVW // V7X_LANES
    if transposed:
        out_shape = (batch, grp, NSA_HEAD_DIM, N_CMP_PAD)
        w2_block = (1, NSA_HEAD_DIM, CMP_HIDDEN)
    else:
        out_shape = (batch, grp, N_CMP_PAD, _PAIR)
        w2_block = (1, CMP_HIDDEN, NSA_HEAD_DIM)
    return pl.pallas_call(
        functools.partial(_compress_kernel, transposed=transposed),
        out_shape=jax.ShapeDtypeStruct(out_shape, BF16),
        grid=(batch,),
        in_specs=[
            pl.BlockSpec((1, tiles, seq, V7X_LANES), lambda b: (b, which, 0, 0)),
            pl.BlockSpec((1, CMP_BLOCK, _KVW), lambda b: (which, 0, 0)),
            pl.BlockSpec((1, CMP_BLOCK * NSA_HEAD_DIM, CMP_HIDDEN), lambda b: (which, 0, 0)),
            pl.BlockSpec(w2_block, lambda b: (0, 0, 0)),
        ],
        out_specs=pl.BlockSpec((1,) + out_shape[1:], lambda b: (b, 0, 0, 0)),
        scratch_shapes=[pltpu.VMEM((_CHUNKS + 8, CMP_HIDDEN), F32)],
        compiler_params=pltpu.CompilerParams(
            dimension_semantics=("arbitrary",),
            vmem_limit_bytes=_vmem_limit(40 * 1024 * 1024)),
        name="compress_v" if transposed else "compress_k",
    )(raw, pos4, w1, w2)


def _t5_bucket(dist):
    n = jnp.maximum(dist, 0)
    max_exact = REL_BUCKETS // 2
    nf = jnp.maximum(n, 1).astype(F32)
    large = max_exact + (jnp.log(nf / max_exact) / np.log(REL_MAX_DIST / max_exact)
                         * (REL_BUCKETS - max_exact)).astype(jnp.int32)
    large = jnp.minimum(large, REL_BUCKETS - 1)
    return jnp.where(n < max_exact, n, large)


def _table_kernel(thr_ref, rb_ref, cbt_ref, tslc_ref, twin_ref):
    tq = Q_TILE
    g = pl.program_id(0)

    def entries(dist, h, limit):
        val = jnp.full(dist.shape, rb_ref[0, h], F32)
        for k in range(1, REL_BUCKETS):
            val = jnp.where(dist >= thr_ref[k], rb_ref[k, h], val)
        val = (val - rb_ref[REL_BUCKETS - 1, h]) * LOG2E
        return jnp.where(jnp.where(dist >= 0, dist, limit) < limit, val, NEG)

    def far_entries(dist, limit):
        return jnp.where(jnp.where(dist >= 0, dist, limit) < limit, 0.0, NEG)

    def fill(ref, band, dist_of_rows, n_rows, limit, h):
        step = 64
        for r0 in range(0, n_rows, step):
            shape = (step, tq)
            dist = dist_of_rows(r0 + lax.broadcasted_iota(jnp.int32, shape, 0),
                                lax.broadcasted_iota(jnp.int32, shape, 1))
            lo, hi = dist_of_rows(r0 + step - 1, 0), dist_of_rows(r0, tq - 1)
            near = hi >= 0 and lo < REL_MAX_DIST
            ref[0, r0:r0 + step, band] = entries(dist, h, limit) if near else far_entries(dist, limit)

    no_limit = 1 << 30
    for j in range(NSA_HPG):
        h = g * NSA_HPG + j
        band = slice(j * tq, (j + 1) * tq)
        fill(cbt_ref, band, lambda u, q: q - (CMP_BLOCK - 1) - CMP_STRIDE * (u - N_CMP_PAD),
             2 * N_CMP_PAD, no_limit, h)
        fill(tslc_ref, band, lambda u, q: q - u + tq, 2 * tq, no_limit, h)
        fill(twin_ref, band, lambda u, q: q - u + WINDOW, WINDOW + tq, WINDOW, h)


def _bias_tables(rel_bias):
    tq = Q_TILE
    rows = NSA_HPG * tq
    buckets = _t5_bucket(jnp.arange(REL_MAX_DIST + 1))
    thr = jnp.sum(buckets[None, :] < jnp.arange(REL_BUCKETS)[:, None], axis=1).astype(jnp.int32)
    smem = pl.BlockSpec(memory_space=pltpu.SMEM)
    return pl.pallas_call(
        _table_kernel,
        out_shape=(jax.ShapeDtypeStruct((NSA_KV_GROUPS, 2 * N_CMP_PAD, rows), F32),
                   jax.ShapeDtypeStruct((NSA_KV_GROUPS, 2 * tq, rows), F32),
                   jax.ShapeDtypeStruct((NSA_KV_GROUPS, WINDOW + tq, rows), F32)),
        grid=(NSA_KV_GROUPS,),
        in_specs=[smem, smem],
        out_specs=(pl.BlockSpec((1, 2 * N_CMP_PAD, rows), lambda g: (g, 0, 0)),
                   pl.BlockSpec((1, 2 * tq, rows), lambda g: (g, 0, 0)),
                   pl.BlockSpec((1, WINDOW + tq, rows), lambda g: (g, 0, 0))),
        compiler_params=pltpu.CompilerParams(
            dimension_semantics=("arbitrary",),
            vmem_limit_bytes=_vmem_limit(40 * 1024 * 1024)),
        name="bias_tables",
    )(thr, rel_bias.astype(F32))


def _block_mask_rows(seq):
    pos = jnp.arange(seq, dtype=jnp.int32)[:, None]
    col = jnp.arange(N_SEL, dtype=jnp.int32)[None, :]
    onehot = jnp.where(col == pos // SEL_BLOCK, NEG, 0.0)
    return jnp.concatenate([onehot, jnp.full((Q_TILE, N_SEL), NEG)], axis=0).astype(BF16)


def _col_softmax(s):
    m = jnp.max(s, axis=0, keepdims=True)
    return m, jnp.exp2(s - m)


def _normalise_t(acc):
    return acc[_SUM_ROWS:] * (1.0 / acc[0:1])


def _nsa_kernel(q_ref, kc_ref, vct_ref, slc_ref, slc1_ref, win_ref, win1_ref, bm_ref, gt_ref,
                cbt_ref, tslc_ref, twin_ref, o_ref, ps_scr, lhs_scr, m_scr, acc_scr, gt_scr, s_scr,
                ocmp_scr):
    tq = Q_TILE
    hpg = NSA_HPG
    dk = NSA_HEAD_DIM
    seq = slc_ref.shape[2]
    g = pl.program_id(1)
    i = pl.program_id(2)
    q0 = pl.multiple_of(i * tq, tq)
    bands = [slice(j * tq, (j + 1) * tq) for j in range(hpg)]

    q_t = q_ref[0].astype(F32).T
    for j in range(hpg):
        lhs_scr[0:dk, bands[j]] = q_t[j * dk:(j + 1) * dk].astype(BF16)
    lhs_scr[dk:dk + _SUM_ROWS, :] = jnp.ones((_SUM_ROWS, hpg * tq), BF16)
    lhs_scr[dk + _SUM_ROWS:_PAIR, :] = jnp.zeros((_PAIR - dk - _SUM_ROWS, hpg * tq), BF16)
    qs_t = lhs_scr[0:_PAIR, :]

    n_prev = WINDOW // tq
    mask_lane = lax.broadcasted_iota(jnp.int32, (1, _PAIR), 1) == dk
    tiles, v1ts = [], []
    for r in range(n_prev, 0, -1):
        tile = jnp.maximum(i - r, 0)
        missing = jnp.where(jnp.logical_and(mask_lane, i < r), NEG, 0.0).astype(BF16)
        tiles.append(win_ref[0, 0, pl.ds(pl.multiple_of(tile * tq, tq), tq), :] + missing)
        v1ts.append(win1_ref[0, 0, tile])
    tiles.append(win_ref[0, 0, pl.ds(q0, tq), :])
    v1ts.append(win1_ref[0, 0, i])
    _, p = _col_softmax(_dot(jnp.concatenate(tiles, axis=0), qs_t) + twin_ref[0])
    o_win_t = _normalise_t(_dot(jnp.concatenate(v1ts, axis=1), p.astype(BF16)))

    def compress_and_select(n_cmp):
        n_sel = n_cmp * CMP_STRIDE // SEL_BLOCK
        st = _dot(kc_ref[0, 0, 0:n_cmp, :], qs_t)
        st = st + cbt_ref[0, pl.ds(pl.multiple_of(N_CMP_PAD - (tq // CMP_STRIDE) * i, 8), n_cmp), :]
        mx = jnp.maximum(jnp.max(st, axis=0, keepdims=True), 0.1 * NEG)
        e = jnp.exp2(st - mx)
        den = jnp.sum(e, axis=0, keepdims=True)
        inv = jnp.where(den > 0.0, 1.0 / den, 0.0)
        ocmp_scr[...] = _dot(vct_ref[0, 0, :, 0:n_cmp], e.astype(BF16)) * inv
        psum = e[:, bands[0]] * inv[:, bands[0]]
        for j in range(1, hpg):
            psum = psum + e[:, bands[j]] * inv[:, bands[j]]

        ratio = SEL_BLOCK // CMP_STRIDE
        imp_tiles = []
        for c in range(tq // V7X_LANES):
            ps_scr[c, 0:8, :] = jnp.zeros((8, V7X_LANES), F32)
            ps_scr[c, 8:8 + n_cmp, :] = psum[:, c * V7X_LANES:(c + 1) * V7X_LANES]
            taps = [ps_scr[c, pl.ds(7 + r, n_sel, stride=ratio), :] for r in range(5)]
            imp_tiles.append(0.5 * taps[0] + taps[1] + taps[2] + taps[3] + 0.5 * taps[4])
        imp = jnp.concatenate(imp_tiles, axis=1)

        jrow = lax.broadcasted_iota(jnp.int32, (n_sel, tq), 0)
        cur = (q0 + lax.broadcasted_iota(jnp.int32, (n_sel, tq), 1)) // SEL_BLOCK
        imp = jnp.where(jrow <= cur, imp, -SEL_FORCE)
        for forced_block in (0, cur, cur - 1):
            imp = jnp.where(jrow == forced_block, SEL_FORCE, imp)
        taken = -3.0e38
        for _ in range(SEL_TOPK):
            top = jnp.max(imp, axis=0, keepdims=True)
            first = jnp.min(jnp.where(imp == top, jrow, n_sel), axis=0, keepdims=True)
            imp = jnp.where(jrow == first, taken, imp)
        not_sel = jnp.where(imp == taken, 0.0, 1.0).astype(BF16)
        for j in range(hpg):
            lhs_scr[_PAIR:_PAIR + n_sel, bands[j]] = not_sel
        if n_sel < N_SEL:
            lhs_scr[_PAIR + n_sel:2 * _PAIR, :] = jnp.ones((N_SEL - n_sel, hpg * tq), BF16)

    tiles_per_variant = (seq // tq) // CMP_VARIANTS
    for v in range(CMP_VARIANTS):
        pl.when(i // tiles_per_variant == v)(
            functools.partial(compress_and_select, N_CMP_PAD * (v + 1) // CMP_VARIANTS))
    lhs = lhs_scr[...]

    def key_tile(kv_ref, tile, mask_tile):
        return jnp.concatenate([kv_ref[0, 0, pl.ds(pl.multiple_of(tile * tq, tq), tq), :],
                                bm_ref[pl.ds(pl.multiple_of(mask_tile * tq, tq), tq), :]], axis=1)

    all_masked = seq // tq

    prev = jnp.maximum(i - 1, 0)
    keys = jnp.concatenate([key_tile(slc_ref, prev, jnp.where(i == 0, all_masked, i - 1)),
                            key_tile(slc_ref, i, i)], axis=0)
    v1t = jnp.concatenate([slc1_ref[0, 0, prev], slc1_ref[0, 0, i]], axis=1)
    s_near = _dot(keys, lhs)

    n_far = jnp.maximum(i - 1, 0)
    n_groups = (n_far + FAR_GROUP - 1) // FAR_GROUP

    def tile_or_none(kt):
        valid = kt < n_far
        return jnp.where(valid, kt, 0), jnp.where(valid, kt, all_masked)

    def far_logits(kg):
        tiles = [key_tile(slc_ref, *tile_or_none(FAR_GROUP * kg + t)) for t in range(FAR_GROUP)]
        return _dot(jnp.concatenate(tiles, axis=0), lhs_scr[...])

    def absorb(slot, kg):
        s = s_scr[slot]
        m_old = m_scr[0:1, :]
        m_new = jnp.maximum(m_old, jnp.max(s, axis=0, keepdims=True))
        p = jnp.exp2(s - m_new)
        v1t = jnp.concatenate([slc1_ref[0, 0, tile_or_none(FAR_GROUP * kg + t)[0]]
                               for t in range(FAR_GROUP)], axis=1)
        acc_scr[...] = jnp.exp2(m_old - m_new) * acc_scr[...] + _dot(v1t, p.astype(BF16))
        m_scr[0:1, :] = m_new

    def far_trip(base, groups):
        for t in range(groups):
            s_scr[(t + 1) % 2] = far_logits(base + t + 1)
            absorb(t % 2, base + t)

    s_scr[0] = far_logits(0)
    m, p = _col_softmax(s_near + tslc_ref[0])
    m_scr[0:1, :] = m
    acc_scr[...] = _dot(v1t, p.astype(BF16))
    n_long = n_groups // 2

    def long_body(k, carry):
        far_trip(2 * k, 2)
        return carry

    lax.fori_loop(0, n_long, long_body, 0)

    @pl.when(n_groups > 2 * n_long)
    def _():
        far_trip(2 * n_long, 1)

    o_slc_t = _normalise_t(acc_scr[...])

    gt_scr[...] = gt_ref[0].T
    merged = []
    for j in range(hpg):
        h = g * hpg + j
        gate = [gt_scr[pl.ds(br * NSA_HEADS + h, 1), :] for br in range(3)]
        merged.append(gate[0] * ocmp_scr[:, bands[j]] + gate[1] * o_slc_t[:, bands[j]]
                      + gate[2] * o_win_t[:, bands[j]])
    o_ref[0] = jnp.concatenate(merged, axis=0).T.astype(BF16)


def _nsa_attention(q, kc, vct, slc, slc1, win, win1, block_mask, gates, cbt, tslc, twin):
    batch, seq, _ = q.shape
    grp = NSA_KV_GROUPS
    tq = Q_TILE
    rows = NSA_HPG * tq
    per_group = lambda b, g, i: (b, g, 0, 0)
    per_group_t = lambda b, g, i: (b, g, 0, 0, 0)
    return pl.pallas_call(
        _nsa_kernel,
        out_shape=jax.ShapeDtypeStruct((batch, seq, _QW), BF16),
        grid=(batch, grp, seq // tq),
        in_specs=[
            pl.BlockSpec((1, tq, NSA_HPG * NSA_HEAD_DIM), lambda b, g, i: (b, i, g)),
            _resident((1, 1, N_CMP_PAD, _PAIR), per_group),
            _resident((1, 1, NSA_HEAD_DIM, N_CMP_PAD), per_group),
            _resident((1, 1, seq, _PAIR), per_group),
            _resident((1, 1, seq // tq, _VT_ROWS, tq), per_group_t),
            _resident((1, 1, seq, _PAIR), per_group),
            _resident((1, 1, seq // tq, _VT_ROWS, tq), per_group_t),
            _resident((seq + tq, N_SEL), lambda b, g, i: (0, 0)),
            pl.BlockSpec((1, tq, V7X_LANES), lambda b, g, i: (b, i, 0)),
            _resident((1, 2 * N_CMP_PAD, rows), lambda b, g, i: (g, 0, 0)),
            _resident((1, 2 * tq, rows), lambda b, g, i: (g, 0, 0)),
            _resident((1, WINDOW + tq, rows), lambda b, g, i: (g, 0, 0)),
        ],
        out_specs=pl.BlockSpec((1, tq, NSA_HPG * NSA_HEAD_DIM), lambda b, g, i: (b, i, g)),
        scratch_shapes=[pltpu.VMEM((tq // V7X_LANES, N_CMP_PAD + 16, V7X_LANES), F32),
                        pltpu.VMEM((2 * _PAIR, rows), BF16),
                        pltpu.VMEM((8, rows), F32),
                        pltpu.VMEM((_VT_ROWS, rows), F32),
                        pltpu.VMEM((V7X_LANES, tq), F32),
                        pltpu.VMEM((2, FAR_GROUP * tq, rows), F32),
                        pltpu.VMEM((NSA_HEAD_DIM, rows), F32)],
        compiler_params=pltpu.CompilerParams(
            dimension_semantics=("arbitrary", "arbitrary", "arbitrary"),
            vmem_limit_bytes=_vmem_limit(52 * 1024 * 1024)),
        name="nsa_attention",
    )(q, kc, vct, slc, slc1, win, win1, block_mask, gates, cbt, tslc, twin)


def _nsa_weight(w):
    d = w.shape[0]
    q = w[:, :_QW]
    kv = w[:, _QW:_QW + 6 * _KVW].reshape(d, 6, NSA_KV_GROUPS, NSA_HEAD_DIM)
    cmp_raw = kv[:, 0:2].reshape(d, 2 * _KVW)
    slc = jnp.stack([kv[:, 2], kv[:, 3]], axis=2).reshape(d, NSA_KV_GROUPS * _PAIR)
    win = jnp.stack([kv[:, 4], kv[:, 5]], axis=2).reshape(d, NSA_KV_GROUPS * _PAIR)
    gates = w[:, _QW + 6 * _KVW:_QW + 6 * _KVW + 3 * NSA_HEADS]
    gates = jnp.pad(gates, ((0, 0), (0, V7X_LANES - 3 * NSA_HEADS)))
    xq = w[:, _QW + 6 * _KVW + 3 * NSA_HEADS:]
    return jnp.concatenate([q, cmp_raw, slc, win, gates, xq], axis=1).astype(BF16)


def kernel(x, mem, norm_mix_g, norm_mem_g, norm_ffn_g, norm_final_g, w_in_conv, conv_w, w_in_nsa,
           cmp_pos_k, cmp_w1_k, cmp_w2_k, cmp_pos_v, cmp_w1_v, cmp_w2_v, rel_bias, w_mem_kv, w_out,
           w_ffn_in, w_ffn_out):
    seq = x.shape[1]
    assert seq % ROW_TILE == 0 and seq % Q_TILE == 0 and x.shape[2] == D_MODEL
    assert seq == CMP_STRIDE * _CHUNKS and seq == SEL_BLOCK * N_SEL and Q_TILE == KV_TILE
    memkv = _memkv(mem, norm_mem_g, w_mem_kv.astype(BF16))
    w_out_b = w_out.astype(BF16)
    w_ffn_in_b = w_ffn_in.astype(BF16)
    w_ffn_out_b = w_ffn_out.astype(BF16)

    tok, xq = _conv_front(x, norm_mix_g[0], w_in_conv[0].astype(BF16), conv_w[0])
    h = _mixout(x, tok, xq, memkv, w_out_b, 0)
    h = _ffn(h, norm_ffn_g[0], w_ffn_in_b, w_ffn_out_b, norm_final_g, 0, final=False)

    q, cmp_raw, slc, slc1, win, win1, gates, xq = _nsa_proj(h, norm_mix_g[1], _nsa_weight(w_in_nsa[0]))
    pos4 = jnp.stack([jnp.tile(cmp_pos_k[0], (1, NSA_KV_GROUPS)),
                      jnp.tile(cmp_pos_v[0], (1, NSA_KV_GROUPS))])
    w1 = jnp.stack([cmp_w1_k[0], cmp_w1_v[0]]).astype(BF16)
    kc = _compress(cmp_raw, pos4, w1, cmp_w2_k[0].astype(BF16)[None], 0, transposed=False)
    vct = _compress(cmp_raw, pos4, w1, cmp_w2_v[0].T.astype(BF16)[None], 1, transposed=True)
    tok = _nsa_attention(q, kc, vct, slc, slc1, win, win1, _block_mask_rows(seq), gates,
                         *_bias_tables(rel_bias))
    h = _mixout(h, tok, xq, memkv, w_out_b, 1)
    return _ffn(h, norm_ffn_g[1], w_ffn_in_b, w_ffn_out_b, norm_final_g, 1, final=True)
```

```python
import functools

import numpy as np
import jax
import jax.numpy as jnp
from jax import lax
from jax.experimental import pallas as pl
from jax.experimental.pallas import tpu as pltpu

F32 = jnp.float32
BF16 = jnp.bfloat16

D_MODEL = 1024
N_MEM = 256
CONV_WIDTH = 3
NSA_HEADS = 16
NSA_HEAD_DIM = 64
NSA_KV_GROUPS = 4
NSA_HPG = NSA_HEADS // NSA_KV_GROUPS
CMP_BLOCK = 32
CMP_STRIDE = 16
CMP_HIDDEN = 2 * NSA_HEAD_DIM
SEL_BLOCK = 64
SEL_TOPK = 16
SEL_FORCE = 1.0e4
WINDOW = 512
XATTN_HEADS = 4
XATTN_HEAD_DIM = 128
XATTN_WIDTH = XATTN_HEADS * XATTN_HEAD_DIM
FFN_HIDDEN = 2816
REL_BUCKETS = 32
REL_MAX_DIST = 128
RMS_EPS = 1e-6
NEG = -1e30
LOG2E = 1.4426950408889634

V7X_LANES = 128
V7X_VMEM_BYTES = 64 * 1024 * 1024

ROW_TILE = 512
FFN_ROW_TILE = 256
FFN_CHUNK = 1408
Q_TILE = 256
KV_TILE = 256
FAR_GROUP = 2
CMP_VARIANTS = 4
N_CMP_PAD = 512
N_SEL = 128

_NT = (((1,), (1,)), ((), ()))


def _vmem_limit(nbytes):
    return int(min(max(nbytes, 16 * 1024 * 1024), V7X_VMEM_BYTES - 8 * 1024 * 1024))


def _rms(x, g):
    ms = jnp.mean(x * x, axis=-1, keepdims=True)
    return x * lax.rsqrt(ms + RMS_EPS) * g


def _dot(a, b):
    return jnp.dot(a, b, preferred_element_type=F32)


def _dot_nt(a, b):
    return lax.dot_general(a, b, _NT, preferred_element_type=F32)


def _sigmoid(x):
    return 1.0 / (1.0 + jnp.exp(-x))


def _resident(block_shape, index_map):
    return pl.BlockSpec(block_shape, index_map, pipeline_mode=pl.Buffered(1))


def _memkv_kernel(mem_ref, g_ref, w_ref, o_ref):
    xn = _rms(mem_ref[0], g_ref[0]).astype(BF16)
    o_ref[0, 0] = _dot(xn, w_ref[0]).astype(BF16)


def _memkv(mem, gains, w):
    depth, batch = w.shape[0], mem.shape[0]
    return pl.pallas_call(
        _memkv_kernel,
        out_shape=jax.ShapeDtypeStruct((depth, batch, N_MEM, 2 * XATTN_WIDTH), BF16),
        grid=(depth, batch),
        in_specs=[
            pl.BlockSpec((1, N_MEM, D_MODEL), lambda l, b: (b, 0, 0)),
            pl.BlockSpec((1, 1, D_MODEL), lambda l, b: (l, 0, 0)),
            pl.BlockSpec((1, D_MODEL, 2 * XATTN_WIDTH), lambda l, b: (l, 0, 0)),
        ],
        out_specs=pl.BlockSpec((1, 1, N_MEM, 2 * XATTN_WIDTH), lambda l, b: (l, b, 0, 0)),
        compiler_params=pltpu.CompilerParams(
            dimension_semantics=("arbitrary", "arbitrary"),
            vmem_limit_bytes=_vmem_limit(24 * 1024 * 1024)),
        name="memkv",
    )(mem, gains.reshape(depth, 1, D_MODEL), w)


def _conv_kernel(h_ref, hp_ref, g_ref, w_ref, cw_ref, tok_ref, xq_ref):
    d = D_MODEL
    i = pl.program_id(1)
    g = g_ref[...]
    hn = _rms(h_ref[0], g).astype(BF16)
    u_b = _dot(hn, w_ref[:, 0:d])
    v = _dot(hn, w_ref[:, d:2 * d]) * _dot(hn, w_ref[:, 2 * d:3 * d])
    xq_ref[0] = _dot(hn, w_ref[:, 3 * d:]).astype(BF16)
    hpn = _rms(hp_ref[0], g).astype(BF16)
    vp = _dot(hpn, w_ref[:, d:2 * d]) * _dot(hpn, w_ref[:, 2 * d:3 * d])
    vp = jnp.where(i > 0, vp, 0.0)
    row = lax.broadcasted_iota(jnp.int32, v.shape, 0)
    v1 = jnp.where(row == 0, vp[7:8], pltpu.roll(v, 1, axis=0))
    v2 = jnp.where(row == 0, vp[6:7], jnp.where(row == 1, vp[7:8], pltpu.roll(v, 2, axis=0)))
    cw = cw_ref[...]
    conv = v2 * cw[0:1] + v1 * cw[1:2] + v * cw[2:3]
    tok_ref[0] = (u_b * conv).astype(BF16)


def _conv_front(h, g, w, cw):
    batch, seq, d = h.shape
    tm = ROW_TILE
    width = w.shape[1]
    halo = 8
    return pl.pallas_call(
        _conv_kernel,
        out_shape=(jax.ShapeDtypeStruct((batch, seq, d), BF16),
                   jax.ShapeDtypeStruct((batch, seq, XATTN_WIDTH), BF16)),
        grid=(batch, seq // tm),
        in_specs=[
            pl.BlockSpec((1, tm, d), lambda b, i: (b, i, 0)),
            pl.BlockSpec((1, halo, d), lambda b, i: (b, jnp.maximum(i * (tm // halo) - 1, 0), 0)),
            pl.BlockSpec((1, d), lambda b, i: (0, 0)),
            pl.BlockSpec((d, width), lambda b, i: (0, 0)),
            pl.BlockSpec((CONV_WIDTH, d), lambda b, i: (0, 0)),
        ],
        out_specs=(pl.BlockSpec((1, tm, d), lambda b, i: (b, i, 0)),
                   pl.BlockSpec((1, tm, XATTN_WIDTH), lambda b, i: (b, i, 0))),
        compiler_params=pltpu.CompilerParams(
            dimension_semantics=("arbitrary", "arbitrary"),
            vmem_limit_bytes=_vmem_limit(48 * 1024 * 1024)),
        name="conv_front",
    )(h, h, g.reshape(1, d), w, cw)


def _mixout_kernel(h_ref, tok_ref, xq_ref, kv_ref, w_ref, o_ref):
    d = D_MODEL
    acc = h_ref[0] + _dot(tok_ref[0], w_ref[0, 0:d])
    xq = xq_ref[0]
    kv = kv_ref[0, 0]
    scale = XATTN_HEAD_DIM ** -0.5 * LOG2E
    outs = []
    for hh in range(XATTN_HEADS):
        lo = hh * XATTN_HEAD_DIM
        q = xq[:, lo:lo + XATTN_HEAD_DIM]
        k = kv[:, lo:lo + XATTN_HEAD_DIM]
        v = kv[:, XATTN_WIDTH + lo:XATTN_WIDTH + lo + XATTN_HEAD_DIM]
        s = _dot_nt(q, k) * scale
        e = jnp.exp2(s - jnp.max(s, axis=-1, keepdims=True))
        p = e / jnp.sum(e, axis=-1, keepdims=True)
        outs.append(_dot(p.astype(BF16), v))
    mo = jnp.concatenate(outs, axis=-1).astype(BF16)
    o_ref[0] = acc + _dot(mo, w_ref[0, d:])


def _mixout(h, tok, xq, kv, w, layer):
    batch, seq, d = h.shape
    tm = ROW_TILE
    return pl.pallas_call(
        _mixout_kernel,
        out_shape=jax.ShapeDtypeStruct((batch, seq, d), F32),
        grid=(batch, seq // tm),
        in_specs=[
            pl.BlockSpec((1, tm, d), lambda b, i: (b, i, 0)),
            pl.BlockSpec((1, tm, d), lambda b, i: (b, i, 0)),
            pl.BlockSpec((1, tm, XATTN_WIDTH), lambda b, i: (b, i, 0)),
            pl.BlockSpec((1, 1, N_MEM, 2 * XATTN_WIDTH), lambda b, i: (layer, b, 0, 0)),
            pl.BlockSpec((1, d + XATTN_WIDTH, d), lambda b, i: (layer, 0, 0)),
        ],
        out_specs=pl.BlockSpec((1, tm, d), lambda b, i: (b, i, 0)),
        compiler_params=pltpu.CompilerParams(
            dimension_semantics=("arbitrary", "arbitrary"),
            vmem_limit_bytes=_vmem_limit(40 * 1024 * 1024)),
        name="mixout",
    )(h, tok, xq, kv, w)


def _ffn_kernel(h_ref, g_ref, w1_ref, w2_ref, gf_ref, o_ref, *, final):
    h = h_ref[0]
    hn = _rms(h, g_ref[...]).astype(BF16)
    acc = h
    for c in range(FFN_HIDDEN // FFN_CHUNK):
        lo = c * FFN_CHUNK
        a = _dot(hn, w1_ref[0, :, lo:lo + FFN_CHUNK])
        u = _dot(hn, w1_ref[0, :, FFN_HIDDEN + lo:FFN_HIDDEN + lo + FFN_CHUNK])
        act = (a * _sigmoid(a) * u).astype(BF16)
        acc = acc + _dot(act, w2_ref[0, lo:lo + FFN_CHUNK])
    if final:
        acc = _rms(acc, gf_ref[...])
    o_ref[0] = acc


def _ffn(h, g, w1, w2, gf, layer, final):
    batch, seq, d = h.shape
    tm = FFN_ROW_TILE
    return pl.pallas_call(
        functools.partial(_ffn_kernel, final=final),
        out_shape=jax.ShapeDtypeStruct((batch, seq, d), F32),
        grid=(batch, seq // tm),
        in_specs=[
            pl.BlockSpec((1, tm, d), lambda b, i: (b, i, 0)),
            pl.BlockSpec((1, d), lambda b, i: (0, 0)),
            pl.BlockSpec((1, d, 2 * FFN_HIDDEN), lambda b, i: (layer, 0, 0)),
            pl.BlockSpec((1, FFN_HIDDEN, d), lambda b, i: (layer, 0, 0)),
            pl.BlockSpec((1, d), lambda b, i: (0, 0)),
        ],
        out_specs=pl.BlockSpec((1, tm, d), lambda b, i: (b, i, 0)),
        compiler_params=pltpu.CompilerParams(
            dimension_semantics=("arbitrary", "arbitrary"),
            vmem_limit_bytes=_vmem_limit(52 * 1024 * 1024)),
        name="ffn",
    )(h, g.reshape(1, d), w1, w2, gf.reshape(1, d))


_QW = NSA_HEADS * NSA_HEAD_DIM
_KVW = NSA_KV_GROUPS * NSA_HEAD_DIM
_PAIR = 2 * NSA_HEAD_DIM
_C_Q = 0
_C_CMP = _C_Q + _QW
_C_SLC = _C_CMP + 2 * _KVW
_C_WIN = _C_SLC + NSA_KV_GROUPS * _PAIR
_C_GATE = _C_WIN + NSA_KV_GROUPS * _PAIR
_C_XQ = _C_GATE + V7X_LANES
_C_END = _C_XQ + XATTN_WIDTH


_SUM_ROWS = 16
_VT_ROWS = _SUM_ROWS + NSA_HEAD_DIM


def _ones_and_vt(kv):
    v_t = kv.T[NSA_HEAD_DIM:]
    return jnp.concatenate([jnp.ones((_SUM_ROWS, kv.shape[0]), F32), v_t], axis=0).astype(BF16)


def _nsa_proj_kernel(h_ref, g_ref, w_ref, q_ref, cr_ref, slc_ref, slc1_ref, win_ref, win1_ref,
                     gt_ref, xq_ref):
    hn = _rms(h_ref[0], g_ref[...]).astype(BF16)
    q = _dot(hn, w_ref[:, _C_Q:_C_CMP]) * (NSA_HEAD_DIM ** -0.5 * LOG2E)
    gates = _sigmoid(_dot(hn, w_ref[:, _C_GATE:_C_XQ]))
    for c in range(ROW_TILE // Q_TILE):
        rows = slice(c * Q_TILE, (c + 1) * Q_TILE)
        q_ref[0, c] = q[rows].T.astype(BF16)
        gt_ref[0, c] = gates[rows].T
    cr = _dot(hn, w_ref[:, _C_CMP:_C_SLC])
    for p in range(2 * _KVW // V7X_LANES):
        cr_ref[0, p] = cr[:, p * V7X_LANES:(p + 1) * V7X_LANES]
    s = _dot(hn, w_ref[:, _C_SLC:_C_WIN])
    w = _dot(hn, w_ref[:, _C_WIN:_C_GATE])
    k_lanes = lax.broadcasted_iota(jnp.int32, (s.shape[0], _PAIR), 1) < NSA_HEAD_DIM
    for g in range(NSA_KV_GROUPS):
        sg = s[:, g * _PAIR:(g + 1) * _PAIR]
        wg = w[:, g * _PAIR:(g + 1) * _PAIR]
        slc_ref[0, g] = jnp.where(k_lanes, sg, 0.0).astype(BF16)
        win_ref[0, g] = jnp.where(k_lanes, wg, 0.0).astype(BF16)
        for c in range(ROW_TILE // KV_TILE):
            rows = slice(c * KV_TILE, (c + 1) * KV_TILE)
            slc1_ref[0, g, c] = _ones_and_vt(sg[rows])
            win1_ref[0, g, c] = _ones_and_vt(wg[rows])
    xq_ref[0] = _dot(hn, w_ref[:, _C_XQ:_C_END]).astype(BF16)


def _nsa_proj(h, g, w):
    batch, seq, d = h.shape
    tm = ROW_TILE
    grp = NSA_KV_GROUPS
    row = lambda b, i: (b, i, 0)
    pair_shape = jax.ShapeDtypeStruct((batch, grp, seq, _PAIR), BF16)
    pair_spec = pl.BlockSpec((1, grp, tm, _PAIR), lambda b, i: (b, 0, i, 0))
    vt_shape = jax.ShapeDtypeStruct((batch, grp, seq // KV_TILE, _VT_ROWS, KV_TILE), BF16)
    vt_spec = pl.BlockSpec((1, grp, tm // KV_TILE, _VT_ROWS, KV_TILE), lambda b, i: (b, 0, i, 0, 0))
    return pl.pallas_call(
        _nsa_proj_kernel,
        out_shape=(jax.ShapeDtypeStruct((batch, seq // Q_TILE, _QW, Q_TILE), BF16),
                   jax.ShapeDtypeStruct((batch, 2 * _KVW // V7X_LANES, seq, V7X_LANES), F32),
                   pair_shape, vt_shape, pair_shape, vt_shape,
                   jax.ShapeDtypeStruct((batch, seq // Q_TILE, V7X_LANES, Q_TILE), F32),
                   jax.ShapeDtypeStruct((batch, seq, XATTN_WIDTH), BF16)),
        grid=(batch, seq // tm),
        in_specs=[
            pl.BlockSpec((1, tm, d), row),
            pl.BlockSpec((1, d), lambda b, i: (0, 0)),
            pl.BlockSpec((d, _C_END), lambda b, i: (0, 0)),
        ],
        out_specs=(pl.BlockSpec((1, tm // Q_TILE, _QW, Q_TILE), lambda b, i: (b, i, 0, 0)),
                   pl.BlockSpec((1, 2 * _KVW // V7X_LANES, tm, V7X_LANES), lambda b, i: (b, 0, i, 0)),
                   pair_spec, vt_spec, pair_spec, vt_spec,
                   pl.BlockSpec((1, tm // Q_TILE, V7X_LANES, Q_TILE), lambda b, i: (b, i, 0, 0)),
                   pl.BlockSpec((1, tm, XATTN_WIDTH), row)),
        compiler_params=pltpu.CompilerParams(
            dimension_semantics=("arbitrary", "arbitrary"),
            vmem_limit_bytes=_vmem_limit(48 * 1024 * 1024)),
        name="nsa_proj",
    )(h, g.reshape(1, d), w)


_CHUNKS = 512


def _compress_kernel(raw_ref, pos_ref, w1_ref, w2_ref, o_ref, h2_scr, *, transposed):
    dk = NSA_HEAD_DIM
    half = CMP_BLOCK // 2
    h1 = [jnp.zeros((_CHUNKS, CMP_HIDDEN), F32) for _ in range(NSA_KV_GROUPS)]
    h2 = [jnp.zeros((_CHUNKS, CMP_HIDDEN), F32) for _ in range(NSA_KV_GROUPS)]
    gpl = V7X_LANES // dk
    for l in range(half):
        for p in range(NSA_KV_GROUPS // gpl):
            lanes = slice(p * V7X_LANES, (p + 1) * V7X_LANES)
            rows = raw_ref[0, p, pl.ds(l, _CHUNKS, stride=CMP_STRIDE), :]
            a1 = (rows + pos_ref[0, l:l + 1, lanes]).astype(BF16)
            a2 = (rows + pos_ref[0, half + l:half + l + 1, lanes]).astype(BF16)
            for gg in range(gpl):
                g = p * gpl + gg
                h1[g] = h1[g] + _dot(a1[:, gg * dk:(gg + 1) * dk], w1_ref[0, l * dk:(l + 1) * dk, :])
                h2[g] = h2[g] + _dot(a2[:, gg * dk:(gg + 1) * dk],
                                     w1_ref[0, (half + l) * dk:(half + l + 1) * dk, :])
    h2_scr[_CHUNKS:_CHUNKS + 8, :] = jnp.zeros((8, CMP_HIDDEN), F32)
    for g in range(NSA_KV_GROUPS):
        h2_scr[0:_CHUNKS, :] = h2[g]
        hm = h1[g] + h2_scr[pl.ds(1, _CHUNKS), :]
        hm = (hm * _sigmoid(hm)).astype(BF16)
        if transposed:
            o_ref[0, g] = _dot_nt(w2_ref[0], hm).astype(BF16)
        else:
            out = _dot(hm, w2_ref[0])
            o_ref[0, g] = jnp.concatenate([out, jnp.zeros_like(out)], axis=1).astype(BF16)


def _compress(raw, pos4, w1, w2, which, transposed):
    batch, _, seq, _ = raw.shape
    grp = NSA_KV_GROUPS
    tiles = _KVW // V7X_LANES
    if transposed:
        out_shape = (batch, grp, NSA_HEAD_DIM, N_CMP_PAD)
        w2_block = (1, NSA_HEAD_DIM, CMP_HIDDEN)
    else:
        out_shape = (batch, grp, N_CMP_PAD, _PAIR)
        w2_block = (1, CMP_HIDDEN, NSA_HEAD_DIM)
    return pl.pallas_call(
        functools.partial(_compress_kernel, transposed=transposed),
        out_shape=jax.ShapeDtypeStruct(out_shape, BF16),
        grid=(batch,),
        in_specs=[
            pl.BlockSpec((1, tiles, seq, V7X_LANES), lambda b: (b, which, 0, 0)),
            pl.BlockSpec((1, CMP_BLOCK, _KVW), lambda b: (which, 0, 0)),
            pl.BlockSpec((1, CMP_BLOCK * NSA_HEAD_DIM, CMP_HIDDEN), lambda b: (which, 0, 0)),
            pl.BlockSpec(w2_block, lambda b: (0, 0, 0)),
        ],
        out_specs=pl.BlockSpec((1,) + out_shape[1:], lambda b: (b, 0, 0, 0)),
        scratch_shapes=[pltpu.VMEM((_CHUNKS + 8, CMP_HIDDEN), F32)],
        compiler_params=pltpu.CompilerParams(
            dimension_semantics=("arbitrary",),
            vmem_limit_bytes=_vmem_limit(40 * 1024 * 1024)),
        name="compress_v" if transposed else "compress_k",
    )(raw, pos4, w1, w2)


def _t5_bucket(dist):
    n = jnp.maximum(dist, 0)
    max_exact = REL_BUCKETS // 2
    nf = jnp.maximum(n, 1).astype(F32)
    large = max_exact + (jnp.log(nf / max_exact) / np.log(REL_MAX_DIST / max_exact)
                         * (REL_BUCKETS - max_exact)).astype(jnp.int32)
    large = jnp.minimum(large, REL_BUCKETS - 1)
    return jnp.where(n < max_exact, n, large)


def _table_kernel(thr_ref, rb_ref, cbt_ref, tslc_ref, twin_ref):
    tq = Q_TILE
    g = pl.program_id(0)

    def entries(dist, h, limit):
        val = jnp.full(dist.shape, rb_ref[0, h], F32)
        for k in range(1, REL_BUCKETS):
            val = jnp.where(dist >= thr_ref[k], rb_ref[k, h], val)
        val = (val - rb_ref[REL_BUCKETS - 1, h]) * LOG2E
        return jnp.where(jnp.where(dist >= 0, dist, limit) < limit, val, NEG)

    def far_entries(dist, limit):
        return jnp.where(jnp.where(dist >= 0, dist, limit) < limit, 0.0, NEG)

    def fill(ref, band, dist_of_rows, n_rows, limit, h):
        step = 64
        for r0 in range(0, n_rows, step):
            shape = (step, tq)
            dist = dist_of_rows(r0 + lax.broadcasted_iota(jnp.int32, shape, 0),
                                lax.broadcasted_iota(jnp.int32, shape, 1))
            lo, hi = dist_of_rows(r0 + step - 1, 0), dist_of_rows(r0, tq - 1)
            near = hi >= 0 and lo < REL_MAX_DIST
            ref[0, r0:r0 + step, band] = entries(dist, h, limit) if near else far_entries(dist, limit)

    no_limit = 1 << 30
    for j in range(NSA_HPG):
        h = g * NSA_HPG + j
        band = slice(j * tq, (j + 1) * tq)
        fill(cbt_ref, band, lambda u, q: q - (CMP_BLOCK - 1) - CMP_STRIDE * (u - N_CMP_PAD),
             2 * N_CMP_PAD, no_limit, h)
        fill(tslc_ref, band, lambda u, q: q - u + tq, 2 * tq, no_limit, h)
        fill(twin_ref, band, lambda u, q: q - u + WINDOW, WINDOW + tq, WINDOW, h)


def _bias_tables(rel_bias):
    tq = Q_TILE
    rows = NSA_HPG * tq
    buckets = _t5_bucket(jnp.arange(REL_MAX_DIST + 1))
    thr = jnp.sum(buckets[None, :] < jnp.arange(REL_BUCKETS)[:, None], axis=1).astype(jnp.int32)
    smem = pl.BlockSpec(memory_space=pltpu.SMEM)
    return pl.pallas_call(
        _table_kernel,
        out_shape=(jax.ShapeDtypeStruct((NSA_KV_GROUPS, 2 * N_CMP_PAD, rows), F32),
                   jax.ShapeDtypeStruct((NSA_KV_GROUPS, 2 * tq, rows), F32),
                   jax.ShapeDtypeStruct((NSA_KV_GROUPS, WINDOW + tq, rows), F32)),
        grid=(NSA_KV_GROUPS,),
        in_specs=[smem, smem],
        out_specs=(pl.BlockSpec((1, 2 * N_CMP_PAD, rows), lambda g: (g, 0, 0)),
                   pl.BlockSpec((1, 2 * tq, rows), lambda g: (g, 0, 0)),
                   pl.BlockSpec((1, WINDOW + tq, rows), lambda g: (g, 0, 0))),
        compiler_params=pltpu.CompilerParams(
            dimension_semantics=("arbitrary",),
            vmem_limit_bytes=_vmem_limit(40 * 1024 * 1024)),
        name="bias_tables",
    )(thr, rel_bias.astype(F32))


def _block_mask_rows(seq):
    pos = jnp.arange(seq, dtype=jnp.int32)[:, None]
    col = jnp.arange(N_SEL, dtype=jnp.int32)[None, :]
    onehot = jnp.where(col == pos // SEL_BLOCK, NEG, 0.0)
    return jnp.concatenate([onehot, jnp.full((Q_TILE, N_SEL), NEG)], axis=0).astype(BF16)


def _online_absorb(s, v1t, state):
    m = jnp.max(s, axis=0, keepdims=True)
    if state is not None:
        m = jnp.maximum(state[0], m)
    acc = _dot(v1t, jnp.exp2(s - m).astype(BF16))
    if state is not None:
        acc = jnp.exp2(state[0] - m) * state[1] + acc
    return m, acc


def _normalise_t(acc):
    return acc[_SUM_ROWS:] * (1.0 / acc[0:1])


def _nsa_kernel(q_ref, kc_ref, vct_ref, slc_ref, slc1_ref, win_ref, win1_ref, bm_ref, gt_ref,
                cbt_ref, tslc_ref, twin_ref, o_ref, ps_scr, lhs_scr, m_scr, acc_scr, s_scr, ocmp_scr):
    tq = Q_TILE
    hpg = NSA_HPG
    dk = NSA_HEAD_DIM
    seq = slc_ref.shape[2]
    g = pl.program_id(1)
    i = pl.program_id(2)
    q0 = pl.multiple_of(i * tq, tq)
    bands = [slice(j * tq, (j + 1) * tq) for j in range(hpg)]

    for j in range(hpg):
        lhs_scr[0:dk, bands[j]] = q_ref[0, 0, j * dk:(j + 1) * dk, :]
    lhs_scr[dk:dk + _SUM_ROWS, :] = jnp.ones((_SUM_ROWS, hpg * tq), BF16)
    lhs_scr[dk + _SUM_ROWS:_PAIR, :] = jnp.zeros((_PAIR - dk - _SUM_ROWS, hpg * tq), BF16)
    qs_t = lhs_scr[0:_PAIR, :]

    n_prev = WINDOW // tq
    mask_lane = lax.broadcasted_iota(jnp.int32, (1, _PAIR), 1) == dk

    def win_logits(r):
        tile = jnp.maximum(i - r, 0)
        keys = win_ref[0, 0, pl.ds(pl.multiple_of(tile * tq, tq), tq), :]
        if r:
            keys = keys + jnp.where(jnp.logical_and(mask_lane, i < r), NEG, 0.0).astype(BF16)
        rows = slice((n_prev - r) * tq, (n_prev - r + 1) * tq)
        return _dot(keys, qs_t) + twin_ref[0, rows, :]

    state = None
    s_cur = win_logits(0)
    for r in range(0, n_prev + 1):
        s_next = win_logits(r + 1) if r < n_prev else None
        state = _online_absorb(s_cur, win1_ref[0, 0, jnp.maximum(i - r, 0)], state)
        s_cur = s_next
    o_win_t = _normalise_t(state[1])

    def compress_and_select(n_cmp):
        n_sel = n_cmp * CMP_STRIDE // SEL_BLOCK
        st = _dot(kc_ref[0, 0, 0:n_cmp, :], qs_t)
        st = st + cbt_ref[0, pl.ds(pl.multiple_of(N_CMP_PAD - (tq // CMP_STRIDE) * i, 8), n_cmp), :]
        mx = jnp.maximum(jnp.max(st, axis=0, keepdims=True), 0.1 * NEG)
        e = jnp.exp2(st - mx)
        den = jnp.sum(e, axis=0, keepdims=True)
        inv = jnp.where(den > 0.0, 1.0 / den, 0.0)
        ocmp_scr[...] = _dot(vct_ref[0, 0, :, 0:n_cmp], e.astype(BF16)) * inv
        psum = e[:, bands[0]] * inv[:, bands[0]]
        for j in range(1, hpg):
            psum = psum + e[:, bands[j]] * inv[:, bands[j]]

        ratio = SEL_BLOCK // CMP_STRIDE
        imp_tiles = []
        for c in range(tq // V7X_LANES):
            ps_scr[c, 0:8, :] = jnp.zeros((8, V7X_LANES), F32)
            ps_scr[c, 8:8 + n_cmp, :] = psum[:, c * V7X_LANES:(c + 1) * V7X_LANES]
            taps = [ps_scr[c, pl.ds(7 + r, n_sel, stride=ratio), :] for r in range(5)]
            imp_tiles.append(0.5 * taps[0] + taps[1] + taps[2] + taps[3] + 0.5 * taps[4])
        imp = jnp.concatenate(imp_tiles, axis=1)

        jrow = lax.broadcasted_iota(jnp.int32, (n_sel, tq), 0)
        cur = (q0 + lax.broadcasted_iota(jnp.int32, (n_sel, tq), 1)) // SEL_BLOCK
        imp = jnp.where(jrow <= cur, imp, -SEL_FORCE)
        for forced_block in (0, cur, cur - 1):
            imp = jnp.where(jrow == forced_block, SEL_FORCE, imp)
        taken = -3.0e38
        for _ in range(SEL_TOPK):
            top = jnp.max(imp, axis=0, keepdims=True)
            first = jnp.min(jnp.where(imp == top, jrow, n_sel), axis=0, keepdims=True)
            imp = jnp.where(jrow == first, taken, imp)
        not_sel = jnp.where(imp == taken, 0.0, 1.0).astype(BF16)
        for j in range(hpg):
            lhs_scr[_PAIR:_PAIR + n_sel, bands[j]] = not_sel
        if n_sel < N_SEL:
            lhs_scr[_PAIR + n_sel:2 * _PAIR, :] = jnp.ones((N_SEL - n_sel, hpg * tq), BF16)

    tiles_per_variant = (seq // tq) // CMP_VARIANTS
    for v in range(CMP_VARIANTS):
        pl.when(i // tiles_per_variant == v)(
            functools.partial(compress_and_select, N_CMP_PAD * (v + 1) // CMP_VARIANTS))
    lhs = lhs_scr[...]

    def key_tile(kv_ref, tile, mask_tile):
        return jnp.concatenate([kv_ref[0, 0, pl.ds(pl.multiple_of(tile * tq, tq), tq), :],
                                bm_ref[pl.ds(pl.multiple_of(mask_tile * tq, tq), tq), :]], axis=1)

    all_masked = seq // tq

    prev = jnp.maximum(i - 1, 0)
    s_own = _dot(key_tile(slc_ref, i, i), lhs) + tslc_ref[0, tq:2 * tq, :]
    s_prev = (_dot(key_tile(slc_ref, prev, jnp.where(i == 0, all_masked, i - 1)), lhs)
              + tslc_ref[0, 0:tq, :])

    n_far = jnp.maximum(i - 1, 0)
    n_groups = (n_far + FAR_GROUP - 1) // FAR_GROUP

    def tile_or_none(kt):
        valid = kt < n_far
        return jnp.where(valid, kt, 0), jnp.where(valid, kt, all_masked)

    def far_logits(kg):
        tiles = [key_tile(slc_ref, *tile_or_none(FAR_GROUP * kg + t)) for t in range(FAR_GROUP)]
        return _dot(jnp.concatenate(tiles, axis=0), lhs_scr[...])

    def absorb(slot, kg):
        s = s_scr[slot]
        m_old = m_scr[0:1, :]
        m_new = jnp.maximum(m_old, jnp.max(s, axis=0, keepdims=True))
        p = jnp.exp2(s - m_new)
        v1t = jnp.concatenate([slc1_ref[0, 0, tile_or_none(FAR_GROUP * kg + t)[0]]
                               for t in range(FAR_GROUP)], axis=1)
        acc_scr[...] = jnp.exp2(m_old - m_new) * acc_scr[...] + _dot(v1t, p.astype(BF16))
        m_scr[0:1, :] = m_new

    def far_trip(base, groups):
        for t in range(groups):
            s_scr[(t + 1) % 2] = far_logits(base + t + 1)
            absorb(t % 2, base + t)

    s_scr[0] = far_logits(0)
    state = _online_absorb(s_own, slc1_ref[0, 0, i], None)
    m, acc = _online_absorb(s_prev, slc1_ref[0, 0, prev], state)
    m_scr[0:1, :] = m
    acc_scr[...] = acc
    n_long = n_groups // 2

    def long_body(k, carry):
        far_trip(2 * k, 2)
        return carry

    lax.fori_loop(0, n_long, long_body, 0)

    @pl.when(n_groups > 2 * n_long)
    def _():
        far_trip(2 * n_long, 1)

    o_slc_t = _normalise_t(acc_scr[...])

    merged = []
    for j in range(hpg):
        h = g * hpg + j
        gate = [gt_ref[0, 0, pl.ds(br * NSA_HEADS + h, 1), :] for br in range(3)]
        merged.append(gate[0] * ocmp_scr[:, bands[j]] + gate[1] * o_slc_t[:, bands[j]]
                      + gate[2] * o_win_t[:, bands[j]])
    o_ref[0] = jnp.concatenate(merged, axis=0).T.astype(BF16)


def _nsa_attention(q, kc, vct, slc, slc1, win, win1, block_mask, gates, cbt, tslc, twin):
    batch, _, seq, _ = slc.shape
    grp = NSA_KV_GROUPS
    tq = Q_TILE
    rows = NSA_HPG * tq
    per_group = lambda b, g, i: (b, g, 0, 0)
    per_group_t = lambda b, g, i: (b, g, 0, 0, 0)
    return pl.pallas_call(
        _nsa_kernel,
        out_shape=jax.ShapeDtypeStruct((batch, seq, _QW), BF16),
        grid=(batch, grp, seq // tq),
        in_specs=[
            pl.BlockSpec((1, 1, NSA_HPG * NSA_HEAD_DIM, tq), lambda b, g, i: (b, i, g, 0)),
            _resident((1, 1, N_CMP_PAD, _PAIR), per_group),
            _resident((1, 1, NSA_HEAD_DIM, N_CMP_PAD), per_group),
            _resident((1, 1, seq, _PAIR), per_group),
            _resident((1, 1, seq // tq, _VT_ROWS, tq), per_group_t),
            _resident((1, 1, seq, _PAIR), per_group),
            _resident((1, 1, seq // tq, _VT_ROWS, tq), per_group_t),
            _resident((seq + tq, N_SEL), lambda b, g, i: (0, 0)),
            pl.BlockSpec((1, 1, V7X_LANES, tq), lambda b, g, i: (b, i, 0, 0)),
            _resident((1, 2 * N_CMP_PAD, rows), lambda b, g, i: (g, 0, 0)),
            _resident((1, 2 * tq, rows), lambda b, g, i: (g, 0, 0)),
            _resident((1, WINDOW + tq, rows), lambda b, g, i: (g, 0, 0)),
        ],
        out_specs=pl.BlockSpec((1, tq, NSA_HPG * NSA_HEAD_DIM), lambda b, g, i: (b, i, g)),
        scratch_shapes=[pltpu.VMEM((tq // V7X_LANES, N_CMP_PAD + 16, V7X_LANES), F32),
                        pltpu.VMEM((2 * _PAIR, rows), BF16),
                        pltpu.VMEM((8, rows), F32),
                        pltpu.VMEM((_VT_ROWS, rows), F32),
                        pltpu.VMEM((2, FAR_GROUP * tq, rows), F32),
                        pltpu.VMEM((NSA_HEAD_DIM, rows), F32)],
        compiler_params=pltpu.CompilerParams(
            dimension_semantics=("arbitrary", "arbitrary", "arbitrary"),
            vmem_limit_bytes=_vmem_limit(52 * 1024 * 1024)),
        name="nsa_attention",
    )(q, kc, vct, slc, slc1, win, win1, block_mask, gates, cbt, tslc, twin)


def _nsa_weight(w):
    d = w.shape[0]
    q = w[:, :_QW]
    kv = w[:, _QW:_QW + 6 * _KVW].reshape(d, 6, NSA_KV_GROUPS, NSA_HEAD_DIM)
    cmp_raw = kv[:, 0:2].reshape(d, 2 * _KVW)
    slc = jnp.stack([kv[:, 2], kv[:, 3]], axis=2).reshape(d, NSA_KV_GROUPS * _PAIR)
    win = jnp.stack([kv[:, 4], kv[:, 5]], axis=2).reshape(d, NSA_KV_GROUPS * _PAIR)
    gates = w[:, _QW + 6 * _KVW:_QW + 6 * _KVW + 3 * NSA_HEADS]
    gates = jnp.pad(gates, ((0, 0), (0, V7X_LANES - 3 * NSA_HEADS)))
    xq = w[:, _QW + 6 * _KVW + 3 * NSA_HEADS:]
    return jnp.concatenate([q, cmp_raw, slc, win, gates, xq], axis=1).astype(BF16)


def kernel(x, mem, norm_mix_g, norm_mem_g, norm_ffn_g, norm_final_g, w_in_conv, conv_w, w_in_nsa,
           cmp_pos_k, cmp_w1_k, cmp_w2_k, cmp_pos_v, cmp_w1_v, cmp_w2_v, rel_bias, w_mem_kv, w_out,
           w_ffn_in, w_ffn_out):
    seq = x.shape[1]
    assert seq % ROW_TILE == 0 and seq % Q_TILE == 0 and x.shape[2] == D_MODEL
    assert seq == CMP_STRIDE * _CHUNKS and seq == SEL_BLOCK * N_SEL and Q_TILE == KV_TILE
    memkv = _memkv(mem, norm_mem_g, w_mem_kv.astype(BF16))
    w_out_b = w_out.astype(BF16)
    w_ffn_in_b = w_ffn_in.astype(BF16)
    w_ffn_out_b = w_ffn_out.astype(BF16)

    tok, xq = _conv_front(x, norm_mix_g[0], w_in_conv[0].astype(BF16), conv_w[0])
    h = _mixout(x, tok, xq, memkv, w_out_b, 0)
    h = _ffn(h, norm_ffn_g[0], w_ffn_in_b, w_ffn_out_b, norm_final_g, 0, final=False)

    q, cmp_raw, slc, slc1, win, win1, gates, xq = _nsa_proj(h, norm_mix_g[1], _nsa_weight(w_in_nsa[0]))
    pos4 = jnp.stack([jnp.tile(cmp_pos_k[0], (1, NSA_KV_GROUPS)),
                      jnp.tile(cmp_pos_v[0], (1, NSA_KV_GROUPS))])
    w1 = jnp.stack([cmp_w1_k[0], cmp_w1_v[0]]).astype(BF16)
    kc = _compress(cmp_raw, pos4, w1, cmp_w2_k[0].astype(BF16)[None], 0, transposed=False)
    vct = _compress(cmp_raw, pos4, w1, cmp_w2_v[0].T.astype(BF16)[None], 1, transposed=True)
    tok = _nsa_attention(q, kc, vct, slc, slc1, win, win1, _block_mask_rows(seq), gates,
                         *_bias_tables(rel_bias))
    h = _mixout(h, tok, xq, memkv, w_out_b, 1)
    return _ffn(h, norm_ffn_g[1], w_ffn_in_b, w_ffn_out_b, norm_final_g, 1, final=True)
```

```python
import functools

import numpy as np
import jax
import jax.numpy as jnp
from jax import lax
from jax.experimental import pallas as pl
from jax.experimental.pallas import tpu as pltpu

F32 = jnp.float32
BF16 = jnp.bfloat16

D_MODEL = 1024
N_MEM = 256
CONV_WIDTH = 3
NSA_HEADS = 16
NSA_HEAD_DIM = 64
NSA_KV_GROUPS = 4
NSA_HPG = NSA_HEADS // NSA_KV_GROUPS
CMP_BLOCK = 32
CMP_STRIDE = 16
CMP_HIDDEN = 2 * NSA_HEAD_DIM
SEL_BLOCK = 64
SEL_TOPK = 16
SEL_FORCE = 1.0e4
WINDOW = 512
XATTN_HEADS = 4
XATTN_HEAD_DIM = 128
XATTN_WIDTH = XATTN_HEADS * XATTN_HEAD_DIM
FFN_HIDDEN = 2816
REL_BUCKETS = 32
REL_MAX_DIST = 128
RMS_EPS = 1e-6
NEG = -1e30
LOG2E = 1.4426950408889634

V7X_LANES = 128
V7X_VMEM_BYTES = 64 * 1024 * 1024

ROW_TILE = 512
CONV_COLS = 256
FFN_ROW_TILE = 512
FFN_CHUNK = 256
Q_TILE = 256
KV_TILE = 256
FAR_GROUP = 2
CMP_VARIANTS = 4
N_CMP_PAD = 512
N_SEL = 128

_NT = (((1,), (1,)), ((), ()))


def _vmem_limit(nbytes):
    return int(min(max(nbytes, 16 * 1024 * 1024), V7X_VMEM_BYTES - 8 * 1024 * 1024))


def _rms(x, g):
    ms = jnp.mean(x * x, axis=-1, keepdims=True)
    return x * lax.rsqrt(ms + RMS_EPS) * g


def _dot(a, b):
    return jnp.dot(a, b, preferred_element_type=F32)


def _dot_nt(a, b):
    return lax.dot_general(a, b, _NT, preferred_element_type=F32)


def _sigmoid(x):
    return 1.0 / (1.0 + jnp.exp(-x))


def _resident(block_shape, index_map):
    return pl.BlockSpec(block_shape, index_map, pipeline_mode=pl.Buffered(1))


def _memkv_kernel(mem_ref, g_ref, w_ref, o_ref):
    xn = _rms(mem_ref[0], g_ref[0]).astype(BF16)
    o_ref[0, 0] = _dot(xn, w_ref[0]).astype(BF16)


def _memkv(mem, gains, w):
    depth, batch = w.shape[0], mem.shape[0]
    return pl.pallas_call(
        _memkv_kernel,
        out_shape=jax.ShapeDtypeStruct((depth, batch, N_MEM, 2 * XATTN_WIDTH), BF16),
        grid=(depth, batch),
        in_specs=[
            pl.BlockSpec((1, N_MEM, D_MODEL), lambda l, b: (b, 0, 0)),
            pl.BlockSpec((1, 1, D_MODEL), lambda l, b: (l, 0, 0)),
            pl.BlockSpec((1, D_MODEL, 2 * XATTN_WIDTH), lambda l, b: (l, 0, 0)),
        ],
        out_specs=pl.BlockSpec((1, 1, N_MEM, 2 * XATTN_WIDTH), lambda l, b: (l, b, 0, 0)),
        compiler_params=pltpu.CompilerParams(
            dimension_semantics=("arbitrary", "arbitrary"),
            vmem_limit_bytes=_vmem_limit(24 * 1024 * 1024)),
        name="memkv",
    )(mem, gains.reshape(depth, 1, D_MODEL), w)


def _conv_kernel(h_ref, hp_ref, g_ref, w_ref, cw_ref, tok_ref, xq_ref):
    d = D_MODEL
    i = pl.program_id(1)
    g = g_ref[...]
    hn = _rms(h_ref[0], g).astype(BF16)
    hpn = _rms(hp_ref[0], g).astype(BF16)
    cw = cw_ref[...]
    row = lax.broadcasted_iota(jnp.int32, (h_ref.shape[1], CONV_COLS), 0)

    def projections(c):
        lo = c * CONV_COLS
        col = lambda part: w_ref[:, part * d + lo:part * d + lo + CONV_COLS]
        vp = jnp.where(i > 0, _dot(hpn, col(1)) * _dot(hpn, col(2)), 0.0)
        return _dot(hn, col(0)), _dot(hn, col(1)) * _dot(hn, col(2)), vp

    n_chunks = d // CONV_COLS
    ready = projections(0)
    for c in range(n_chunks):
        u_b, v, vp = ready
        if c + 1 < n_chunks:
            ready = projections(c + 1)
        else:
            xq_ref[0] = _dot(hn, w_ref[:, 3 * d:]).astype(BF16)
        cols = slice(c * CONV_COLS, (c + 1) * CONV_COLS)
        v1 = jnp.where(row == 0, vp[7:8], pltpu.roll(v, 1, axis=0))
        v2 = jnp.where(row == 0, vp[6:7], jnp.where(row == 1, vp[7:8], pltpu.roll(v, 2, axis=0)))
        conv = v2 * cw[0:1, cols] + v1 * cw[1:2, cols] + v * cw[2:3, cols]
        tok_ref[0, :, cols] = (u_b * conv).astype(BF16)


def _conv_front(h, g, w, cw):
    batch, seq, d = h.shape
    tm = ROW_TILE
    width = w.shape[1]
    halo = 8
    return pl.pallas_call(
        _conv_kernel,
        out_shape=(jax.ShapeDtypeStruct((batch, seq, d), BF16),
                   jax.ShapeDtypeStruct((batch, seq, XATTN_WIDTH), BF16)),
        grid=(batch, seq // tm),
        in_specs=[
            pl.BlockSpec((1, tm, d), lambda b, i: (b, i, 0)),
            pl.BlockSpec((1, halo, d), lambda b, i: (b, jnp.maximum(i * (tm // halo) - 1, 0), 0)),
            pl.BlockSpec((1, d), lambda b, i: (0, 0)),
            pl.BlockSpec((d, width), lambda b, i: (0, 0)),
            pl.BlockSpec((CONV_WIDTH, d), lambda b, i: (0, 0)),
        ],
        out_specs=(pl.BlockSpec((1, tm, d), lambda b, i: (b, i, 0)),
                   pl.BlockSpec((1, tm, XATTN_WIDTH), lambda b, i: (b, i, 0))),
        compiler_params=pltpu.CompilerParams(
            dimension_semantics=("arbitrary", "arbitrary"),
            vmem_limit_bytes=_vmem_limit(48 * 1024 * 1024)),
        name="conv_front",
    )(h, h, g.reshape(1, d), w, cw)


def _mixout_kernel(h_ref, tok_ref, xq_ref, kv_ref, w_ref, o_ref):
    d = D_MODEL
    xq = xq_ref[0]
    kv = kv_ref[0, 0]
    scale = XATTN_HEAD_DIM ** -0.5 * LOG2E
    heads = [slice(hh * XATTN_HEAD_DIM, (hh + 1) * XATTN_HEAD_DIM) for hh in range(XATTN_HEADS)]
    logits = [_dot_nt(xq[:, hd], kv[:, hd]) for hd in heads]
    acc = h_ref[0] + _dot(tok_ref[0], w_ref[0, 0:d])
    outs = []
    for hd, s in zip(heads, logits):
        s = s * scale
        e = jnp.exp2(s - jnp.max(s, axis=-1, keepdims=True))
        p = e / jnp.sum(e, axis=-1, keepdims=True)
        outs.append(_dot(p.astype(BF16), kv[:, XATTN_WIDTH + hd.start:XATTN_WIDTH + hd.stop]))
    mo = jnp.concatenate(outs, axis=-1).astype(BF16)
    o_ref[0] = acc + _dot(mo, w_ref[0, d:])


def _mixout(h, tok, xq, kv, w, layer):
    batch, seq, d = h.shape
    tm = ROW_TILE
    return pl.pallas_call(
        _mixout_kernel,
        out_shape=jax.ShapeDtypeStruct((batch, seq, d), F32),
        grid=(batch, seq // tm),
        in_specs=[
            pl.BlockSpec((1, tm, d), lambda b, i: (b, i, 0)),
            pl.BlockSpec((1, tm, d), lambda b, i: (b, i, 0)),
            pl.BlockSpec((1, tm, XATTN_WIDTH), lambda b, i: (b, i, 0)),
            pl.BlockSpec((1, 1, N_MEM, 2 * XATTN_WIDTH), lambda b, i: (layer, b, 0, 0)),
            pl.BlockSpec((1, d + XATTN_WIDTH, d), lambda b, i: (layer, 0, 0)),
        ],
        out_specs=pl.BlockSpec((1, tm, d), lambda b, i: (b, i, 0)),
        compiler_params=pltpu.CompilerParams(
            dimension_semantics=("arbitrary", "arbitrary"),
            vmem_limit_bytes=_vmem_limit(40 * 1024 * 1024)),
        name="mixout",
    )(h, tok, xq, kv, w)


def _ffn_kernel(h_ref, g_ref, w1_ref, w2_ref, gf_ref, o_ref, *, final):
    h = h_ref[0]
    hn = _rms(h, g_ref[...]).astype(BF16)
    acc = h
    chunks = [c * FFN_CHUNK for c in range(FFN_HIDDEN // FFN_CHUNK)]
    gate_up = [(_dot(hn, w1_ref[0, :, lo:lo + FFN_CHUNK]),
                _dot(hn, w1_ref[0, :, FFN_HIDDEN + lo:FFN_HIDDEN + lo + FFN_CHUNK]))
               for lo in chunks[:1]]
    for c, lo in enumerate(chunks):
        if c + 1 < len(chunks):
            nxt = chunks[c + 1]
            gate_up.append((_dot(hn, w1_ref[0, :, nxt:nxt + FFN_CHUNK]),
                            _dot(hn, w1_ref[0, :, FFN_HIDDEN + nxt:FFN_HIDDEN + nxt + FFN_CHUNK])))
        a, u = gate_up[c]
        act = (a * _sigmoid(a) * u).astype(BF16)
        acc = acc + _dot(act, w2_ref[0, lo:lo + FFN_CHUNK])
    if final:
        acc = _rms(acc, gf_ref[...])
    o_ref[0] = acc


def _ffn(h, g, w1, w2, gf, layer, final):
    batch, seq, d = h.shape
    tm = FFN_ROW_TILE
    return pl.pallas_call(
        functools.partial(_ffn_kernel, final=final),
        out_shape=jax.ShapeDtypeStruct((batch, seq, d), F32),
        grid=(batch, seq // tm),
        in_specs=[
            pl.BlockSpec((1, tm, d), lambda b, i: (b, i, 0)),
            pl.BlockSpec((1, d), lambda b, i: (0, 0)),
            pl.BlockSpec((1, d, 2 * FFN_HIDDEN), lambda b, i: (layer, 0, 0)),
            pl.BlockSpec((1, FFN_HIDDEN, d), lambda b, i: (layer, 0, 0)),
            pl.BlockSpec((1, d), lambda b, i: (0, 0)),
        ],
        out_specs=pl.BlockSpec((1, tm, d), lambda b, i: (b, i, 0)),
        compiler_params=pltpu.CompilerParams(
            dimension_semantics=("arbitrary", "arbitrary"),
            vmem_limit_bytes=_vmem_limit(52 * 1024 * 1024)),
        name="ffn",
    )(h, g.reshape(1, d), w1, w2, gf.reshape(1, d))


_QW = NSA_HEADS * NSA_HEAD_DIM
_KVW = NSA_KV_GROUPS * NSA_HEAD_DIM
_PAIR = 2 * NSA_HEAD_DIM
_C_Q = 0
_C_CMP = _C_Q + _QW
_C_SLC = _C_CMP + 2 * _KVW
_C_WIN = _C_SLC + NSA_KV_GROUPS * _PAIR
_C_GATE = _C_WIN + NSA_KV_GROUPS * _PAIR
_C_XQ = _C_GATE + V7X_LANES
_C_END = _C_XQ + XATTN_WIDTH


_SUM_ROWS = 16
_VT_ROWS = _SUM_ROWS + NSA_HEAD_DIM


def _ones_and_vt(kv):
    v_t = kv.T[NSA_HEAD_DIM:]
    return jnp.concatenate([jnp.ones((_SUM_ROWS, kv.shape[0]), F32), v_t], axis=0).astype(BF16)


def _nsa_proj_kernel(h_ref, g_ref, w_ref, q_ref, cr_ref, slc_ref, slc1_ref, win_ref, win1_ref,
                     gt_ref, xq_ref):
    hn = _rms(h_ref[0], g_ref[...]).astype(BF16)
    q = _dot(hn, w_ref[:, _C_Q:_C_CMP]) * (NSA_HEAD_DIM ** -0.5 * LOG2E)
    gates = _sigmoid(_dot(hn, w_ref[:, _C_GATE:_C_XQ]))
    for c in range(ROW_TILE // Q_TILE):
        rows = slice(c * Q_TILE, (c + 1) * Q_TILE)
        q_ref[0, c] = q[rows].T.astype(BF16)
        gt_ref[0, c] = gates[rows].T
    cr = _dot(hn, w_ref[:, _C_CMP:_C_SLC])
    for p in range(2 * _KVW // V7X_LANES):
        cr_ref[0, p] = cr[:, p * V7X_LANES:(p + 1) * V7X_LANES]
    s = _dot(hn, w_ref[:, _C_SLC:_C_WIN])
    w = _dot(hn, w_ref[:, _C_WIN:_C_GATE])
    k_lanes = lax.broadcasted_iota(jnp.int32, (s.shape[0], _PAIR), 1) < NSA_HEAD_DIM
    for g in range(NSA_KV_GROUPS):
        sg = s[:, g * _PAIR:(g + 1) * _PAIR]
        wg = w[:, g * _PAIR:(g + 1) * _PAIR]
        slc_ref[0, g] = jnp.where(k_lanes, sg, 0.0).astype(BF16)
        win_ref[0, g] = jnp.where(k_lanes, wg, 0.0).astype(BF16)
        for c in range(ROW_TILE // KV_TILE):
            rows = slice(c * KV_TILE, (c + 1) * KV_TILE)
            slc1_ref[0, g, c] = _ones_and_vt(sg[rows])
            win1_ref[0, g, c] = _ones_and_vt(wg[rows])
    xq_ref[0] = _dot(hn, w_ref[:, _C_XQ:_C_END]).astype(BF16)


def _nsa_proj(h, g, w):
    batch, seq, d = h.shape
    tm = ROW_TILE
    grp = NSA_KV_GROUPS
    row = lambda b, i: (b, i, 0)
    pair_shape = jax.ShapeDtypeStruct((batch, grp, seq, _PAIR), BF16)
    pair_spec = pl.BlockSpec((1, grp, tm, _PAIR), lambda b, i: (b, 0, i, 0))
    vt_shape = jax.ShapeDtypeStruct((batch, grp, seq // KV_TILE, _VT_ROWS, KV_TILE), BF16)
    vt_spec = pl.BlockSpec((1, grp, tm // KV_TILE, _VT_ROWS, KV_TILE), lambda b, i: (b, 0, i, 0, 0))
    return pl.pallas_call(
        _nsa_proj_kernel,
        out_shape=(jax.ShapeDtypeStruct((batch, seq // Q_TILE, _QW, Q_TILE), BF16),
                   jax.ShapeDtypeStruct((batch, 2 * _KVW // V7X_LANES, seq, V7X_LANES), F32),
                   pair_shape, vt_shape, pair_shape, vt_shape,
                   jax.ShapeDtypeStruct((batch, seq // Q_TILE, V7X_LANES, Q_TILE), F32),
                   jax.ShapeDtypeStruct((batch, seq, XATTN_WIDTH), BF16)),
        grid=(batch, seq // tm),
        in_specs=[
            pl.BlockSpec((1, tm, d), row),
            pl.BlockSpec((1, d), lambda b, i: (0, 0)),
            pl.BlockSpec((d, _C_END), lambda b, i: (0, 0)),
        ],
        out_specs=(pl.BlockSpec((1, tm // Q_TILE, _QW, Q_TILE), lambda b, i: (b, i, 0, 0)),
                   pl.BlockSpec((1, 2 * _KVW // V7X_LANES, tm, V7X_LANES), lambda b, i: (b, 0, i, 0)),
                   pair_spec, vt_spec, pair_spec, vt_spec,
                   pl.BlockSpec((1, tm // Q_TILE, V7X_LANES, Q_TILE), lambda b, i: (b, i, 0, 0)),
                   pl.BlockSpec((1, tm, XATTN_WIDTH), row)),
        compiler_params=pltpu.CompilerParams(
            dimension_semantics=("arbitrary", "arbitrary"),
            vmem_limit_bytes=_vmem_limit(48 * 1024 * 1024)),
        name="nsa_proj",
    )(h, g.reshape(1, d), w)


_CHUNKS = 512


def _compress_kernel(raw_ref, pos_ref, w1_ref, w2_ref, o_ref, h2_scr, *, transposed):
    dk = NSA_HEAD_DIM
    half = CMP_BLOCK // 2
    h1 = [jnp.zeros((_CHUNKS, CMP_HIDDEN), F32) for _ in range(NSA_KV_GROUPS)]
    h2 = [jnp.zeros((_CHUNKS, CMP_HIDDEN), F32) for _ in range(NSA_KV_GROUPS)]
    gpl = V7X_LANES // dk
    for l in range(half):
        for p in range(NSA_KV_GROUPS // gpl):
            lanes = slice(p * V7X_LANES, (p + 1) * V7X_LANES)
            rows = raw_ref[0, p, pl.ds(l, _CHUNKS, stride=CMP_STRIDE), :]
            a1 = (rows + pos_ref[0, l:l + 1, lanes]).astype(BF16)
            a2 = (rows + pos_ref[0, half + l:half + l + 1, lanes]).astype(BF16)
            for gg in range(gpl):
                g = p * gpl + gg
                h1[g] = h1[g] + _dot(a1[:, gg * dk:(gg + 1) * dk], w1_ref[0, l * dk:(l + 1) * dk, :])
                h2[g] = h2[g] + _dot(a2[:, gg * dk:(gg + 1) * dk],
                                     w1_ref[0, (half + l) * dk:(half + l + 1) * dk, :])
    h2_scr[_CHUNKS:_CHUNKS + 8, :] = jnp.zeros((8, CMP_HIDDEN), F32)
    for g in range(NSA_KV_GROUPS):
        h2_scr[0:_CHUNKS, :] = h2[g]
        hm = h1[g] + h2_scr[pl.ds(1, _CHUNKS), :]
        hm = (hm * _sigmoid(hm)).astype(BF16)
        if transposed:
            o_ref[0, g] = _dot_nt(w2_ref[0], hm).astype(BF16)
        else:
            out = _dot(hm, w2_ref[0])
            o_ref[0, g] = jnp.concatenate([out, jnp.zeros_like(out)], axis=1).astype(BF16)


def _compress(raw, pos4, w1, w2, which, transposed):
    batch, _, seq, _ = raw.shape
    grp = NSA_KV_GROUPS
    tiles = _KVW // V7X_LANES
    if transposed:
        out_shape = (batch, grp, NSA_HEAD_DIM, N_CMP_PAD)
        w2_block = (1, NSA_HEAD_DIM, CMP_HIDDEN)
    else:
        out_shape = (batch, grp, N_CMP_PAD, _PAIR)
        w2_block = (1, CMP_HIDDEN, NSA_HEAD_DIM)
    return pl.pallas_call(
        functools.partial(_compress_kernel, transposed=transposed),
        out_shape=jax.ShapeDtypeStruct(out_shape, BF16),
        grid=(batch,),
        in_specs=[
            pl.BlockSpec((1, tiles, seq, V7X_LANES), lambda b: (b, which, 0, 0)),
            pl.BlockSpec((1, CMP_BLOCK, _KVW), lambda b: (which, 0, 0)),
            pl.BlockSpec((1, CMP_BLOCK * NSA_HEAD_DIM, CMP_HIDDEN), lambda b: (which, 0, 0)),
            pl.BlockSpec(w2_block, lambda b: (0, 0, 0)),
        ],
        out_specs=pl.BlockSpec((1,) + out_shape[1:], lambda b: (b, 0, 0, 0)),
        scratch_shapes=[pltpu.VMEM((_CHUNKS + 8, CMP_HIDDEN), F32)],
        compiler_params=pltpu.CompilerParams(
            dimension_semantics=("arbitrary",),
            vmem_limit_bytes=_vmem_limit(40 * 1024 * 1024)),
        name="compress_v" if transposed else "compress_k",
    )(raw, pos4, w1, w2)


def _t5_bucket(dist):
    n = jnp.maximum(dist, 0)
    max_exact = REL_BUCKETS // 2
    nf = jnp.maximum(n, 1).astype(F32)
    large = max_exact + (jnp.log(nf / max_exact) / np.log(REL_MAX_DIST / max_exact)
                         * (REL_BUCKETS - max_exact)).astype(jnp.int32)
    large = jnp.minimum(large, REL_BUCKETS - 1)
    return jnp.where(n < max_exact, n, large)


def _table_kernel(thr_ref, rb_ref, cbt_ref, tslc_ref, twin_ref):
    tq = Q_TILE
    g = pl.program_id(0)

    def entries(dist, h, limit):
        val = jnp.full(dist.shape, rb_ref[0, h], F32)
        for k in range(1, REL_BUCKETS):
            val = jnp.where(dist >= thr_ref[k], rb_ref[k, h], val)
        val = (val - rb_ref[REL_BUCKETS - 1, h]) * LOG2E
        return jnp.where(jnp.where(dist >= 0, dist, limit) < limit, val, NEG)

    def far_entries(dist, limit):
        return jnp.where(jnp.where(dist >= 0, dist, limit) < limit, 0.0, NEG)

    def fill(ref, band, dist_of_rows, n_rows, limit, h):
        step = 64
        for r0 in range(0, n_rows, step):
            shape = (step, tq)
            dist = dist_of_rows(r0 + lax.broadcasted_iota(jnp.int32, shape, 0),
                                lax.broadcasted_iota(jnp.int32, shape, 1))
            lo, hi = dist_of_rows(r0 + step - 1, 0), dist_of_rows(r0, tq - 1)
            near = hi >= 0 and lo < REL_MAX_DIST
            ref[0, r0:r0 + step, band] = entries(dist, h, limit) if near else far_entries(dist, limit)

    no_limit = 1 << 30
    for j in range(NSA_HPG):
        h = g * NSA_HPG + j
        band = slice(j * tq, (j + 1) * tq)
        fill(cbt_ref, band, lambda u, q: q - (CMP_BLOCK - 1) - CMP_STRIDE * (u - N_CMP_PAD),
             2 * N_CMP_PAD, no_limit, h)
        fill(tslc_ref, band, lambda u, q: q - u + tq, 2 * tq, no_limit, h)
        fill(twin_ref, band, lambda u, q: q - u + WINDOW, WINDOW + tq, WINDOW, h)


def _bias_tables(rel_bias):
    tq = Q_TILE
    rows = NSA_HPG * tq
    buckets = _t5_bucket(jnp.arange(REL_MAX_DIST + 1))
    thr = jnp.sum(buckets[None, :] < jnp.arange(REL_BUCKETS)[:, None], axis=1).astype(jnp.int32)
    smem = pl.BlockSpec(memory_space=pltpu.SMEM)
    return pl.pallas_call(
        _table_kernel,
        out_shape=(jax.ShapeDtypeStruct((NSA_KV_GROUPS, 2 * N_CMP_PAD, rows), F32),
                   jax.ShapeDtypeStruct((NSA_KV_GROUPS, 2 * tq, rows), F32),
                   jax.ShapeDtypeStruct((NSA_KV_GROUPS, WINDOW + tq, rows), F32)),
        grid=(NSA_KV_GROUPS,),
        in_specs=[smem, smem],
        out_specs=(pl.BlockSpec((1, 2 * N_CMP_PAD, rows), lambda g: (g, 0, 0)),
                   pl.BlockSpec((1, 2 * tq, rows), lambda g: (g, 0, 0)),
                   pl.BlockSpec((1, WINDOW + tq, rows), lambda g: (g, 0, 0))),
        compiler_params=pltpu.CompilerParams(
            dimension_semantics=("arbitrary",),
            vmem_limit_bytes=_vmem_limit(40 * 1024 * 1024)),
        name="bias_tables",
    )(thr, rel_bias.astype(F32))


def _block_mask_rows(seq):
    pos = jnp.arange(seq, dtype=jnp.int32)[:, None]
    col = jnp.arange(N_SEL, dtype=jnp.int32)[None, :]
    onehot = jnp.where(col == pos // SEL_BLOCK, NEG, 0.0)
    return jnp.concatenate([onehot, jnp.full((Q_TILE, N_SEL), NEG)], axis=0).astype(BF16)


def _online_absorb(s, v1t, state):
    m = jnp.max(s, axis=0, keepdims=True)
    if state is not None:
        m = jnp.maximum(state[0], m)
    acc = _dot(v1t, jnp.exp2(s - m).astype(BF16))
    if state is not None:
        acc = jnp.exp2(state[0] - m) * state[1] + acc
    return m, acc


def _normalise_t(acc):
    return acc[_SUM_ROWS:] * (1.0 / acc[0:1])


def _nsa_kernel(q_ref, kc_ref, vct_ref, slc_ref, slc1_ref, win_ref, win1_ref, bm_ref, gt_ref,
                cbt_ref, tslc_ref, twin_ref, o_ref, ps_scr, lhs_scr, m_scr, acc_scr, s_scr, ocmp_scr):
    tq = Q_TILE
    hpg = NSA_HPG
    dk = NSA_HEAD_DIM
    seq = slc_ref.shape[2]
    g = pl.program_id(1)
    i = pl.program_id(2)
    q0 = pl.multiple_of(i * tq, tq)
    bands = [slice(j * tq, (j + 1) * tq) for j in range(hpg)]

    for j in range(hpg):
        lhs_scr[0:dk, bands[j]] = q_ref[0, 0, j * dk:(j + 1) * dk, :]
    lhs_scr[dk:dk + _SUM_ROWS, :] = jnp.ones((_SUM_ROWS, hpg * tq), BF16)
    lhs_scr[dk + _SUM_ROWS:_PAIR, :] = jnp.zeros((_PAIR - dk - _SUM_ROWS, hpg * tq), BF16)
    qs_t = lhs_scr[0:_PAIR, :]

    n_prev = WINDOW // tq
    mask_lane = lax.broadcasted_iota(jnp.int32, (1, _PAIR), 1) == dk

    def win_logits(r):
        tile = jnp.maximum(i - r, 0)
        keys = win_ref[0, 0, pl.ds(pl.multiple_of(tile * tq, tq), tq), :]
        if r:
            keys = keys + jnp.where(jnp.logical_and(mask_lane, i < r), NEG, 0.0).astype(BF16)
        rows = slice((n_prev - r) * tq, (n_prev - r + 1) * tq)
        return _dot(keys, qs_t) + twin_ref[0, rows, :]

    state = None
    s_cur = win_logits(0)
    for r in range(0, n_prev + 1):
        s_next = win_logits(r + 1) if r < n_prev else None
        state = _online_absorb(s_cur, win1_ref[0, 0, jnp.maximum(i - r, 0)], state)
        s_cur = s_next
    o_win_t = _normalise_t(state[1])

    def compress_and_select(n_cmp):
        n_sel = n_cmp * CMP_STRIDE // SEL_BLOCK
        st = _dot(kc_ref[0, 0, 0:n_cmp, :], qs_t)
        st = st + cbt_ref[0, pl.ds(pl.multiple_of(N_CMP_PAD - (tq // CMP_STRIDE) * i, 8), n_cmp), :]
        mx = jnp.maximum(jnp.max(st, axis=0, keepdims=True), 0.1 * NEG)
        e = jnp.exp2(st - mx)
        den = jnp.sum(e, axis=0, keepdims=True)
        inv = jnp.where(den > 0.0, 1.0 / den, 0.0)
        ocmp_scr[...] = _dot(vct_ref[0, 0, :, 0:n_cmp], e.astype(BF16)) * inv
        psum = e[:, bands[0]] * inv[:, bands[0]]
        for j in range(1, hpg):
            psum = psum + e[:, bands[j]] * inv[:, bands[j]]

        ratio = SEL_BLOCK // CMP_STRIDE
        imp_tiles = []
        for c in range(tq // V7X_LANES):
            ps_scr[c, 0:8, :] = jnp.zeros((8, V7X_LANES), F32)
            ps_scr[c, 8:8 + n_cmp, :] = psum[:, c * V7X_LANES:(c + 1) * V7X_LANES]
            taps = [ps_scr[c, pl.ds(7 + r, n_sel, stride=ratio), :] for r in range(5)]
            imp_tiles.append(0.5 * taps[0] + taps[1] + taps[2] + taps[3] + 0.5 * taps[4])
        imp = jnp.concatenate(imp_tiles, axis=1)

        jrow = lax.broadcasted_iota(jnp.int32, (n_sel, tq), 0)
        cur = (q0 + lax.broadcasted_iota(jnp.int32, (n_sel, tq), 1)) // SEL_BLOCK
        imp = jnp.where(jrow <= cur, imp, -SEL_FORCE)
        for forced_block in (0, cur, cur - 1):
            imp = jnp.where(jrow == forced_block, SEL_FORCE, imp)
        taken = -3.0e38
        for _ in range(SEL_TOPK):
            top = jnp.max(imp, axis=0, keepdims=True)
            first = jnp.min(jnp.where(imp == top, jrow, n_sel), axis=0, keepdims=True)
            imp = jnp.where(jrow == first, taken, imp)
        not_sel = jnp.where(imp == taken, 0.0, 1.0).astype(BF16)
        for j in range(hpg):
            lhs_scr[_PAIR:_PAIR + n_sel, bands[j]] = not_sel
        if n_sel < N_SEL:
            lhs_scr[_PAIR + n_sel:2 * _PAIR, :] = jnp.ones((N_SEL - n_sel, hpg * tq), BF16)

    tiles_per_variant = (seq // tq) // CMP_VARIANTS
    for v in range(CMP_VARIANTS):
        pl.when(i // tiles_per_variant == v)(
            functools.partial(compress_and_select, N_CMP_PAD * (v + 1) // CMP_VARIANTS))
    lhs = lhs_scr[...]

    def key_tile(kv_ref, tile, mask_tile):
        return jnp.concatenate([kv_ref[0, 0, pl.ds(pl.multiple_of(tile * tq, tq), tq), :],
                                bm_ref[pl.ds(pl.multiple_of(mask_tile * tq, tq), tq), :]], axis=1)

    all_masked = seq // tq

    prev = jnp.maximum(i - 1, 0)
    s_own = _dot(key_tile(slc_ref, i, i), lhs) + tslc_ref[0, tq:2 * tq, :]
    s_prev = (_dot(key_tile(slc_ref, prev, jnp.where(i == 0, all_masked, i - 1)), lhs)
              + tslc_ref[0, 0:tq, :])

    n_far = jnp.maximum(i - 1, 0)
    n_groups = (n_far + FAR_GROUP - 1) // FAR_GROUP

    def tile_or_none(kt):
        valid = kt < n_far
        return jnp.where(valid, kt, 0), jnp.where(valid, kt, all_masked)

    def far_logits(kg):
        tiles = [key_tile(slc_ref, *tile_or_none(FAR_GROUP * kg + t)) for t in range(FAR_GROUP)]
        return _dot(jnp.concatenate(tiles, axis=0), lhs_scr[...])

    def absorb(slot, kg):
        s = s_scr[slot]
        m_old = m_scr[0:1, :]
        m_new = jnp.maximum(m_old, jnp.max(s, axis=0, keepdims=True))
        p = jnp.exp2(s - m_new)
        v1t = jnp.concatenate([slc1_ref[0, 0, tile_or_none(FAR_GROUP * kg + t)[0]]
                               for t in range(FAR_GROUP)], axis=1)
        acc_scr[...] = jnp.exp2(m_old - m_new) * acc_scr[...] + _dot(v1t, p.astype(BF16))
        m_scr[0:1, :] = m_new

    def far_trip(base, groups):
        for t in range(groups):
            s_scr[(t + 1) % 2] = far_logits(base + t + 1)
            absorb(t % 2, base + t)

    s_scr[0] = far_logits(0)
    state = _online_absorb(s_own, slc1_ref[0, 0, i], None)
    m, acc = _online_absorb(s_prev, slc1_ref[0, 0, prev], state)
    m_scr[0:1, :] = m
    acc_scr[...] = acc
    n_long = n_groups // 4

    def long_body(k, carry):
        far_trip(4 * k, 4)
        return carry

    lax.fori_loop(0, n_long, long_body, 0)
    done = 4 * n_long
    pair = (n_groups - done) // 2

    @pl.when(pair > 0)
    def _():
        far_trip(done, 2)

    @pl.when(n_groups - done - 2 * pair > 0)
    def _():
        far_trip(done + 2 * pair, 1)

    o_slc_t = _normalise_t(acc_scr[...])

    merged = []
    for j in range(hpg):
        h = g * hpg + j
        gate = [gt_ref[0, 0, pl.ds(br * NSA_HEADS + h, 1), :] for br in range(3)]
        merged.append(gate[0] * ocmp_scr[:, bands[j]] + gate[1] * o_slc_t[:, bands[j]]
                      + gate[2] * o_win_t[:, bands[j]])
    o_ref[0] = jnp.concatenate(merged, axis=0).T.astype(BF16)


def _nsa_attention(q, kc, vct, slc, slc1, win, win1, block_mask, gates, cbt, tslc, twin):
    batch, _, seq, _ = slc.shape
    grp = NSA_KV_GROUPS
    tq = Q_TILE
    rows = NSA_HPG * tq
    per_group = lambda b, g, i: (b, g, 0, 0)
    per_group_t = lambda b, g, i: (b, g, 0, 0, 0)
    return pl.pallas_call(
        _nsa_kernel,
        out_shape=jax.ShapeDtypeStruct((batch, seq, _QW), BF16),
        grid=(batch, grp, seq // tq),
        in_specs=[
            pl.BlockSpec((1, 1, NSA_HPG * NSA_HEAD_DIM, tq), lambda b, g, i: (b, i, g, 0)),
            _resident((1, 1, N_CMP_PAD, _PAIR), per_group),
            _resident((1, 1, NSA_HEAD_DIM, N_CMP_PAD), per_group),
            _resident((1, 1, seq, _PAIR), per_group),
            _resident((1, 1, seq // tq, _VT_ROWS, tq), per_group_t),
            _resident((1, 1, seq, _PAIR), per_group),
            _resident((1, 1, seq // tq, _VT_ROWS, tq), per_group_t),
            _resident((seq + tq, N_SEL), lambda b, g, i: (0, 0)),
            pl.BlockSpec((1, 1, V7X_LANES, tq), lambda b, g, i: (b, i, 0, 0)),
            _resident((1, 2 * N_CMP_PAD, rows), lambda b, g, i: (g, 0, 0)),
            _resident((1, 2 * tq, rows), lambda b, g, i: (g, 0, 0)),
            _resident((1, WINDOW + tq, rows), lambda b, g, i: (g, 0, 0)),
        ],
        out_specs=pl.BlockSpec((1, tq, NSA_HPG * NSA_HEAD_DIM), lambda b, g, i: (b, i, g)),
        scratch_shapes=[pltpu.VMEM((tq // V7X_LANES, N_CMP_PAD + 16, V7X_LANES), F32),
                        pltpu.VMEM((2 * _PAIR, rows), BF16),
                        pltpu.VMEM((8, rows), F32),
                        pltpu.VMEM((_VT_ROWS, rows), F32),
                        pltpu.VMEM((2, FAR_GROUP * tq, rows), F32),
                        pltpu.VMEM((NSA_HEAD_DIM, rows), F32)],
        compiler_params=pltpu.CompilerParams(
            dimension_semantics=("arbitrary", "arbitrary", "arbitrary"),
            vmem_limit_bytes=_vmem_limit(52 * 1024 * 1024)),
        name="nsa_attention",
    )(q, kc, vct, slc, slc1, win, win1, block_mask, gates, cbt, tslc, twin)


def _nsa_weight(w):
    d = w.shape[0]
    q = w[:, :_QW]
    kv = w[:, _QW:_QW + 6 * _KVW].reshape(d, 6, NSA_KV_GROUPS, NSA_HEAD_DIM)
    cmp_raw = kv[:, 0:2].reshape(d, 2 * _KVW)
    slc = jnp.stack([kv[:, 2], kv[:, 3]], axis=2).reshape(d, NSA_KV_GROUPS * _PAIR)
    win = jnp.stack([kv[:, 4], kv[:, 5]], axis=2).reshape(d, NSA_KV_GROUPS * _PAIR)
    gates = w[:, _QW + 6 * _KVW:_QW + 6 * _KVW + 3 * NSA_HEADS]
    gates = jnp.pad(gates, ((0, 0), (0, V7X_LANES - 3 * NSA_HEADS)))
    xq = w[:, _QW + 6 * _KVW + 3 * NSA_HEADS:]
    return jnp.concatenate([q, cmp_raw, slc, win, gates, xq], axis=1).astype(BF16)


def kernel(x, mem, norm_mix_g, norm_mem_g, norm_ffn_g, norm_final_g, w_in_conv, conv_w, w_in_nsa,
           cmp_pos_k, cmp_w1_k, cmp_w2_k, cmp_pos_v, cmp_w1_v, cmp_w2_v, rel_bias, w_mem_kv, w_out,
           w_ffn_in, w_ffn_out):
    seq = x.shape[1]
    assert seq % ROW_TILE == 0 and seq % Q_TILE == 0 and x.shape[2] == D_MODEL
    assert seq == CMP_STRIDE * _CHUNKS and seq == SEL_BLOCK * N_SEL and Q_TILE == KV_TILE
    memkv = _memkv(mem, norm_mem_g, w_mem_kv.astype(BF16))
    w_out_b = w_out.astype(BF16)
    w_ffn_in_b = w_ffn_in.astype(BF16)
    w_ffn_out_b = w_ffn_out.astype(BF16)

    tok, xq = _conv_front(x, norm_mix_g[0], w_in_conv[0].astype(BF16), conv_w[0])
    h = _mixout(x, tok, xq, memkv, w_out_b, 0)
    h = _ffn(h, norm_ffn_g[0], w_ffn_in_b, w_ffn_out_b, norm_final_g, 0, final=False)

    q, cmp_raw, slc, slc1, win, win1, gates, xq = _nsa_proj(h, norm_mix_g[1], _nsa_weight(w_in_nsa[0]))
    pos4 = jnp.stack([jnp.tile(cmp_pos_k[0], (1, NSA_KV_GROUPS)),
                      jnp.tile(cmp_pos_v[0], (1, NSA_KV_GROUPS))])
    w1 = jnp.stack([cmp_w1_k[0], cmp_w1_v[0]]).astype(BF16)
    kc = _compress(cmp_raw, pos4, w1, cmp_w2_k[0].astype(BF16)[None], 0, transposed=False)
    vct = _compress(cmp_raw, pos4, w1, cmp_w2_v[0].T.astype(BF16)[None], 1, transposed=True)
    tok = _nsa_attention(q, kc, vct, slc, slc1, win, win1, _block_mask_rows(seq), gates,
                         *_bias_tables(rel_bias))
    h = _mixout(h, tok, xq, memkv, w_out_b, 1)
    return _ffn(h, norm_ffn_g[1], w_ffn_in_b, w_ffn_out_b, norm_final_g, 1, final=True)
```

```python
import functools

import numpy as np
import jax
import jax.numpy as jnp
from jax import lax
from jax.experimental import pallas as pl
from jax.experimental.pallas import tpu as pltpu

F32 = jnp.float32
BF16 = jnp.bfloat16

D_MODEL = 1024
N_MEM = 256
CONV_WIDTH = 3
NSA_HEADS = 16
NSA_HEAD_DIM = 64
NSA_KV_GROUPS = 4
NSA_HPG = NSA_HEADS // NSA_KV_GROUPS
CMP_BLOCK = 32
CMP_STRIDE = 16
CMP_HIDDEN = 2 * NSA_HEAD_DIM
SEL_BLOCK = 64
SEL_TOPK = 16
SEL_FORCE = 1.0e4
WINDOW = 512
XATTN_HEADS = 4
XATTN_HEAD_DIM = 128
XATTN_WIDTH = XATTN_HEADS * XATTN_HEAD_DIM
FFN_HIDDEN = 2816
REL_BUCKETS = 32
REL_MAX_DIST = 128
RMS_EPS = 1e-6
NEG = -1e30
LOG2E = 1.4426950408889634

V7X_LANES = 128
V7X_VMEM_BYTES = 64 * 1024 * 1024

ROW_TILE = 512
CONV_COLS = 256
FFN_ROW_TILE = 512
FFN_CHUNK = 256
Q_TILE = 256
KV_TILE = 256
FAR_GROUP = 2
CMP_VARIANTS = 4
N_CMP_PAD = 512
N_SEL = 128

_NT = (((1,), (1,)), ((), ()))


def _vmem_limit(nbytes):
    return int(min(max(nbytes, 16 * 1024 * 1024), V7X_VMEM_BYTES - 8 * 1024 * 1024))


def _rms(x, g):
    ms = jnp.mean(x * x, axis=-1, keepdims=True)
    return x * lax.rsqrt(ms + RMS_EPS) * g


def _dot(a, b):
    return jnp.dot(a, b, preferred_element_type=F32)


def _dot_nt(a, b):
    return lax.dot_general(a, b, _NT, preferred_element_type=F32)


def _sigmoid(x):
    return 1.0 / (1.0 + jnp.exp(-x))


def _resident(block_shape, index_map):
    return pl.BlockSpec(block_shape, index_map, pipeline_mode=pl.Buffered(1))


def _memkv_kernel(mem_ref, g_ref, w_ref, o_ref):
    xn = _rms(mem_ref[0], g_ref[0]).astype(BF16)
    o_ref[0, 0] = _dot(xn, w_ref[0]).astype(BF16)


def _memkv(mem, gains, w):
    depth, batch = w.shape[0], mem.shape[0]
    return pl.pallas_call(
        _memkv_kernel,
        out_shape=jax.ShapeDtypeStruct((depth, batch, N_MEM, 2 * XATTN_WIDTH), BF16),
        grid=(depth, batch),
        in_specs=[
            pl.BlockSpec((1, N_MEM, D_MODEL), lambda l, b: (b, 0, 0)),
            pl.BlockSpec((1, 1, D_MODEL), lambda l, b: (l, 0, 0)),
            pl.BlockSpec((1, D_MODEL, 2 * XATTN_WIDTH), lambda l, b: (l, 0, 0)),
        ],
        out_specs=pl.BlockSpec((1, 1, N_MEM, 2 * XATTN_WIDTH), lambda l, b: (l, b, 0, 0)),
        compiler_params=pltpu.CompilerParams(
            dimension_semantics=("arbitrary", "arbitrary"),
            vmem_limit_bytes=_vmem_limit(24 * 1024 * 1024)),
        name="memkv",
    )(mem, gains.reshape(depth, 1, D_MODEL), w)


def _conv_kernel(h_ref, hp_ref, g_ref, w_ref, cw_ref, tok_ref, xq_ref):
    d = D_MODEL
    i = pl.program_id(1)
    g = g_ref[...]
    hn = _rms(h_ref[0], g).astype(BF16)
    hpn = _rms(hp_ref[0], g).astype(BF16)
    cw = cw_ref[...]
    row = lax.broadcasted_iota(jnp.int32, (h_ref.shape[1], CONV_COLS), 0)

    def projections(c):
        lo = c * CONV_COLS
        col = lambda part: w_ref[:, part * d + lo:part * d + lo + CONV_COLS]
        vp = jnp.where(i > 0, _dot(hpn, col(1)) * _dot(hpn, col(2)), 0.0)
        return _dot(hn, col(0)), _dot(hn, col(1)) * _dot(hn, col(2)), vp

    n_chunks = d // CONV_COLS
    ready = projections(0)
    for c in range(n_chunks):
        u_b, v, vp = ready
        if c + 1 < n_chunks:
            ready = projections(c + 1)
        else:
            xq_ref[0] = _dot(hn, w_ref[:, 3 * d:]).astype(BF16)
        cols = slice(c * CONV_COLS, (c + 1) * CONV_COLS)
        v1 = jnp.where(row == 0, vp[7:8], pltpu.roll(v, 1, axis=0))
        v2 = jnp.where(row == 0, vp[6:7], jnp.where(row == 1, vp[7:8], pltpu.roll(v, 2, axis=0)))
        conv = v2 * cw[0:1, cols] + v1 * cw[1:2, cols] + v * cw[2:3, cols]
        tok_ref[0, :, cols] = (u_b * conv).astype(BF16)


def _conv_front(h, g, w, cw):
    batch, seq, d = h.shape
    tm = ROW_TILE
    width = w.shape[1]
    halo = 8
    return pl.pallas_call(
        _conv_kernel,
        out_shape=(jax.ShapeDtypeStruct((batch, seq, d), BF16),
                   jax.ShapeDtypeStruct((batch, seq, XATTN_WIDTH), BF16)),
        grid=(batch, seq // tm),
        in_specs=[
            pl.BlockSpec((1, tm, d), lambda b, i: (b, i, 0)),
            pl.BlockSpec((1, halo, d), lambda b, i: (b, jnp.maximum(i * (tm // halo) - 1, 0), 0)),
            pl.BlockSpec((1, d), lambda b, i: (0, 0)),
            pl.BlockSpec((d, width), lambda b, i: (0, 0)),
            pl.BlockSpec((CONV_WIDTH, d), lambda b, i: (0, 0)),
        ],
        out_specs=(pl.BlockSpec((1, tm, d), lambda b, i: (b, i, 0)),
                   pl.BlockSpec((1, tm, XATTN_WIDTH), lambda b, i: (b, i, 0))),
        compiler_params=pltpu.CompilerParams(
            dimension_semantics=("arbitrary", "arbitrary"),
            vmem_limit_bytes=_vmem_limit(48 * 1024 * 1024)),
        name="conv_front",
    )(h, h, g.reshape(1, d), w, cw)


def _mixout_kernel(h_ref, tok_ref, xq_ref, kv_ref, w_ref, o_ref):
    d = D_MODEL
    xq = xq_ref[0]
    kv = kv_ref[0, 0]
    scale = XATTN_HEAD_DIM ** -0.5 * LOG2E
    heads = [slice(hh * XATTN_HEAD_DIM, (hh + 1) * XATTN_HEAD_DIM) for hh in range(XATTN_HEADS)]
    logits = [_dot_nt(xq[:, hd], kv[:, hd]) for hd in heads]
    acc = h_ref[0] + _dot(tok_ref[0], w_ref[0, 0:d])
    outs = []
    for hd, s in zip(heads, logits):
        s = s * scale
        e = jnp.exp2(s - jnp.max(s, axis=-1, keepdims=True))
        p = e / jnp.sum(e, axis=-1, keepdims=True)
        outs.append(_dot(p.astype(BF16), kv[:, XATTN_WIDTH + hd.start:XATTN_WIDTH + hd.stop]))
    mo = jnp.concatenate(outs, axis=-1).astype(BF16)
    o_ref[0] = acc + _dot(mo, w_ref[0, d:])


def _mixout(h, tok, xq, kv, w, layer):
    batch, seq, d = h.shape
    tm = ROW_TILE
    return pl.pallas_call(
        _mixout_kernel,
        out_shape=jax.ShapeDtypeStruct((batch, seq, d), F32),
        grid=(batch, seq // tm),
        in_specs=[
            pl.BlockSpec((1, tm, d), lambda b, i: (b, i, 0)),
            pl.BlockSpec((1, tm, d), lambda b, i: (b, i, 0)),
            pl.BlockSpec((1, tm, XATTN_WIDTH), lambda b, i: (b, i, 0)),
            pl.BlockSpec((1, 1, N_MEM, 2 * XATTN_WIDTH), lambda b, i: (layer, b, 0, 0)),
            pl.BlockSpec((1, d + XATTN_WIDTH, d), lambda b, i: (layer, 0, 0)),
        ],
        out_specs=pl.BlockSpec((1, tm, d), lambda b, i: (b, i, 0)),
        compiler_params=pltpu.CompilerParams(
            dimension_semantics=("arbitrary", "arbitrary"),
            vmem_limit_bytes=_vmem_limit(40 * 1024 * 1024)),
        name="mixout",
    )(h, tok, xq, kv, w)


def _ffn_kernel(h_ref, g_ref, w1_ref, w2_ref, gf_ref, o_ref, *, final):
    h = h_ref[0]
    hn = _rms(h, g_ref[...]).astype(BF16)
    acc = h
    chunks = [c * FFN_CHUNK for c in range(FFN_HIDDEN // FFN_CHUNK)]
    gate_up = [(_dot(hn, w1_ref[0, :, lo:lo + FFN_CHUNK]),
                _dot(hn, w1_ref[0, :, FFN_HIDDEN + lo:FFN_HIDDEN + lo + FFN_CHUNK]))
               for lo in chunks[:1]]
    for c, lo in enumerate(chunks):
        if c + 1 < len(chunks):
            nxt = chunks[c + 1]
            gate_up.append((_dot(hn, w1_ref[0, :, nxt:nxt + FFN_CHUNK]),
                            _dot(hn, w1_ref[0, :, FFN_HIDDEN + nxt:FFN_HIDDEN + nxt + FFN_CHUNK])))
        a, u = gate_up[c]
        act = (a * _sigmoid(a) * u).astype(BF16)
        acc = acc + _dot(act, w2_ref[0, lo:lo + FFN_CHUNK])
    if final:
        acc = _rms(acc, gf_ref[...])
    o_ref[0] = acc


def _ffn(h, g, w1, w2, gf, layer, final):
    batch, seq, d = h.shape
    tm = FFN_ROW_TILE
    return pl.pallas_call(
        functools.partial(_ffn_kernel, final=final),
        out_shape=jax.ShapeDtypeStruct((batch, seq, d), F32),
        grid=(batch, seq // tm),
        in_specs=[
            pl.BlockSpec((1, tm, d), lambda b, i: (b, i, 0)),
            pl.BlockSpec((1, d), lambda b, i: (0, 0)),
            pl.BlockSpec((1, d, 2 * FFN_HIDDEN), lambda b, i: (layer, 0, 0)),
            pl.BlockSpec((1, FFN_HIDDEN, d), lambda b, i: (layer, 0, 0)),
            pl.BlockSpec((1, d), lambda b, i: (0, 0)),
        ],
        out_specs=pl.BlockSpec((1, tm, d), lambda b, i: (b, i, 0)),
        compiler_params=pltpu.CompilerParams(
            dimension_semantics=("arbitrary", "arbitrary"),
            vmem_limit_bytes=_vmem_limit(52 * 1024 * 1024)),
        name="ffn",
    )(h, g.reshape(1, d), w1, w2, gf.reshape(1, d))


_QW = NSA_HEADS * NSA_HEAD_DIM
_KVW = NSA_KV_GROUPS * NSA_HEAD_DIM
_PAIR = 2 * NSA_HEAD_DIM
_C_Q = 0
_C_CMP = _C_Q + _QW
_C_SLC = _C_CMP + 2 * _KVW
_C_WIN = _C_SLC + NSA_KV_GROUPS * _PAIR
_C_GATE = _C_WIN + NSA_KV_GROUPS * _PAIR
_C_XQ = _C_GATE + V7X_LANES
_C_END = _C_XQ + XATTN_WIDTH


_SUM_ROWS = 16
_VT_ROWS = _SUM_ROWS + NSA_HEAD_DIM


def _ones_and_vt(kv):
    v_t = kv.T[NSA_HEAD_DIM:]
    return jnp.concatenate([jnp.ones((_SUM_ROWS, kv.shape[0]), F32), v_t], axis=0).astype(BF16)


def _nsa_proj_kernel(h_ref, g_ref, w_ref, q_ref, cr_ref, slc_ref, slc1_ref, win_ref, win1_ref,
                     gt_ref, xq_ref):
    hn = _rms(h_ref[0], g_ref[...]).astype(BF16)
    q = _dot(hn, w_ref[:, _C_Q:_C_CMP]) * (NSA_HEAD_DIM ** -0.5 * LOG2E)
    gates = _sigmoid(_dot(hn, w_ref[:, _C_GATE:_C_XQ]))
    for c in range(ROW_TILE // Q_TILE):
        rows = slice(c * Q_TILE, (c + 1) * Q_TILE)
        q_ref[0, c] = q[rows].T.astype(BF16)
        gt_ref[0, c] = gates[rows].T
    cr = _dot(hn, w_ref[:, _C_CMP:_C_SLC])
    for p in range(2 * _KVW // V7X_LANES):
        cr_ref[0, p] = cr[:, p * V7X_LANES:(p + 1) * V7X_LANES]
    s = _dot(hn, w_ref[:, _C_SLC:_C_WIN])
    w = _dot(hn, w_ref[:, _C_WIN:_C_GATE])
    k_lanes = lax.broadcasted_iota(jnp.int32, (s.shape[0], _PAIR), 1) < NSA_HEAD_DIM
    for g in range(NSA_KV_GROUPS):
        sg = s[:, g * _PAIR:(g + 1) * _PAIR]
        wg = w[:, g * _PAIR:(g + 1) * _PAIR]
        slc_ref[0, g] = jnp.where(k_lanes, sg, 0.0).astype(BF16)
        win_ref[0, g] = jnp.where(k_lanes, wg, 0.0).astype(BF16)
        for c in range(ROW_TILE // KV_TILE):
            rows = slice(c * KV_TILE, (c + 1) * KV_TILE)
            slc1_ref[0, g, c] = _ones_and_vt(sg[rows])
            win1_ref[0, g, c] = _ones_and_vt(wg[rows])
    xq_ref[0] = _dot(hn, w_ref[:, _C_XQ:_C_END]).astype(BF16)


def _nsa_proj(h, g, w):
    batch, seq, d = h.shape
    tm = ROW_TILE
    grp = NSA_KV_GROUPS
    row = lambda b, i: (b, i, 0)
    pair_shape = jax.ShapeDtypeStruct((batch, grp, seq, _PAIR), BF16)
    pair_spec = pl.BlockSpec((1, grp, tm, _PAIR), lambda b, i: (b, 0, i, 0))
    vt_shape = jax.ShapeDtypeStruct((batch, grp, seq // KV_TILE, _VT_ROWS, KV_TILE), BF16)
    vt_spec = pl.BlockSpec((1, grp, tm // KV_TILE, _VT_ROWS, KV_TILE), lambda b, i: (b, 0, i, 0, 0))
    return pl.pallas_call(
        _nsa_proj_kernel,
        out_shape=(jax.ShapeDtypeStruct((batch, seq // Q_TILE, _QW, Q_TILE), BF16),
                   jax.ShapeDtypeStruct((batch, 2 * _KVW // V7X_LANES, seq, V7X_LANES), F32),
                   pair_shape, vt_shape, pair_shape, vt_shape,
                   jax.ShapeDtypeStruct((batch, seq // Q_TILE, V7X_LANES, Q_TILE), F32),
                   jax.ShapeDtypeStruct((batch, seq, XATTN_WIDTH), BF16)),
        grid=(batch, seq // tm),
        in_specs=[
            pl.BlockSpec((1, tm, d), row),
            pl.BlockSpec((1, d), lambda b, i: (0, 0)),
            pl.BlockSpec((d, _C_END), lambda b, i: (0, 0)),
        ],
        out_specs=(pl.BlockSpec((1, tm // Q_TILE, _QW, Q_TILE), lambda b, i: (b, i, 0, 0)),
                   pl.BlockSpec((1, 2 * _KVW // V7X_LANES, tm, V7X_LANES), lambda b, i: (b, 0, i, 0)),
                   pair_spec, vt_spec, pair_spec, vt_spec,
                   pl.BlockSpec((1, tm // Q_TILE, V7X_LANES, Q_TILE), lambda b, i: (b, i, 0, 0)),
                   pl.BlockSpec((1, tm, XATTN_WIDTH), row)),
        compiler_params=pltpu.CompilerParams(
            dimension_semantics=("arbitrary", "arbitrary"),
            vmem_limit_bytes=_vmem_limit(48 * 1024 * 1024)),
        name="nsa_proj",
    )(h, g.reshape(1, d), w)


_CHUNKS = 512


def _compress_kernel(raw_ref, pos_ref, w1_ref, w2_ref, o_ref, h2_scr, *, transposed):
    dk = NSA_HEAD_DIM
    half = CMP_BLOCK // 2
    h1 = [jnp.zeros((_CHUNKS, CMP_HIDDEN), F32) for _ in range(NSA_KV_GROUPS)]
    h2 = [jnp.zeros((_CHUNKS, CMP_HIDDEN), F32) for _ in range(NSA_KV_GROUPS)]
    gpl = V7X_LANES // dk
    for l in range(half):
        for p in range(NSA_KV_GROUPS // gpl):
            lanes = slice(p * V7X_LANES, (p + 1) * V7X_LANES)
            rows = raw_ref[0, p, pl.ds(l, _CHUNKS, stride=CMP_STRIDE), :]
            a1 = (rows + pos_ref[0, l:l + 1, lanes]).astype(BF16)
            a2 = (rows + pos_ref[0, half + l:half + l + 1, lanes]).astype(BF16)
            for gg in range(gpl):
                g = p * gpl + gg
                h1[g] = h1[g] + _dot(a1[:, gg * dk:(gg + 1) * dk], w1_ref[0, l * dk:(l + 1) * dk, :])
                h2[g] = h2[g] + _dot(a2[:, gg * dk:(gg + 1) * dk],
                                     w1_ref[0, (half + l) * dk:(half + l + 1) * dk, :])
    h2_scr[_CHUNKS:_CHUNKS + 8, :] = jnp.zeros((8, CMP_HIDDEN), F32)
    for g in range(NSA_KV_GROUPS):
        h2_scr[0:_CHUNKS, :] = h2[g]
        hm = h1[g] + h2_scr[pl.ds(1, _CHUNKS), :]
        hm = (hm * _sigmoid(hm)).astype(BF16)
        if transposed:
            o_ref[0, g] = _dot_nt(w2_ref[0], hm).astype(BF16)
        else:
            out = _dot(hm, w2_ref[0])
            o_ref[0, g] = jnp.concatenate([out, jnp.zeros_like(out)], axis=1).astype(BF16)


def _compress(raw, pos4, w1, w2, which, transposed):
    batch, _, seq, _ = raw.shape
    grp = NSA_KV_GROUPS
    tiles = _KVW // V7X_LANES
    if transposed:
        out_shape = (batch, grp, NSA_HEAD_DIM, N_CMP_PAD)
        w2_block = (1, NSA_HEAD_DIM, CMP_HIDDEN)
    else:
        out_shape = (batch, grp, N_CMP_PAD, _PAIR)
        w2_block = (1, CMP_HIDDEN, NSA_HEAD_DIM)
    return pl.pallas_call(
        functools.partial(_compress_kernel, transposed=transposed),
        out_shape=jax.ShapeDtypeStruct(out_shape, BF16),
        grid=(batch,),
        in_specs=[
            pl.BlockSpec((1, tiles, seq, V7X_LANES), lambda b: (b, which, 0, 0)),
            pl.BlockSpec((1, CMP_BLOCK, _KVW), lambda b: (which, 0, 0)),
            pl.BlockSpec((1, CMP_BLOCK * NSA_HEAD_DIM, CMP_HIDDEN), lambda b: (which, 0, 0)),
            pl.BlockSpec(w2_block, lambda b: (0, 0, 0)),
        ],
        out_specs=pl.BlockSpec((1,) + out_shape[1:], lambda b: (b, 0, 0, 0)),
        scratch_shapes=[pltpu.VMEM((_CHUNKS + 8, CMP_HIDDEN), F32)],
        compiler_params=pltpu.CompilerParams(
            dimension_semantics=("arbitrary",),
            vmem_limit_bytes=_vmem_limit(40 * 1024 * 1024)),
        name="compress_v" if transposed else "compress_k",
    )(raw, pos4, w1, w2)


def _t5_bucket(dist):
    n = jnp.maximum(dist, 0)
    max_exact = REL_BUCKETS // 2
    nf = jnp.maximum(n, 1).astype(F32)
    large = max_exact + (jnp.log(nf / max_exact) / np.log(REL_MAX_DIST / max_exact)
                         * (REL_BUCKETS - max_exact)).astype(jnp.int32)
    large = jnp.minimum(large, REL_BUCKETS - 1)
    return jnp.where(n < max_exact, n, large)


def _table_kernel(thr_ref, rb_ref, cbt_ref, tslc_ref, twin_ref):
    tq = Q_TILE
    g = pl.program_id(0)

    def entries(dist, h, limit):
        val = jnp.full(dist.shape, rb_ref[0, h], F32)
        for k in range(1, REL_BUCKETS):
            val = jnp.where(dist >= thr_ref[k], rb_ref[k, h], val)
        val = (val - rb_ref[REL_BUCKETS - 1, h]) * LOG2E
        return jnp.where(jnp.where(dist >= 0, dist, limit) < limit, val, NEG)

    def far_entries(dist, limit):
        return jnp.where(jnp.where(dist >= 0, dist, limit) < limit, 0.0, NEG)

    def fill(ref, band, dist_of_rows, n_rows, limit, h):
        step = 64
        for r0 in range(0, n_rows, step):
            shape = (step, tq)
            dist = dist_of_rows(r0 + lax.broadcasted_iota(jnp.int32, shape, 0),
                                lax.broadcasted_iota(jnp.int32, shape, 1))
            lo, hi = dist_of_rows(r0 + step - 1, 0), dist_of_rows(r0, tq - 1)
            near = hi >= 0 and lo < REL_MAX_DIST
            ref[0, r0:r0 + step, band] = entries(dist, h, limit) if near else far_entries(dist, limit)

    no_limit = 1 << 30
    for j in range(NSA_HPG):
        h = g * NSA_HPG + j
        band = slice(j * tq, (j + 1) * tq)
        fill(cbt_ref, band, lambda u, q: q - (CMP_BLOCK - 1) - CMP_STRIDE * (u - N_CMP_PAD),
             2 * N_CMP_PAD, no_limit, h)
        fill(tslc_ref, band, lambda u, q: q - u + tq, 2 * tq, no_limit, h)
        fill(twin_ref, band, lambda u, q: q - u + WINDOW, WINDOW + tq, WINDOW, h)


def _bias_tables(rel_bias):
    tq = Q_TILE
    rows = NSA_HPG * tq
    buckets = _t5_bucket(jnp.arange(REL_MAX_DIST + 1))
    thr = jnp.sum(buckets[None, :] < jnp.arange(REL_BUCKETS)[:, None], axis=1).astype(jnp.int32)
    smem = pl.BlockSpec(memory_space=pltpu.SMEM)
    return pl.pallas_call(
        _table_kernel,
        out_shape=(jax.ShapeDtypeStruct((NSA_KV_GROUPS, 2 * N_CMP_PAD, rows), F32),
                   jax.ShapeDtypeStruct((NSA_KV_GROUPS, 2 * tq, rows), F32),
                   jax.ShapeDtypeStruct((NSA_KV_GROUPS, WINDOW + tq, rows), F32)),
        grid=(NSA_KV_GROUPS,),
        in_specs=[smem, smem],
        out_specs=(pl.BlockSpec((1, 2 * N_CMP_PAD, rows), lambda g: (g, 0, 0)),
                   pl.BlockSpec((1, 2 * tq, rows), lambda g: (g, 0, 0)),
                   pl.BlockSpec((1, WINDOW + tq, rows), lambda g: (g, 0, 0))),
        compiler_params=pltpu.CompilerParams(
            dimension_semantics=("arbitrary",),
            vmem_limit_bytes=_vmem_limit(40 * 1024 * 1024)),
        name="bias_tables",
    )(thr, rel_bias.astype(F32))


def _block_mask_rows(seq):
    pos = jnp.arange(seq, dtype=jnp.int32)[:, None]
    col = jnp.arange(N_SEL, dtype=jnp.int32)[None, :]
    onehot = jnp.where(col == pos // SEL_BLOCK, NEG, 0.0)
    return jnp.concatenate([onehot, jnp.full((Q_TILE, N_SEL), NEG)], axis=0).astype(BF16)


def _online_absorb(s, v1t, state):
    m = jnp.max(s, axis=0, keepdims=True)
    if state is not None:
        m = jnp.maximum(state[0], m)
    acc = _dot(v1t, jnp.exp2(s - m).astype(BF16))
    if state is not None:
        acc = jnp.exp2(state[0] - m) * state[1] + acc
    return m, acc


def _normalise_t(acc):
    return acc[_SUM_ROWS:] * (1.0 / acc[0:1])


def _nsa_kernel(q_ref, kc_ref, vct_ref, slc_ref, slc1_ref, win_ref, win1_ref, bm_ref, gt_ref,
                cbt_ref, tslc_ref, twin_ref, o_ref, ps_scr, lhs_scr, m_scr, acc_scr, s_scr, ocmp_scr,
                owin_scr):
    tq = Q_TILE
    hpg = NSA_HPG
    dk = NSA_HEAD_DIM
    seq = slc_ref.shape[2]
    g = pl.program_id(1)
    i = pl.program_id(2)
    q0 = pl.multiple_of(i * tq, tq)
    bands = [slice(j * tq, (j + 1) * tq) for j in range(hpg)]

    for j in range(hpg):
        lhs_scr[0:dk, bands[j]] = q_ref[0, 0, j * dk:(j + 1) * dk, :]
    lhs_scr[dk:dk + _SUM_ROWS, :] = jnp.ones((_SUM_ROWS, hpg * tq), BF16)
    lhs_scr[dk + _SUM_ROWS:_PAIR, :] = jnp.zeros((_PAIR - dk - _SUM_ROWS, hpg * tq), BF16)
    qs_t = lhs_scr[0:_PAIR, :]

    n_prev = WINDOW // tq
    mask_lane = lax.broadcasted_iota(jnp.int32, (1, _PAIR), 1) == dk

    def win_logits(r):
        tile = jnp.maximum(i - r, 0)
        keys = win_ref[0, 0, pl.ds(pl.multiple_of(tile * tq, tq), tq), :]
        if r:
            keys = keys + jnp.where(jnp.logical_and(mask_lane, i < r), NEG, 0.0).astype(BF16)
        rows = slice((n_prev - r) * tq, (n_prev - r + 1) * tq)
        return _dot(keys, qs_t) + twin_ref[0, rows, :]

    def win_absorb(r, s, state):
        return _online_absorb(s, win1_ref[0, 0, jnp.maximum(i - r, 0)], state)

    def compress_and_select(n_cmp):
        n_sel = n_cmp * CMP_STRIDE // SEL_BLOCK
        st = _dot(kc_ref[0, 0, 0:n_cmp, :], qs_t)
        st = st + cbt_ref[0, pl.ds(pl.multiple_of(N_CMP_PAD - (tq // CMP_STRIDE) * i, 8), n_cmp), :]
        mx = jnp.maximum(jnp.max(st, axis=0, keepdims=True), 0.1 * NEG)
        e = jnp.exp2(st - mx)
        den = jnp.sum(e, axis=0, keepdims=True)
        inv = jnp.where(den > 0.0, 1.0 / den, 0.0)
        ocmp_scr[...] = _dot(vct_ref[0, 0, :, 0:n_cmp], e.astype(BF16)) * inv
        psum = e[:, bands[0]] * inv[:, bands[0]]
        for j in range(1, hpg):
            psum = psum + e[:, bands[j]] * inv[:, bands[j]]

        ratio = SEL_BLOCK // CMP_STRIDE
        imp_tiles = []
        for c in range(tq // V7X_LANES):
            ps_scr[c, 0:8, :] = jnp.zeros((8, V7X_LANES), F32)
            ps_scr[c, 8:8 + n_cmp, :] = psum[:, c * V7X_LANES:(c + 1) * V7X_LANES]
            taps = [ps_scr[c, pl.ds(7 + r, n_sel, stride=ratio), :] for r in range(5)]
            imp_tiles.append(0.5 * taps[0] + taps[1] + taps[2] + taps[3] + 0.5 * taps[4])
        imp = jnp.concatenate(imp_tiles, axis=1)

        jrow = lax.broadcasted_iota(jnp.int32, (n_sel, tq), 0)
        cur = (q0 + lax.broadcasted_iota(jnp.int32, (n_sel, tq), 1)) // SEL_BLOCK
        imp = jnp.where(jrow <= cur, imp, -SEL_FORCE)
        for forced_block in (0, cur, cur - 1):
            imp = jnp.where(jrow == forced_block, SEL_FORCE, imp)
        taken = -3.0e38
        n_tiles = n_prev + 1
        s_win = [win_logits(0), win_logits(1)]
        win_state, absorbed = None, 0
        for rnd in range(SEL_TOPK):
            top = jnp.max(imp, axis=0, keepdims=True)
            first = jnp.min(jnp.where(imp == top, jrow, n_sel), axis=0, keepdims=True)
            imp = jnp.where(jrow == first, taken, imp)
            while absorbed < (rnd + 1) * n_tiles // SEL_TOPK:
                if absorbed + 2 < n_tiles:
                    s_win.append(win_logits(absorbed + 2))
                win_state = win_absorb(absorbed, s_win[absorbed], win_state)
                absorbed += 1
        owin_scr[...] = _normalise_t(win_state[1])
        not_sel = jnp.where(imp < 0.5 * taken, 0.0, 1.0).astype(BF16)
        for j in range(hpg):
            lhs_scr[_PAIR:_PAIR + n_sel, bands[j]] = not_sel
        if n_sel < N_SEL:
            lhs_scr[_PAIR + n_sel:2 * _PAIR, :] = jnp.ones((N_SEL - n_sel, hpg * tq), BF16)

    tiles_per_variant = (seq // tq) // CMP_VARIANTS
    for v in range(CMP_VARIANTS):
        pl.when(i // tiles_per_variant == v)(
            functools.partial(compress_and_select, N_CMP_PAD * (v + 1) // CMP_VARIANTS))
    lhs = lhs_scr[...]

    def key_tile(kv_ref, tile, mask_tile):
        return jnp.concatenate([kv_ref[0, 0, pl.ds(pl.multiple_of(tile * tq, tq), tq), :],
                                bm_ref[pl.ds(pl.multiple_of(mask_tile * tq, tq), tq), :]], axis=1)

    all_masked = seq // tq

    prev = jnp.maximum(i - 1, 0)
    s_own = _dot(key_tile(slc_ref, i, i), lhs) + tslc_ref[0, tq:2 * tq, :]
    s_prev = (_dot(key_tile(slc_ref, prev, jnp.where(i == 0, all_masked, i - 1)), lhs)
              + tslc_ref[0, 0:tq, :])

    n_far = jnp.maximum(i - 1, 0)
    n_groups = (n_far + FAR_GROUP - 1) // FAR_GROUP

    def tile_or_none(kt):
        valid = kt < n_far
        return jnp.where(valid, kt, 0), jnp.where(valid, kt, all_masked)

    def far_logits(kg):
        tiles = [key_tile(slc_ref, *tile_or_none(FAR_GROUP * kg + t)) for t in range(FAR_GROUP)]
        return _dot(jnp.concatenate(tiles, axis=0), lhs_scr[...])

    def absorb(slot, kg):
        s = s_scr[slot]
        m_old = m_scr[0:1, :]
        m_new = jnp.maximum(m_old, jnp.max(s, axis=0, keepdims=True))
        p = jnp.exp2(s - m_new)
        v1t = jnp.concatenate([slc1_ref[0, 0, tile_or_none(FAR_GROUP * kg + t)[0]]
                               for t in range(FAR_GROUP)], axis=1)
        acc_scr[...] = jnp.exp2(m_old - m_new) * acc_scr[...] + _dot(v1t, p.astype(BF16))
        m_scr[0:1, :] = m_new

    def far_trip(base, groups):
        for t in range(groups):
            s_scr[(t + 1) % 2] = far_logits(base + t + 1)
            absorb(t % 2, base + t)

    s_scr[0] = far_logits(0)
    state = _online_absorb(s_own, slc1_ref[0, 0, i], None)
    m, acc = _online_absorb(s_prev, slc1_ref[0, 0, prev], state)
    m_scr[0:1, :] = m
    acc_scr[...] = acc
    n_long = n_groups // 4

    def long_body(k, carry):
        far_trip(4 * k, 4)
        return carry

    lax.fori_loop(0, n_long, long_body, 0)
    done = 4 * n_long
    pair = (n_groups - done) // 2

    @pl.when(pair > 0)
    def _():
        far_trip(done, 2)

    @pl.when(n_groups - done - 2 * pair > 0)
    def _():
        far_trip(done + 2 * pair, 1)

    o_slc_t = _normalise_t(acc_scr[...])

    merged = []
    for j in range(hpg):
        h = g * hpg + j
        gate = [gt_ref[0, 0, pl.ds(br * NSA_HEADS + h, 1), :] for br in range(3)]
        merged.append(gate[0] * ocmp_scr[:, bands[j]] + gate[1] * o_slc_t[:, bands[j]]
                      + gate[2] * owin_scr[:, bands[j]])
    o_ref[0] = jnp.concatenate(merged, axis=0).T.astype(BF16)


def _nsa_attention(q, kc, vct, slc, slc1, win, win1, block_mask, gates, cbt, tslc, twin):
    batch, _, seq, _ = slc.shape
    grp = NSA_KV_GROUPS
    tq = Q_TILE
    rows = NSA_HPG * tq
    per_group = lambda b, g, i: (b, g, 0, 0)
    per_group_t = lambda b, g, i: (b, g, 0, 0, 0)
    return pl.pallas_call(
        _nsa_kernel,
        out_shape=jax.ShapeDtypeStruct((batch, seq, _QW), BF16),
        grid=(batch, grp, seq // tq),
        in_specs=[
            pl.BlockSpec((1, 1, NSA_HPG * NSA_HEAD_DIM, tq), lambda b, g, i: (b, i, g, 0)),
            pl.BlockSpec((1, 1, N_CMP_PAD, _PAIR), per_group),
            pl.BlockSpec((1, 1, NSA_HEAD_DIM, N_CMP_PAD), per_group),
            pl.BlockSpec((1, 1, seq, _PAIR), per_group),
            pl.BlockSpec((1, 1, seq // tq, _VT_ROWS, tq), per_group_t),
            pl.BlockSpec((1, 1, seq, _PAIR), per_group),
            pl.BlockSpec((1, 1, seq // tq, _VT_ROWS, tq), per_group_t),
            _resident((seq + tq, N_SEL), lambda b, g, i: (0, 0)),
            pl.BlockSpec((1, 1, V7X_LANES, tq), lambda b, g, i: (b, i, 0, 0)),
            pl.BlockSpec((1, 2 * N_CMP_PAD, rows), lambda b, g, i: (g, 0, 0)),
            pl.BlockSpec((1, 2 * tq, rows), lambda b, g, i: (g, 0, 0)),
            pl.BlockSpec((1, WINDOW + tq, rows), lambda b, g, i: (g, 0, 0)),
        ],
        out_specs=pl.BlockSpec((1, tq, NSA_HPG * NSA_HEAD_DIM), lambda b, g, i: (b, i, g)),
        scratch_shapes=[pltpu.VMEM((tq // V7X_LANES, N_CMP_PAD + 16, V7X_LANES), F32),
                        pltpu.VMEM((2 * _PAIR, rows), BF16),
                        pltpu.VMEM((8, rows), F32),
                        pltpu.VMEM((_VT_ROWS, rows), F32),
                        pltpu.VMEM((2, FAR_GROUP * tq, rows), F32),
                        pltpu.VMEM((NSA_HEAD_DIM, rows), F32),
                        pltpu.VMEM((NSA_HEAD_DIM, rows), F32)],
        compiler_params=pltpu.CompilerParams(
            dimension_semantics=("arbitrary", "arbitrary", "arbitrary"),
            vmem_limit_bytes=_vmem_limit(52 * 1024 * 1024)),
        name="nsa_attention",
    )(q, kc, vct, slc, slc1, win, win1, block_mask, gates, cbt, tslc, twin)


def _nsa_weight(w):
    d = w.shape[0]
    q = w[:, :_QW]
    kv = w[:, _QW:_QW + 6 * _KVW].reshape(d, 6, NSA_KV_GROUPS, NSA_HEAD_DIM)
    cmp_raw = kv[:, 0:2].reshape(d, 2 * _KVW)
    slc = jnp.stack([kv[:, 2], kv[:, 3]], axis=2).reshape(d, NSA_KV_GROUPS * _PAIR)
    win = jnp.stack([kv[:, 4], kv[:, 5]], axis=2).reshape(d, NSA_KV_GROUPS * _PAIR)
    gates = w[:, _QW + 6 * _KVW:_QW + 6 * _KVW + 3 * NSA_HEADS]
    gates = jnp.pad(gates, ((0, 0), (0, V7X_LANES - 3 * NSA_HEADS)))
    xq = w[:, _QW + 6 * _KVW + 3 * NSA_HEADS:]
    return jnp.concatenate([q, cmp_raw, slc, win, gates, xq], axis=1).astype(BF16)


def kernel(x, mem, norm_mix_g, norm_mem_g, norm_ffn_g, norm_final_g, w_in_conv, conv_w, w_in_nsa,
           cmp_pos_k, cmp_w1_k, cmp_w2_k, cmp_pos_v, cmp_w1_v, cmp_w2_v, rel_bias, w_mem_kv, w_out,
           w_ffn_in, w_ffn_out):
    seq = x.shape[1]
    assert seq % ROW_TILE == 0 and seq % Q_TILE == 0 and x.shape[2] == D_MODEL
    assert seq == CMP_STRIDE * _CHUNKS and seq == SEL_BLOCK * N_SEL and Q_TILE == KV_TILE
    memkv = _memkv(mem, norm_mem_g, w_mem_kv.astype(BF16))
    w_out_b = w_out.astype(BF16)
    w_ffn_in_b = w_ffn_in.astype(BF16)
    w_ffn_out_b = w_ffn_out.astype(BF16)

    tok, xq = _conv_front(x, norm_mix_g[0], w_in_conv[0].astype(BF16), conv_w[0])
    h = _mixout(x, tok, xq, memkv, w_out_b, 0)
    h = _ffn(h, norm_ffn_g[0], w_ffn_in_b, w_ffn_out_b, norm_final_g, 0, final=False)

    q, cmp_raw, slc, slc1, win, win1, gates, xq = _nsa_proj(h, norm_mix_g[1], _nsa_weight(w_in_nsa[0]))
    pos4 = jnp.stack([jnp.tile(cmp_pos_k[0], (1, NSA_KV_GROUPS)),
                      jnp.tile(cmp_pos_v[0], (1, NSA_KV_GROUPS))])
    w1 = jnp.stack([cmp_w1_k[0], cmp_w1_v[0]]).astype(BF16)
    kc = _compress(cmp_raw, pos4, w1, cmp_w2_k[0].astype(BF16)[None], 0, transposed=False)
    vct = _compress(cmp_raw, pos4, w1, cmp_w2_v[0].T.astype(BF16)[None], 1, transposed=True)
    tok = _nsa_attention(q, kc, vct, slc, slc1, win, win1, _block_mask_rows(seq), gates,
                         *_bias_tables(rel_bias))
    h = _mixout(h, tok, xq, memkv, w_out_b, 1)
    return _ffn(h, norm_ffn_g[1], w_ffn_in_b, w_ffn_out_b, norm_final_g, 1, final=True)
```

```python
import functools

import numpy as np
import jax
import jax.numpy as jnp
from jax import lax
from jax.experimental import pallas as pl
from jax.experimental.pallas import tpu as pltpu

F32 = jnp.float32
BF16 = jnp.bfloat16

D_MODEL = 1024
N_MEM = 256
CONV_WIDTH = 3
NSA_HEADS = 16
NSA_HEAD_DIM = 64
NSA_KV_GROUPS = 4
NSA_HPG = NSA_HEADS // NSA_KV_GROUPS
CMP_BLOCK = 32
CMP_STRIDE = 16
CMP_HIDDEN = 2 * NSA_HEAD_DIM
SEL_BLOCK = 64
SEL_TOPK = 16
SEL_FORCE = 1.0e4
WINDOW = 512
XATTN_HEADS = 4
XATTN_HEAD_DIM = 128
XATTN_WIDTH = XATTN_HEADS * XATTN_HEAD_DIM
FFN_HIDDEN = 2816
REL_BUCKETS = 32
REL_MAX_DIST = 128
RMS_EPS = 1e-6
NEG = -1e30
LOG2E = 1.4426950408889634

V7X_LANES = 128
V7X_VMEM_BYTES = 64 * 1024 * 1024

ROW_TILE = 512
CONV_COLS = 256
FFN_ROW_TILE = 512
FFN_CHUNK = 256
Q_TILE = 256
KV_TILE = 256
FAR_GROUP = 2
CMP_VARIANTS = 4
N_CMP_PAD = 512
N_SEL = 128

_NT = (((1,), (1,)), ((), ()))


def _vmem_limit(nbytes):
    return int(min(max(nbytes, 16 * 1024 * 1024), V7X_VMEM_BYTES - 8 * 1024 * 1024))


def _rms(x, g):
    ms = jnp.mean(x * x, axis=-1, keepdims=True)
    return x * lax.rsqrt(ms + RMS_EPS) * g


def _dot(a, b):
    return jnp.dot(a, b, preferred_element_type=F32)


def _dot_nt(a, b):
    return lax.dot_general(a, b, _NT, preferred_element_type=F32)


def _sigmoid(x):
    return 1.0 / (1.0 + jnp.exp(-x))


def _resident(block_shape, index_map):
    return pl.BlockSpec(block_shape, index_map, pipeline_mode=pl.Buffered(1))


def _memkv_kernel(mem_ref, g_ref, w_ref, o_ref):
    xn = _rms(mem_ref[0], g_ref[0]).astype(BF16)
    o_ref[0, 0] = _dot(xn, w_ref[0]).astype(BF16)


def _memkv(mem, gains, w):
    depth, batch = w.shape[0], mem.shape[0]
    return pl.pallas_call(
        _memkv_kernel,
        out_shape=jax.ShapeDtypeStruct((depth, batch, N_MEM, 2 * XATTN_WIDTH), BF16),
        grid=(depth, batch),
        in_specs=[
            pl.BlockSpec((1, N_MEM, D_MODEL), lambda l, b: (b, 0, 0)),
            pl.BlockSpec((1, 1, D_MODEL), lambda l, b: (l, 0, 0)),
            pl.BlockSpec((1, D_MODEL, 2 * XATTN_WIDTH), lambda l, b: (l, 0, 0)),
        ],
        out_specs=pl.BlockSpec((1, 1, N_MEM, 2 * XATTN_WIDTH), lambda l, b: (l, b, 0, 0)),
        compiler_params=pltpu.CompilerParams(
            dimension_semantics=("arbitrary", "arbitrary"),
            vmem_limit_bytes=_vmem_limit(24 * 1024 * 1024)),
        name="memkv",
    )(mem, gains.reshape(depth, 1, D_MODEL), w)


def _conv_kernel(h_ref, hp_ref, g_ref, w_ref, cw_ref, tok_ref, xq_ref):
    d = D_MODEL
    i = pl.program_id(1)
    g = g_ref[...]
    hn = _rms(h_ref[0], g).astype(BF16)
    hpn = _rms(hp_ref[0], g).astype(BF16)
    cw = cw_ref[...]
    row = lax.broadcasted_iota(jnp.int32, (h_ref.shape[1], CONV_COLS), 0)

    def projections(c):
        lo = c * CONV_COLS
        col = lambda part: w_ref[:, part * d + lo:part * d + lo + CONV_COLS]
        vp = jnp.where(i > 0, _dot(hpn, col(1)) * _dot(hpn, col(2)), 0.0)
        return _dot(hn, col(0)), _dot(hn, col(1)) * _dot(hn, col(2)), vp

    n_chunks = d // CONV_COLS
    ready = projections(0)
    for c in range(n_chunks):
        u_b, v, vp = ready
        if c + 1 < n_chunks:
            ready = projections(c + 1)
        else:
            xq_ref[0] = _dot(hn, w_ref[:, 3 * d:]).astype(BF16)
        cols = slice(c * CONV_COLS, (c + 1) * CONV_COLS)
        v1 = jnp.where(row == 0, vp[7:8], pltpu.roll(v, 1, axis=0))
        v2 = jnp.where(row == 0, vp[6:7], jnp.where(row == 1, vp[7:8], pltpu.roll(v, 2, axis=0)))
        conv = v2 * cw[0:1, cols] + v1 * cw[1:2, cols] + v * cw[2:3, cols]
        tok_ref[0, :, cols] = (u_b * conv).astype(BF16)


def _conv_front(h, g, w, cw):
    batch, seq, d = h.shape
    tm = ROW_TILE
    width = w.shape[1]
    halo = 8
    return pl.pallas_call(
        _conv_kernel,
        out_shape=(jax.ShapeDtypeStruct((batch, seq, d), BF16),
                   jax.ShapeDtypeStruct((batch, seq, XATTN_WIDTH), BF16)),
        grid=(batch, seq // tm),
        in_specs=[
            pl.BlockSpec((1, tm, d), lambda b, i: (b, i, 0)),
            pl.BlockSpec((1, halo, d), lambda b, i: (b, jnp.maximum(i * (tm // halo) - 1, 0), 0)),
            pl.BlockSpec((1, d), lambda b, i: (0, 0)),
            pl.BlockSpec((d, width), lambda b, i: (0, 0)),
            pl.BlockSpec((CONV_WIDTH, d), lambda b, i: (0, 0)),
        ],
        out_specs=(pl.BlockSpec((1, tm, d), lambda b, i: (b, i, 0)),
                   pl.BlockSpec((1, tm, XATTN_WIDTH), lambda b, i: (b, i, 0))),
        compiler_params=pltpu.CompilerParams(
            dimension_semantics=("arbitrary", "arbitrary"),
            vmem_limit_bytes=_vmem_limit(48 * 1024 * 1024)),
        name="conv_front",
    )(h, h, g.reshape(1, d), w, cw)


def _mixout_kernel(h_ref, tok_ref, xq_ref, kv_ref, w_ref, o_ref):
    d = D_MODEL
    xq = xq_ref[0]
    kv = kv_ref[0, 0]
    scale = XATTN_HEAD_DIM ** -0.5 * LOG2E
    heads = [slice(hh * XATTN_HEAD_DIM, (hh + 1) * XATTN_HEAD_DIM) for hh in range(XATTN_HEADS)]
    logits = [_dot_nt(xq[:, hd], kv[:, hd]) for hd in heads]
    acc = h_ref[0] + _dot(tok_ref[0], w_ref[0, 0:d])
    outs = []
    for hd, s in zip(heads, logits):
        s = s * scale
        e = jnp.exp2(s - jnp.max(s, axis=-1, keepdims=True))
        p = e / jnp.sum(e, axis=-1, keepdims=True)
        outs.append(_dot(p.astype(BF16), kv[:, XATTN_WIDTH + hd.start:XATTN_WIDTH + hd.stop]))
    mo = jnp.concatenate(outs, axis=-1).astype(BF16)
    o_ref[0] = acc + _dot(mo, w_ref[0, d:])


def _mixout(h, tok, xq, kv, w, layer):
    batch, seq, d = h.shape
    tm = ROW_TILE
    return pl.pallas_call(
        _mixout_kernel,
        out_shape=jax.ShapeDtypeStruct((batch, seq, d), F32),
        grid=(batch, seq // tm),
        in_specs=[
            pl.BlockSpec((1, tm, d), lambda b, i: (b, i, 0)),
            pl.BlockSpec((1, tm, d), lambda b, i: (b, i, 0)),
            pl.BlockSpec((1, tm, XATTN_WIDTH), lambda b, i: (b, i, 0)),
            pl.BlockSpec((1, 1, N_MEM, 2 * XATTN_WIDTH), lambda b, i: (layer, b, 0, 0)),
            pl.BlockSpec((1, d + XATTN_WIDTH, d), lambda b, i: (layer, 0, 0)),
        ],
        out_specs=pl.BlockSpec((1, tm, d), lambda b, i: (b, i, 0)),
        compiler_params=pltpu.CompilerParams(
            dimension_semantics=("arbitrary", "arbitrary"),
            vmem_limit_bytes=_vmem_limit(40 * 1024 * 1024)),
        name="mixout",
    )(h, tok, xq, kv, w)


def _ffn_kernel(h_ref, g_ref, w1_ref, w2_ref, gf_ref, o_ref, *, final):
    h = h_ref[0]
    hn = _rms(h, g_ref[...]).astype(BF16)
    acc = h
    chunks = [c * FFN_CHUNK for c in range(FFN_HIDDEN // FFN_CHUNK)]
    gate_up = [(_dot(hn, w1_ref[0, :, lo:lo + FFN_CHUNK]),
                _dot(hn, w1_ref[0, :, FFN_HIDDEN + lo:FFN_HIDDEN + lo + FFN_CHUNK]))
               for lo in chunks[:1]]
    for c, lo in enumerate(chunks):
        if c + 1 < len(chunks):
            nxt = chunks[c + 1]
            gate_up.append((_dot(hn, w1_ref[0, :, nxt:nxt + FFN_CHUNK]),
                            _dot(hn, w1_ref[0, :, FFN_HIDDEN + nxt:FFN_HIDDEN + nxt + FFN_CHUNK])))
        a, u = gate_up[c]
        act = (a * _sigmoid(a) * u).astype(BF16)
        acc = acc + _dot(act, w2_ref[0, lo:lo + FFN_CHUNK])
    if final:
        acc = _rms(acc, gf_ref[...])
    o_ref[0] = acc


def _ffn(h, g, w1, w2, gf, layer, final):
    batch, seq, d = h.shape
    tm = FFN_ROW_TILE
    return pl.pallas_call(
        functools.partial(_ffn_kernel, final=final),
        out_shape=jax.ShapeDtypeStruct((batch, seq, d), F32),
        grid=(batch, seq // tm),
        in_specs=[
            pl.BlockSpec((1, tm, d), lambda b, i: (b, i, 0)),
            pl.BlockSpec((1, d), lambda b, i: (0, 0)),
            pl.BlockSpec((1, d, 2 * FFN_HIDDEN), lambda b, i: (layer, 0, 0)),
            pl.BlockSpec((1, FFN_HIDDEN, d), lambda b, i: (layer, 0, 0)),
            pl.BlockSpec((1, d), lambda b, i: (0, 0)),
        ],
        out_specs=pl.BlockSpec((1, tm, d), lambda b, i: (b, i, 0)),
        compiler_params=pltpu.CompilerParams(
            dimension_semantics=("arbitrary", "arbitrary"),
            vmem_limit_bytes=_vmem_limit(52 * 1024 * 1024)),
        name="ffn",
    )(h, g.reshape(1, d), w1, w2, gf.reshape(1, d))


_QW = NSA_HEADS * NSA_HEAD_DIM
_KVW = NSA_KV_GROUPS * NSA_HEAD_DIM
_PAIR = 2 * NSA_HEAD_DIM
_C_Q = 0
_C_CMP = _C_Q + _QW
_C_SLC = _C_CMP + 2 * _KVW
_C_WIN = _C_SLC + NSA_KV_GROUPS * _PAIR
_C_GATE = _C_WIN + NSA_KV_GROUPS * _PAIR
_C_XQ = _C_GATE + V7X_LANES
_C_END = _C_XQ + XATTN_WIDTH


_SUM_ROWS = 16
_VT_ROWS = _SUM_ROWS + NSA_HEAD_DIM


def _ones_and_vt(kv):
    v_t = kv.T[NSA_HEAD_DIM:]
    return jnp.concatenate([jnp.ones((_SUM_ROWS, kv.shape[0]), F32), v_t], axis=0).astype(BF16)


def _nsa_proj_kernel(h_ref, g_ref, w_ref, q_ref, cr_ref, slc_ref, slc1_ref, win_ref, win1_ref,
                     gt_ref, xq_ref):
    hn = _rms(h_ref[0], g_ref[...]).astype(BF16)
    q = _dot(hn, w_ref[:, _C_Q:_C_CMP]) * (NSA_HEAD_DIM ** -0.5 * LOG2E)
    gates = _sigmoid(_dot(hn, w_ref[:, _C_GATE:_C_XQ]))
    s = _dot(hn, w_ref[:, _C_SLC:_C_WIN])
    for c in range(ROW_TILE // Q_TILE):
        rows = slice(c * Q_TILE, (c + 1) * Q_TILE)
        q_ref[0, c] = q[rows].T.astype(BF16)
        gt_ref[0, c] = gates[rows].T
    w = _dot(hn, w_ref[:, _C_WIN:_C_GATE])
    cr = _dot(hn, w_ref[:, _C_CMP:_C_SLC])
    for p in range(2 * _KVW // V7X_LANES):
        cr_ref[0, p] = cr[:, p * V7X_LANES:(p + 1) * V7X_LANES]
    xq_ref[0] = _dot(hn, w_ref[:, _C_XQ:_C_END]).astype(BF16)
    k_lanes = lax.broadcasted_iota(jnp.int32, (s.shape[0], _PAIR), 1) < NSA_HEAD_DIM
    for g in range(NSA_KV_GROUPS):
        sg = s[:, g * _PAIR:(g + 1) * _PAIR]
        wg = w[:, g * _PAIR:(g + 1) * _PAIR]
        slc_ref[0, g] = jnp.where(k_lanes, sg, 0.0).astype(BF16)
        win_ref[0, g] = jnp.where(k_lanes, wg, 0.0).astype(BF16)
        for c in range(ROW_TILE // KV_TILE):
            rows = slice(c * KV_TILE, (c + 1) * KV_TILE)
            slc1_ref[0, g, c] = _ones_and_vt(sg[rows])
            win1_ref[0, g, c] = _ones_and_vt(wg[rows])


def _nsa_proj(h, g, w):
    batch, seq, d = h.shape
    tm = ROW_TILE
    grp = NSA_KV_GROUPS
    row = lambda b, i: (b, i, 0)
    pair_shape = jax.ShapeDtypeStruct((batch, grp, seq, _PAIR), BF16)
    pair_spec = pl.BlockSpec((1, grp, tm, _PAIR), lambda b, i: (b, 0, i, 0))
    vt_shape = jax.ShapeDtypeStruct((batch, grp, seq // KV_TILE, _VT_ROWS, KV_TILE), BF16)
    vt_spec = pl.BlockSpec((1, grp, tm // KV_TILE, _VT_ROWS, KV_TILE), lambda b, i: (b, 0, i, 0, 0))
    return pl.pallas_call(
        _nsa_proj_kernel,
        out_shape=(jax.ShapeDtypeStruct((batch, seq // Q_TILE, _QW, Q_TILE), BF16),
                   jax.ShapeDtypeStruct((batch, 2 * _KVW // V7X_LANES, seq, V7X_LANES), F32),
                   pair_shape, vt_shape, pair_shape, vt_shape,
                   jax.ShapeDtypeStruct((batch, seq // Q_TILE, V7X_LANES, Q_TILE), F32),
                   jax.ShapeDtypeStruct((batch, seq, XATTN_WIDTH), BF16)),
        grid=(batch, seq // tm),
        in_specs=[
            pl.BlockSpec((1, tm, d), row),
            pl.BlockSpec((1, d), lambda b, i: (0, 0)),
            pl.BlockSpec((d, _C_END), lambda b, i: (0, 0)),
        ],
        out_specs=(pl.BlockSpec((1, tm // Q_TILE, _QW, Q_TILE), lambda b, i: (b, i, 0, 0)),
                   pl.BlockSpec((1, 2 * _KVW // V7X_LANES, tm, V7X_LANES), lambda b, i: (b, 0, i, 0)),
                   pair_spec, vt_spec, pair_spec, vt_spec,
                   pl.BlockSpec((1, tm // Q_TILE, V7X_LANES, Q_TILE), lambda b, i: (b, i, 0, 0)),
                   pl.BlockSpec((1, tm, XATTN_WIDTH), row)),
        compiler_params=pltpu.CompilerParams(
            dimension_semantics=("arbitrary", "arbitrary"),
            vmem_limit_bytes=_vmem_limit(48 * 1024 * 1024)),
        name="nsa_proj",
    )(h, g.reshape(1, d), w)


_CHUNKS = 512


def _compress_kernel(raw_ref, pos_ref, w1_ref, w2_ref, o_ref, h2_scr, *, transposed):
    dk = NSA_HEAD_DIM
    half = CMP_BLOCK // 2
    h1 = [jnp.zeros((_CHUNKS, CMP_HIDDEN), F32) for _ in range(NSA_KV_GROUPS)]
    h2 = [jnp.zeros((_CHUNKS, CMP_HIDDEN), F32) for _ in range(NSA_KV_GROUPS)]
    gpl = V7X_LANES // dk
    for l in range(half):
        for p in range(NSA_KV_GROUPS // gpl):
            lanes = slice(p * V7X_LANES, (p + 1) * V7X_LANES)
            rows = raw_ref[0, p, pl.ds(l, _CHUNKS, stride=CMP_STRIDE), :]
            a1 = (rows + pos_ref[0, l:l + 1, lanes]).astype(BF16)
            a2 = (rows + pos_ref[0, half + l:half + l + 1, lanes]).astype(BF16)
            for gg in range(gpl):
                g = p * gpl + gg
                h1[g] = h1[g] + _dot(a1[:, gg * dk:(gg + 1) * dk], w1_ref[0, l * dk:(l + 1) * dk, :])
                h2[g] = h2[g] + _dot(a2[:, gg * dk:(gg + 1) * dk],
                                     w1_ref[0, (half + l) * dk:(half + l + 1) * dk, :])
    h2_scr[_CHUNKS:_CHUNKS + 8, :] = jnp.zeros((8, CMP_HIDDEN), F32)
    for g in range(NSA_KV_GROUPS):
        h2_scr[0:_CHUNKS, :] = h2[g]
        hm = h1[g] + h2_scr[pl.ds(1, _CHUNKS), :]
        hm = (hm * _sigmoid(hm)).astype(BF16)
        if transposed:
            o_ref[0, g] = _dot_nt(w2_ref[0], hm).astype(BF16)
        else:
            out = _dot(hm, w2_ref[0])
            o_ref[0, g] = jnp.concatenate([out, jnp.zeros_like(out)], axis=1).astype(BF16)


def _compress(raw, pos4, w1, w2, which, transposed):
    batch, _, seq, _ = raw.shape
    grp = NSA_KV_GROUPS
    tiles = _KVW // V7X_LANES
    if transposed:
        out_shape = (batch, grp, NSA_HEAD_DIM, N_CMP_PAD)
        w2_block = (1, NSA_HEAD_DIM, CMP_HIDDEN)
    else:
        out_shape = (batch, grp, N_CMP_PAD, _PAIR)
        w2_block = (1, CMP_HIDDEN, NSA_HEAD_DIM)
    return pl.pallas_call(
        functools.partial(_compress_kernel, transposed=transposed),
        out_shape=jax.ShapeDtypeStruct(out_shape, BF16),
        grid=(batch,),
        in_specs=[
            pl.BlockSpec((1, tiles, seq, V7X_LANES), lambda b: (b, which, 0, 0)),
            pl.BlockSpec((1, CMP_BLOCK, _KVW), lambda b: (which, 0, 0)),
            pl.BlockSpec((1, CMP_BLOCK * NSA_HEAD_DIM, CMP_HIDDEN), lambda b: (which, 0, 0)),
            pl.BlockSpec(w2_block, lambda b: (0, 0, 0)),
        ],
        out_specs=pl.BlockSpec((1,) + out_shape[1:], lambda b: (b, 0, 0, 0)),
        scratch_shapes=[pltpu.VMEM((_CHUNKS + 8, CMP_HIDDEN), F32)],
        compiler_params=pltpu.CompilerParams(
            dimension_semantics=("arbitrary",),
            vmem_limit_bytes=_vmem_limit(40 * 1024 * 1024)),
        name="compress_v" if transposed else "compress_k",
    )(raw, pos4, w1, w2)


def _t5_bucket(dist):
    n = jnp.maximum(dist, 0)
    max_exact = REL_BUCKETS // 2
    nf = jnp.maximum(n, 1).astype(F32)
    large = max_exact + (jnp.log(nf / max_exact) / np.log(REL_MAX_DIST / max_exact)
                         * (REL_BUCKETS - max_exact)).astype(jnp.int32)
    large = jnp.minimum(large, REL_BUCKETS - 1)
    return jnp.where(n < max_exact, n, large)


def _table_kernel(thr_ref, rb_ref, cbt_ref, tslc_ref, twin_ref):
    tq = Q_TILE
    g = pl.program_id(0)

    def entries(dist, h, limit):
        val = jnp.full(dist.shape, rb_ref[0, h], F32)
        for k in range(1, REL_BUCKETS):
            val = jnp.where(dist >= thr_ref[k], rb_ref[k, h], val)
        val = (val - rb_ref[REL_BUCKETS - 1, h]) * LOG2E
        return jnp.where(jnp.where(dist >= 0, dist, limit) < limit, val, NEG)

    def far_entries(dist, limit):
        return jnp.where(jnp.where(dist >= 0, dist, limit) < limit, 0.0, NEG)

    def fill(ref, band, dist_of_rows, n_rows, limit, h):
        step = 64
        for r0 in range(0, n_rows, step):
            shape = (step, tq)
            dist = dist_of_rows(r0 + lax.broadcasted_iota(jnp.int32, shape, 0),
                                lax.broadcasted_iota(jnp.int32, shape, 1))
            lo, hi = dist_of_rows(r0 + step - 1, 0), dist_of_rows(r0, tq - 1)
            near = hi >= 0 and lo < REL_MAX_DIST
            ref[0, r0:r0 + step, band] = entries(dist, h, limit) if near else far_entries(dist, limit)

    no_limit = 1 << 30
    for j in range(NSA_HPG):
        h = g * NSA_HPG + j
        band = slice(j * tq, (j + 1) * tq)
        fill(cbt_ref, band, lambda u, q: q - (CMP_BLOCK - 1) - CMP_STRIDE * (u - N_CMP_PAD),
             2 * N_CMP_PAD, no_limit, h)
        fill(tslc_ref, band, lambda u, q: q - u + tq, 2 * tq, no_limit, h)
        fill(twin_ref, band, lambda u, q: q - u + WINDOW, WINDOW + tq, WINDOW, h)


def _bias_tables(rel_bias):
    tq = Q_TILE
    rows = NSA_HPG * tq
    buckets = _t5_bucket(jnp.arange(REL_MAX_DIST + 1))
    thr = jnp.sum(buckets[None, :] < jnp.arange(REL_BUCKETS)[:, None], axis=1).astype(jnp.int32)
    smem = pl.BlockSpec(memory_space=pltpu.SMEM)
    return pl.pallas_call(
        _table_kernel,
        out_shape=(jax.ShapeDtypeStruct((NSA_KV_GROUPS, 2 * N_CMP_PAD, rows), F32),
                   jax.ShapeDtypeStruct((NSA_KV_GROUPS, 2 * tq, rows), F32),
                   jax.ShapeDtypeStruct((NSA_KV_GROUPS, WINDOW + tq, rows), F32)),
        grid=(NSA_KV_GROUPS,),
        in_specs=[smem, smem],
        out_specs=(pl.BlockSpec((1, 2 * N_CMP_PAD, rows), lambda g: (g, 0, 0)),
                   pl.BlockSpec((1, 2 * tq, rows), lambda g: (g, 0, 0)),
                   pl.BlockSpec((1, WINDOW + tq, rows), lambda g: (g, 0, 0))),
        compiler_params=pltpu.CompilerParams(
            dimension_semantics=("arbitrary",),
            vmem_limit_bytes=_vmem_limit(40 * 1024 * 1024)),
        name="bias_tables",
    )(thr, rel_bias.astype(F32))


def _block_mask_rows(seq):
    pos = jnp.arange(seq, dtype=jnp.int32)[:, None]
    col = jnp.arange(N_SEL, dtype=jnp.int32)[None, :]
    onehot = jnp.where(col == pos // SEL_BLOCK, NEG, 0.0)
    return jnp.concatenate([onehot, jnp.full((Q_TILE, N_SEL), NEG)], axis=0).astype(BF16)


def _online_absorb(s, v1t, state):
    m = jnp.max(s, axis=0, keepdims=True)
    if state is not None:
        m = jnp.maximum(state[0], m)
    acc = _dot(v1t, jnp.exp2(s - m).astype(BF16))
    if state is not None:
        acc = jnp.exp2(state[0] - m) * state[1] + acc
    return m, acc


def _normalise_t(acc):
    return acc[_SUM_ROWS:] * (1.0 / acc[0:1])


def _nsa_kernel(q_ref, kc_ref, vct_ref, slc_ref, slc1_ref, win_ref, win1_ref, bm_ref, gt_ref,
                cbt_ref, tslc_ref, twin_ref, o_ref, ps_scr, lhs_scr, m_scr, acc_scr, s_scr, ocmp_scr,
                owin_scr, smax_scr):
    tq = Q_TILE
    hpg = NSA_HPG
    dk = NSA_HEAD_DIM
    seq = slc_ref.shape[2]
    g = pl.program_id(1)
    i = pl.program_id(2)
    q0 = pl.multiple_of(i * tq, tq)
    bands = [slice(j * tq, (j + 1) * tq) for j in range(hpg)]

    for j in range(hpg):
        lhs_scr[0:dk, bands[j]] = q_ref[0, 0, j * dk:(j + 1) * dk, :]
    lhs_scr[dk:dk + _SUM_ROWS, :] = jnp.ones((_SUM_ROWS, hpg * tq), BF16)
    lhs_scr[dk + _SUM_ROWS:_PAIR, :] = jnp.zeros((_PAIR - dk - _SUM_ROWS, hpg * tq), BF16)
    qs_t = lhs_scr[0:_PAIR, :]

    n_prev = WINDOW // tq
    mask_lane = lax.broadcasted_iota(jnp.int32, (1, _PAIR), 1) == dk

    def win_logits(r):
        tile = jnp.maximum(i - r, 0)
        keys = win_ref[0, 0, pl.ds(pl.multiple_of(tile * tq, tq), tq), :]
        if r:
            keys = keys + jnp.where(jnp.logical_and(mask_lane, i < r), NEG, 0.0).astype(BF16)
        rows = slice((n_prev - r) * tq, (n_prev - r + 1) * tq)
        return _dot(keys, qs_t) + twin_ref[0, rows, :]

    def win_absorb(r, s, state):
        return _online_absorb(s, win1_ref[0, 0, jnp.maximum(i - r, 0)], state)

    def compress_and_select(n_cmp):
        n_sel = n_cmp * CMP_STRIDE // SEL_BLOCK
        st = _dot(kc_ref[0, 0, 0:n_cmp, :], qs_t)
        st = st + cbt_ref[0, pl.ds(pl.multiple_of(N_CMP_PAD - (tq // CMP_STRIDE) * i, 8), n_cmp), :]
        mx = jnp.maximum(jnp.max(st, axis=0, keepdims=True), 0.1 * NEG)
        e = jnp.exp2(st - mx)
        den = jnp.sum(e, axis=0, keepdims=True)
        inv = jnp.where(den > 0.0, 1.0 / den, 0.0)
        ocmp_scr[...] = _dot(vct_ref[0, 0, :, 0:n_cmp], e.astype(BF16)) * inv
        psum = e[:, bands[0]] * inv[:, bands[0]]
        for j in range(1, hpg):
            psum = psum + e[:, bands[j]] * inv[:, bands[j]]

        ratio = SEL_BLOCK // CMP_STRIDE
        imp_tiles = []
        for c in range(tq // V7X_LANES):
            ps_scr[c, 0:8, :] = jnp.zeros((8, V7X_LANES), F32)
            ps_scr[c, 8:8 + n_cmp, :] = psum[:, c * V7X_LANES:(c + 1) * V7X_LANES]
            taps = [ps_scr[c, pl.ds(7 + r, n_sel, stride=ratio), :] for r in range(5)]
            imp_tiles.append(0.5 * taps[0] + taps[1] + taps[2] + taps[3] + 0.5 * taps[4])
        imp = jnp.concatenate(imp_tiles, axis=1)

        jrow = lax.broadcasted_iota(jnp.int32, (n_sel, tq), 0)
        cur = (q0 + lax.broadcasted_iota(jnp.int32, (n_sel, tq), 1)) // SEL_BLOCK
        taken = -3.0e38
        imp = jnp.where(jrow <= cur, imp, -SEL_FORCE)
        for forced_block in (0, cur, cur - 1):
            imp = jnp.where(jrow == forced_block, taken, imp)
        n_rounds = SEL_TOPK - 3
        n_tiles = n_prev + 1
        s_win = [win_logits(0), win_logits(1)]
        win_state, absorbed = None, 0
        for rnd in range(n_rounds):
            top = jnp.max(imp, axis=0, keepdims=True)
            first = jnp.min(jnp.where(imp == top, jrow, n_sel), axis=0, keepdims=True)
            imp = jnp.where(jrow == first, taken, imp)
            while absorbed < (rnd + 1) * n_tiles // n_rounds:
                if absorbed + 2 < n_tiles:
                    s_win.append(win_logits(absorbed + 2))
                win_state = win_absorb(absorbed, s_win[absorbed], win_state)
                absorbed += 1
        owin_scr[...] = _normalise_t(win_state[1])
        not_sel = jnp.where(imp < 0.5 * taken, 0.0, 1.0).astype(BF16)
        for j in range(hpg):
            lhs_scr[_PAIR:_PAIR + n_sel, bands[j]] = not_sel
        if n_sel < N_SEL:
            lhs_scr[_PAIR + n_sel:2 * _PAIR, :] = jnp.ones((N_SEL - n_sel, hpg * tq), BF16)

    tiles_per_variant = (seq // tq) // CMP_VARIANTS
    for v in range(CMP_VARIANTS):
        pl.when(i // tiles_per_variant == v)(
            functools.partial(compress_and_select, N_CMP_PAD * (v + 1) // CMP_VARIANTS))
    lhs = lhs_scr[...]

    def key_tile(kv_ref, tile, mask_tile):
        return jnp.concatenate([kv_ref[0, 0, pl.ds(pl.multiple_of(tile * tq, tq), tq), :],
                                bm_ref[pl.ds(pl.multiple_of(mask_tile * tq, tq), tq), :]], axis=1)

    all_masked = seq // tq

    prev = jnp.maximum(i - 1, 0)
    s_own = _dot(key_tile(slc_ref, i, i), lhs) + tslc_ref[0, tq:2 * tq, :]
    s_prev = (_dot(key_tile(slc_ref, prev, jnp.where(i == 0, all_masked, i - 1)), lhs)
              + tslc_ref[0, 0:tq, :])

    n_far = jnp.maximum(i - 1, 0)
    n_groups = (n_far + FAR_GROUP - 1) // FAR_GROUP

    def tile_or_none(kt):
        valid = kt < n_far
        return jnp.where(valid, kt, 0), jnp.where(valid, kt, all_masked)

    def stage_logits(slot, kg):
        tiles = [key_tile(slc_ref, *tile_or_none(FAR_GROUP * kg + t)) for t in range(FAR_GROUP)]
        s = _dot(jnp.concatenate(tiles, axis=0), lhs_scr[...])
        s_scr[slot] = s
        smax_scr[8 * slot:8 * slot + 1, :] = jnp.max(s, axis=0, keepdims=True)

    def absorb(slot, kg):
        m_old = m_scr[0:1, :]
        m_new = jnp.maximum(m_old, smax_scr[8 * slot:8 * slot + 1, :])
        p = jnp.exp2(s_scr[slot] - m_new)
        v1t = jnp.concatenate([slc1_ref[0, 0, tile_or_none(FAR_GROUP * kg + t)[0]]
                               for t in range(FAR_GROUP)], axis=1)
        acc_scr[...] = jnp.exp2(m_old - m_new) * acc_scr[...] + _dot(v1t, p.astype(BF16))
        m_scr[0:1, :] = m_new

    def far_trip(base, groups):
        for t in range(groups):
            stage_logits((t + 1) % 2, base + t + 1)
            absorb(t % 2, base + t)

    stage_logits(0, 0)
    state = _online_absorb(s_own, slc1_ref[0, 0, i], None)
    m, acc = _online_absorb(s_prev, slc1_ref[0, 0, prev], state)
    m_scr[0:1, :] = m
    acc_scr[...] = acc
    n_long = n_groups // 4

    def long_body(k, carry):
        far_trip(4 * k, 4)
        return carry

    lax.fori_loop(0, n_long, long_body, 0)
    done = 4 * n_long
    pair = (n_groups - done) // 2

    @pl.when(pair > 0)
    def _():
        far_trip(done, 2)

    @pl.when(n_groups - done - 2 * pair > 0)
    def _():
        far_trip(done + 2 * pair, 1)

    o_slc_t = _normalise_t(acc_scr[...])

    merged = []
    for j in range(hpg):
        h = g * hpg + j
        gate = [gt_ref[0, 0, pl.ds(br * NSA_HEADS + h, 1), :] for br in range(3)]
        merged.append(gate[0] * ocmp_scr[:, bands[j]] + gate[1] * o_slc_t[:, bands[j]]
                      + gate[2] * owin_scr[:, bands[j]])
    o_ref[0] = jnp.concatenate(merged, axis=0).T.astype(BF16)


def _nsa_attention(q, kc, vct, slc, slc1, win, win1, block_mask, gates, cbt, tslc, twin):
    batch, _, seq, _ = slc.shape
    grp = NSA_KV_GROUPS
    tq = Q_TILE
    rows = NSA_HPG * tq
    per_group = lambda b, g, i: (b, g, 0, 0)
    per_group_t = lambda b, g, i: (b, g, 0, 0, 0)
    return pl.pallas_call(
        _nsa_kernel,
        out_shape=jax.ShapeDtypeStruct((batch, seq, _QW), BF16),
        grid=(batch, grp, seq // tq),
        in_specs=[
            pl.BlockSpec((1, 1, NSA_HPG * NSA_HEAD_DIM, tq), lambda b, g, i: (b, i, g, 0)),
            pl.BlockSpec((1, 1, N_CMP_PAD, _PAIR), per_group),
            pl.BlockSpec((1, 1, NSA_HEAD_DIM, N_CMP_PAD), per_group),
            pl.BlockSpec((1, 1, seq, _PAIR), per_group),
            pl.BlockSpec((1, 1, seq // tq, _VT_ROWS, tq), per_group_t),
            pl.BlockSpec((1, 1, seq, _PAIR), per_group),
            pl.BlockSpec((1, 1, seq // tq, _VT_ROWS, tq), per_group_t),
            _resident((seq + tq, N_SEL), lambda b, g, i: (0, 0)),
            pl.BlockSpec((1, 1, V7X_LANES, tq), lambda b, g, i: (b, i, 0, 0)),
            pl.BlockSpec((1, 2 * N_CMP_PAD, rows), lambda b, g, i: (g, 0, 0)),
            pl.BlockSpec((1, 2 * tq, rows), lambda b, g, i: (g, 0, 0)),
            pl.BlockSpec((1, WINDOW + tq, rows), lambda b, g, i: (g, 0, 0)),
        ],
        out_specs=pl.BlockSpec((1, tq, NSA_HPG * NSA_HEAD_DIM), lambda b, g, i: (b, i, g)),
        scratch_shapes=[pltpu.VMEM((tq // V7X_LANES, N_CMP_PAD + 16, V7X_LANES), F32),
                        pltpu.VMEM((2 * _PAIR, rows), BF16),
                        pltpu.VMEM((8, rows), F32),
                        pltpu.VMEM((_VT_ROWS, rows), F32),
                        pltpu.VMEM((2, FAR_GROUP * tq, rows), F32),
                        pltpu.VMEM((NSA_HEAD_DIM, rows), F32),
                        pltpu.VMEM((NSA_HEAD_DIM, rows), F32),
                        pltpu.VMEM((16, rows), F32)],
        compiler_params=pltpu.CompilerParams(
            dimension_semantics=("arbitrary", "arbitrary", "arbitrary"),
            vmem_limit_bytes=_vmem_limit(52 * 1024 * 1024)),
        name="nsa_attention",
    )(q, kc, vct, slc, slc1, win, win1, block_mask, gates, cbt, tslc, twin)


def _nsa_weight(w):
    d = w.shape[0]
    q = w[:, :_QW]
    kv = w[:, _QW:_QW + 6 * _KVW].reshape(d, 6, NSA_KV_GROUPS, NSA_HEAD_DIM)
    cmp_raw = kv[:, 0:2].reshape(d, 2 * _KVW)
    slc = jnp.stack([kv[:, 2], kv[:, 3]], axis=2).reshape(d, NSA_KV_GROUPS * _PAIR)
    win = jnp.stack([kv[:, 4], kv[:, 5]], axis=2).reshape(d, NSA_KV_GROUPS * _PAIR)
    gates = w[:, _QW + 6 * _KVW:_QW + 6 * _KVW + 3 * NSA_HEADS]
    gates = jnp.pad(gates, ((0, 0), (0, V7X_LANES - 3 * NSA_HEADS)))
    xq = w[:, _QW + 6 * _KVW + 3 * NSA_HEADS:]
    return jnp.concatenate([q, cmp_raw, slc, win, gates, xq], axis=1).astype(BF16)


def kernel(x, mem, norm_mix_g, norm_mem_g, norm_ffn_g, norm_final_g, w_in_conv, conv_w, w_in_nsa,
           cmp_pos_k, cmp_w1_k, cmp_w2_k, cmp_pos_v, cmp_w1_v, cmp_w2_v, rel_bias, w_mem_kv, w_out,
           w_ffn_in, w_ffn_out):
    seq = x.shape[1]
    assert seq % ROW_TILE == 0 and seq % Q_TILE == 0 and x.shape[2] == D_MODEL
    assert seq == CMP_STRIDE * _CHUNKS and seq == SEL_BLOCK * N_SEL and Q_TILE == KV_TILE
    memkv = _memkv(mem, norm_mem_g, w_mem_kv.astype(BF16))
    w_out_b = w_out.astype(BF16)
    w_ffn_in_b = w_ffn_in.astype(BF16)
    w_ffn_out_b = w_ffn_out.astype(BF16)

    tok, xq = _conv_front(x, norm_mix_g[0], w_in_conv[0].astype(BF16), conv_w[0])
    h = _mixout(x, tok, xq, memkv, w_out_b, 0)
    h = _ffn(h, norm_ffn_g[0], w_ffn_in_b, w_ffn_out_b, norm_final_g, 0, final=False)

    q, cmp_raw, slc, slc1, win, win1, gates, xq = _nsa_proj(h, norm_mix_g[1], _nsa_weight(w_in_nsa[0]))
    pos4 = jnp.stack([jnp.tile(cmp_pos_k[0], (1, NSA_KV_GROUPS)),
                      jnp.tile(cmp_pos_v[0], (1, NSA_KV_GROUPS))])
    w1 = jnp.stack([cmp_w1_k[0], cmp_w1_v[0]]).astype(BF16)
    kc = _compress(cmp_raw, pos4, w1, cmp_w2_k[0].astype(BF16)[None], 0, transposed=False)
    vct = _compress(cmp_raw, pos4, w1, cmp_w2_v[0].T.astype(BF16)[None], 1, transposed=True)
    tok = _nsa_attention(q, kc, vct, slc, slc1, win, win1, _block_mask_rows(seq), gates,
                         *_bias_tables(rel_bias))
    h = _mixout(h, tok, xq, memkv, w_out_b, 1)
    return _ffn(h, norm_ffn_g[1], w_ffn_in_b, w_ffn_out_b, norm_final_g, 1, final=True)
```

```python
import functools

import numpy as np
import jax
import jax.numpy as jnp
from jax import lax
from jax.experimental import pallas as pl
from jax.experimental.pallas import tpu as pltpu

F32 = jnp.float32
BF16 = jnp.bfloat16

D_MODEL = 1024
N_MEM = 256
CONV_WIDTH = 3
NSA_HEADS = 16
NSA_HEAD_DIM = 64
NSA_KV_GROUPS = 4
NSA_HPG = NSA_HEADS // NSA_KV_GROUPS
CMP_BLOCK = 32
CMP_STRIDE = 16
CMP_HIDDEN = 2 * NSA_HEAD_DIM
SEL_BLOCK = 64
SEL_TOPK = 16
SEL_FORCE = 1.0e4
WINDOW = 512
XATTN_HEADS = 4
XATTN_HEAD_DIM = 128
XATTN_WIDTH = XATTN_HEADS * XATTN_HEAD_DIM
FFN_HIDDEN = 2816
REL_BUCKETS = 32
REL_MAX_DIST = 128
RMS_EPS = 1e-6
NEG = -1e30
LOG2E = 1.4426950408889634

V7X_LANES = 128
V7X_VMEM_BYTES = 64 * 1024 * 1024

ROW_TILE = 512
CONV_COLS = 256
FFN_ROW_TILE = 512
FFN_CHUNK = 256
Q_TILE = 256
KV_TILE = 256
FAR_GROUP = 2
CMP_VARIANTS = 8
N_CMP_PAD = 512
N_SEL = 128

_NT = (((1,), (1,)), ((), ()))


def _vmem_limit(nbytes):
    return int(min(max(nbytes, 16 * 1024 * 1024), V7X_VMEM_BYTES - 8 * 1024 * 1024))


def _rms(x, g):
    ms = jnp.mean(x * x, axis=-1, keepdims=True)
    return x * lax.rsqrt(ms + RMS_EPS) * g


def _dot(a, b):
    return jnp.dot(a, b, preferred_element_type=F32)


def _dot_nt(a, b):
    return lax.dot_general(a, b, _NT, preferred_element_type=F32)


def _sigmoid(x):
    return 1.0 / (1.0 + jnp.exp(-x))


def _resident(block_shape, index_map):
    return pl.BlockSpec(block_shape, index_map, pipeline_mode=pl.Buffered(1))


def _memkv_kernel(mem_ref, g_ref, w_ref, o_ref):
    xn = _rms(mem_ref[0], g_ref[0]).astype(BF16)
    o_ref[0, 0] = _dot(xn, w_ref[0]).astype(BF16)


def _memkv(mem, gains, w):
    depth, batch = w.shape[0], mem.shape[0]
    return pl.pallas_call(
        _memkv_kernel,
        out_shape=jax.ShapeDtypeStruct((depth, batch, N_MEM, 2 * XATTN_WIDTH), BF16),
        grid=(depth, batch),
        in_specs=[
            pl.BlockSpec((1, N_MEM, D_MODEL), lambda l, b: (b, 0, 0)),
            pl.BlockSpec((1, 1, D_MODEL), lambda l, b: (l, 0, 0)),
            pl.BlockSpec((1, D_MODEL, 2 * XATTN_WIDTH), lambda l, b: (l, 0, 0)),
        ],
        out_specs=pl.BlockSpec((1, 1, N_MEM, 2 * XATTN_WIDTH), lambda l, b: (l, b, 0, 0)),
        compiler_params=pltpu.CompilerParams(
            dimension_semantics=("arbitrary", "arbitrary"),
            vmem_limit_bytes=_vmem_limit(24 * 1024 * 1024)),
        name="memkv",
    )(mem, gains.reshape(depth, 1, D_MODEL), w)


def _conv_kernel(h_ref, hp_ref, g_ref, w_ref, cw_ref, tok_ref, xq_ref):
    d = D_MODEL
    i = pl.program_id(1)
    g = g_ref[...]
    hn = _rms(h_ref[0], g).astype(BF16)
    hpn = _rms(hp_ref[0], g).astype(BF16)
    cw = cw_ref[...]
    row = lax.broadcasted_iota(jnp.int32, (h_ref.shape[1], CONV_COLS), 0)

    def projections(c):
        lo = c * CONV_COLS
        col = lambda part: w_ref[:, part * d + lo:part * d + lo + CONV_COLS]
        vp = jnp.where(i > 0, _dot(hpn, col(1)) * _dot(hpn, col(2)), 0.0)
        return _dot(hn, col(0)), _dot(hn, col(1)) * _dot(hn, col(2)), vp

    n_chunks = d // CONV_COLS
    ready = projections(0)
    for c in range(n_chunks):
        u_b, v, vp = ready
        if c + 1 < n_chunks:
            ready = projections(c + 1)
        else:
            xq_ref[0] = _dot(hn, w_ref[:, 3 * d:]).astype(BF16)
        cols = slice(c * CONV_COLS, (c + 1) * CONV_COLS)
        v1 = jnp.where(row == 0, vp[7:8], pltpu.roll(v, 1, axis=0))
        v2 = jnp.where(row == 0, vp[6:7], jnp.where(row == 1, vp[7:8], pltpu.roll(v, 2, axis=0)))
        conv = v2 * cw[0:1, cols] + v1 * cw[1:2, cols] + v * cw[2:3, cols]
        tok_ref[0, :, cols] = (u_b * conv).astype(BF16)


def _conv_front(h, g, w, cw):
    batch, seq, d = h.shape
    tm = ROW_TILE
    width = w.shape[1]
    halo = 8
    return pl.pallas_call(
        _conv_kernel,
        out_shape=(jax.ShapeDtypeStruct((batch, seq, d), BF16),
                   jax.ShapeDtypeStruct((batch, seq, XATTN_WIDTH), BF16)),
        grid=(batch, seq // tm),
        in_specs=[
            pl.BlockSpec((1, tm, d), lambda b, i: (b, i, 0)),
            pl.BlockSpec((1, halo, d), lambda b, i: (b, jnp.maximum(i * (tm // halo) - 1, 0), 0)),
            pl.BlockSpec((1, d), lambda b, i: (0, 0)),
            pl.BlockSpec((d, width), lambda b, i: (0, 0)),
            pl.BlockSpec((CONV_WIDTH, d), lambda b, i: (0, 0)),
        ],
        out_specs=(pl.BlockSpec((1, tm, d), lambda b, i: (b, i, 0)),
                   pl.BlockSpec((1, tm, XATTN_WIDTH), lambda b, i: (b, i, 0))),
        compiler_params=pltpu.CompilerParams(
            dimension_semantics=("arbitrary", "arbitrary"),
            vmem_limit_bytes=_vmem_limit(48 * 1024 * 1024)),
        name="conv_front",
    )(h, h, g.reshape(1, d), w, cw)


def _mixout_kernel(h_ref, tok_ref, xq_ref, kv_ref, w_ref, o_ref):
    d = D_MODEL
    xq = xq_ref[0]
    kv = kv_ref[0, 0]
    scale = XATTN_HEAD_DIM ** -0.5 * LOG2E
    heads = [slice(hh * XATTN_HEAD_DIM, (hh + 1) * XATTN_HEAD_DIM) for hh in range(XATTN_HEADS)]
    logits = [_dot_nt(xq[:, hd], kv[:, hd]) for hd in heads]
    acc = h_ref[0] + _dot(tok_ref[0], w_ref[0, 0:d])
    outs = []
    for hd, s in zip(heads, logits):
        s = s * scale
        e = jnp.exp2(s - jnp.max(s, axis=-1, keepdims=True))
        p = e / jnp.sum(e, axis=-1, keepdims=True)
        outs.append(_dot(p.astype(BF16), kv[:, XATTN_WIDTH + hd.start:XATTN_WIDTH + hd.stop]))
    mo = jnp.concatenate(outs, axis=-1).astype(BF16)
    o_ref[0] = acc + _dot(mo, w_ref[0, d:])


def _mixout(h, tok, xq, kv, w, layer):
    batch, seq, d = h.shape
    tm = ROW_TILE
    return pl.pallas_call(
        _mixout_kernel,
        out_shape=jax.ShapeDtypeStruct((batch, seq, d), F32),
        grid=(batch, seq // tm),
        in_specs=[
            pl.BlockSpec((1, tm, d), lambda b, i: (b, i, 0)),
            pl.BlockSpec((1, tm, d), lambda b, i: (b, i, 0)),
            pl.BlockSpec((1, tm, XATTN_WIDTH), lambda b, i: (b, i, 0)),
            pl.BlockSpec((1, 1, N_MEM, 2 * XATTN_WIDTH), lambda b, i: (layer, b, 0, 0)),
            pl.BlockSpec((1, d + XATTN_WIDTH, d), lambda b, i: (layer, 0, 0)),
        ],
        out_specs=pl.BlockSpec((1, tm, d), lambda b, i: (b, i, 0)),
        compiler_params=pltpu.CompilerParams(
            dimension_semantics=("arbitrary", "arbitrary"),
            vmem_limit_bytes=_vmem_limit(40 * 1024 * 1024)),
        name="mixout",
    )(h, tok, xq, kv, w)


def _ffn_kernel(h_ref, g_ref, w1_ref, w2_ref, gf_ref, o_ref, *, final):
    h = h_ref[0]
    hn = _rms(h, g_ref[...]).astype(BF16)
    acc = h
    chunks = [c * FFN_CHUNK for c in range(FFN_HIDDEN // FFN_CHUNK)]
    gate_up = [(_dot(hn, w1_ref[0, :, lo:lo + FFN_CHUNK]),
                _dot(hn, w1_ref[0, :, FFN_HIDDEN + lo:FFN_HIDDEN + lo + FFN_CHUNK]))
               for lo in chunks[:1]]
    for c, lo in enumerate(chunks):
        if c + 1 < len(chunks):
            nxt = chunks[c + 1]
            gate_up.append((_dot(hn, w1_ref[0, :, nxt:nxt + FFN_CHUNK]),
                            _dot(hn, w1_ref[0, :, FFN_HIDDEN + nxt:FFN_HIDDEN + nxt + FFN_CHUNK])))
        a, u = gate_up[c]
        act = (a * _sigmoid(a) * u).astype(BF16)
        acc = acc + _dot(act, w2_ref[0, lo:lo + FFN_CHUNK])
    if final:
        acc = _rms(acc, gf_ref[...])
    o_ref[0] = acc


def _ffn(h, g, w1, w2, gf, layer, final):
    batch, seq, d = h.shape
    tm = FFN_ROW_TILE
    return pl.pallas_call(
        functools.partial(_ffn_kernel, final=final),
        out_shape=jax.ShapeDtypeStruct((batch, seq, d), F32),
        grid=(batch, seq // tm),
        in_specs=[
            pl.BlockSpec((1, tm, d), lambda b, i: (b, i, 0)),
            pl.BlockSpec((1, d), lambda b, i: (0, 0)),
            pl.BlockSpec((1, d, 2 * FFN_HIDDEN), lambda b, i: (layer, 0, 0)),
            pl.BlockSpec((1, FFN_HIDDEN, d), lambda b, i: (layer, 0, 0)),
            pl.BlockSpec((1, d), lambda b, i: (0, 0)),
        ],
        out_specs=pl.BlockSpec((1, tm, d), lambda b, i: (b, i, 0)),
        compiler_params=pltpu.CompilerParams(
            dimension_semantics=("arbitrary", "arbitrary"),
            vmem_limit_bytes=_vmem_limit(52 * 1024 * 1024)),
        name="ffn",
    )(h, g.reshape(1, d), w1, w2, gf.reshape(1, d))


_QW = NSA_HEADS * NSA_HEAD_DIM
_KVW = NSA_KV_GROUPS * NSA_HEAD_DIM
_PAIR = 2 * NSA_HEAD_DIM
_C_Q = 0
_C_CMP = _C_Q + _QW
_C_SLC = _C_CMP + 2 * _KVW
_C_WIN = _C_SLC + NSA_KV_GROUPS * _PAIR
_C_GATE = _C_WIN + NSA_KV_GROUPS * _PAIR
_C_XQ = _C_GATE + V7X_LANES
_C_END = _C_XQ + XATTN_WIDTH


_SUM_ROWS = 16
_VT_ROWS = _SUM_ROWS + NSA_HEAD_DIM


def _ones_and_vt(kv):
    v_t = kv.T[NSA_HEAD_DIM:]
    return jnp.concatenate([jnp.ones((_SUM_ROWS, kv.shape[0]), F32), v_t], axis=0).astype(BF16)


def _nsa_proj_kernel(h_ref, g_ref, w_ref, q_ref, cr_ref, slc_ref, slc1_ref, win_ref, win1_ref,
                     gt_ref, xq_ref):
    hn = _rms(h_ref[0], g_ref[...]).astype(BF16)
    q = _dot(hn, w_ref[:, _C_Q:_C_CMP]) * (NSA_HEAD_DIM ** -0.5 * LOG2E)
    gates = _sigmoid(_dot(hn, w_ref[:, _C_GATE:_C_XQ]))
    s = _dot(hn, w_ref[:, _C_SLC:_C_WIN])
    for c in range(ROW_TILE // Q_TILE):
        rows = slice(c * Q_TILE, (c + 1) * Q_TILE)
        q_ref[0, c] = q[rows].T.astype(BF16)
        gt_ref[0, c] = gates[rows].T
    w = _dot(hn, w_ref[:, _C_WIN:_C_GATE])
    cr = _dot(hn, w_ref[:, _C_CMP:_C_SLC])
    for p in range(2 * _KVW // V7X_LANES):
        cr_ref[0, p] = cr[:, p * V7X_LANES:(p + 1) * V7X_LANES]
    xq_ref[0] = _dot(hn, w_ref[:, _C_XQ:_C_END]).astype(BF16)
    k_lanes = lax.broadcasted_iota(jnp.int32, (s.shape[0], _PAIR), 1) < NSA_HEAD_DIM
    for g in range(NSA_KV_GROUPS):
        sg = s[:, g * _PAIR:(g + 1) * _PAIR]
        wg = w[:, g * _PAIR:(g + 1) * _PAIR]
        slc_ref[0, g] = jnp.where(k_lanes, sg, 0.0).astype(BF16)
        win_ref[0, g] = jnp.where(k_lanes, wg, 0.0).astype(BF16)
        for c in range(ROW_TILE // KV_TILE):
            rows = slice(c * KV_TILE, (c + 1) * KV_TILE)
            slc1_ref[0, g, c] = _ones_and_vt(sg[rows])
            win1_ref[0, g, c] = _ones_and_vt(wg[rows])


def _nsa_proj(h, g, w):
    batch, seq, d = h.shape
    tm = ROW_TILE
    grp = NSA_KV_GROUPS
    row = lambda b, i: (b, i, 0)
    pair_shape = jax.ShapeDtypeStruct((batch, grp, seq, _PAIR), BF16)
    pair_spec = pl.BlockSpec((1, grp, tm, _PAIR), lambda b, i: (b, 0, i, 0))
    vt_shape = jax.ShapeDtypeStruct((batch, grp, seq // KV_TILE, _VT_ROWS, KV_TILE), BF16)
    vt_spec = pl.BlockSpec((1, grp, tm // KV_TILE, _VT_ROWS, KV_TILE), lambda b, i: (b, 0, i, 0, 0))
    return pl.pallas_call(
        _nsa_proj_kernel,
        out_shape=(jax.ShapeDtypeStruct((batch, seq // Q_TILE, _QW, Q_TILE), BF16),
                   jax.ShapeDtypeStruct((batch, 2 * _KVW // V7X_LANES, seq, V7X_LANES), F32),
                   pair_shape, vt_shape, pair_shape, vt_shape,
                   jax.ShapeDtypeStruct((batch, seq // Q_TILE, V7X_LANES, Q_TILE), F32),
                   jax.ShapeDtypeStruct((batch, seq, XATTN_WIDTH), BF16)),
        grid=(batch, seq // tm),
        in_specs=[
            pl.BlockSpec((1, tm, d), row),
            pl.BlockSpec((1, d), lambda b, i: (0, 0)),
            pl.BlockSpec((d, _C_END), lambda b, i: (0, 0)),
        ],
        out_specs=(pl.BlockSpec((1, tm // Q_TILE, _QW, Q_TILE), lambda b, i: (b, i, 0, 0)),
                   pl.BlockSpec((1, 2 * _KVW // V7X_LANES, tm, V7X_LANES), lambda b, i: (b, 0, i, 0)),
                   pair_spec, vt_spec, pair_spec, vt_spec,
                   pl.BlockSpec((1, tm // Q_TILE, V7X_LANES, Q_TILE), lambda b, i: (b, i, 0, 0)),
                   pl.BlockSpec((1, tm, XATTN_WIDTH), row)),
        compiler_params=pltpu.CompilerParams(
            dimension_semantics=("arbitrary", "arbitrary"),
            vmem_limit_bytes=_vmem_limit(48 * 1024 * 1024)),
        name="nsa_proj",
    )(h, g.reshape(1, d), w)


_CHUNKS = 512


def _compress_kernel(raw_ref, pos_ref, w1_ref, w2_ref, o_ref, h2_scr, *, transposed):
    dk = NSA_HEAD_DIM
    half = CMP_BLOCK // 2
    h1 = [jnp.zeros((_CHUNKS, CMP_HIDDEN), F32) for _ in range(NSA_KV_GROUPS)]
    h2 = [jnp.zeros((_CHUNKS, CMP_HIDDEN), F32) for _ in range(NSA_KV_GROUPS)]
    gpl = V7X_LANES // dk
    for l in range(half):
        for p in range(NSA_KV_GROUPS // gpl):
            lanes = slice(p * V7X_LANES, (p + 1) * V7X_LANES)
            rows = raw_ref[0, p, pl.ds(l, _CHUNKS, stride=CMP_STRIDE), :]
            a1 = (rows + pos_ref[0, l:l + 1, lanes]).astype(BF16)
            a2 = (rows + pos_ref[0, half + l:half + l + 1, lanes]).astype(BF16)
            for gg in range(gpl):
                g = p * gpl + gg
                h1[g] = h1[g] + _dot(a1[:, gg * dk:(gg + 1) * dk], w1_ref[0, l * dk:(l + 1) * dk, :])
                h2[g] = h2[g] + _dot(a2[:, gg * dk:(gg + 1) * dk],
                                     w1_ref[0, (half + l) * dk:(half + l + 1) * dk, :])
    h2_scr[_CHUNKS:_CHUNKS + 8, :] = jnp.zeros((8, CMP_HIDDEN), F32)
    for g in range(NSA_KV_GROUPS):
        h2_scr[0:_CHUNKS, :] = h2[g]
        hm = h1[g] + h2_scr[pl.ds(1, _CHUNKS), :]
        hm = (hm * _sigmoid(hm)).astype(BF16)
        if transposed:
            o_ref[0, g] = _dot_nt(w2_ref[0], hm).astype(BF16)
        else:
            out = _dot(hm, w2_ref[0])
            o_ref[0, g] = jnp.concatenate([out, jnp.zeros_like(out)], axis=1).astype(BF16)


def _compress(raw, pos4, w1, w2, which, transposed):
    batch, _, seq, _ = raw.shape
    grp = NSA_KV_GROUPS
    tiles = _KVW // V7X_LANES
    if transposed:
        out_shape = (batch, grp, NSA_HEAD_DIM, N_CMP_PAD)
        w2_block = (1, NSA_HEAD_DIM, CMP_HIDDEN)
    else:
        out_shape = (batch, grp, N_CMP_PAD, _PAIR)
        w2_block = (1, CMP_HIDDEN, NSA_HEAD_DIM)
    return pl.pallas_call(
        functools.partial(_compress_kernel, transposed=transposed),
        out_shape=jax.ShapeDtypeStruct(out_shape, BF16),
        grid=(batch,),
        in_specs=[
            pl.BlockSpec((1, tiles, seq, V7X_LANES), lambda b: (b, which, 0, 0)),
            pl.BlockSpec((1, CMP_BLOCK, _KVW), lambda b: (which, 0, 0)),
            pl.BlockSpec((1, CMP_BLOCK * NSA_HEAD_DIM, CMP_HIDDEN), lambda b: (which, 0, 0)),
            pl.BlockSpec(w2_block, lambda b: (0, 0, 0)),
        ],
        out_specs=pl.BlockSpec((1,) + out_shape[1:], lambda b: (b, 0, 0, 0)),
        scratch_shapes=[pltpu.VMEM((_CHUNKS + 8, CMP_HIDDEN), F32)],
        compiler_params=pltpu.CompilerParams(
            dimension_semantics=("arbitrary",),
            vmem_limit_bytes=_vmem_limit(40 * 1024 * 1024)),
        name="compress_v" if transposed else "compress_k",
    )(raw, pos4, w1, w2)


def _t5_bucket(dist):
    n = jnp.maximum(dist, 0)
    max_exact = REL_BUCKETS // 2
    nf = jnp.maximum(n, 1).astype(F32)
    large = max_exact + (jnp.log(nf / max_exact) / np.log(REL_MAX_DIST / max_exact)
                         * (REL_BUCKETS - max_exact)).astype(jnp.int32)
    large = jnp.minimum(large, REL_BUCKETS - 1)
    return jnp.where(n < max_exact, n, large)


def _table_kernel(thr_ref, rb_ref, cbt_ref, tslc_ref, twin_ref):
    tq = Q_TILE
    g = pl.program_id(0)

    def entries(dist, h, limit):
        val = jnp.full(dist.shape, rb_ref[0, h], F32)
        for k in range(1, REL_BUCKETS):
            val = jnp.where(dist >= thr_ref[k], rb_ref[k, h], val)
        val = (val - rb_ref[REL_BUCKETS - 1, h]) * LOG2E
        return jnp.where(jnp.where(dist >= 0, dist, limit) < limit, val, NEG)

    def far_entries(dist, limit):
        return jnp.where(jnp.where(dist >= 0, dist, limit) < limit, 0.0, NEG)

    def fill(ref, band, dist_of_rows, n_rows, limit, h):
        step = 64
        for r0 in range(0, n_rows, step):
            shape = (step, tq)
            dist = dist_of_rows(r0 + lax.broadcasted_iota(jnp.int32, shape, 0),
                                lax.broadcasted_iota(jnp.int32, shape, 1))
            lo, hi = dist_of_rows(r0 + step - 1, 0), dist_of_rows(r0, tq - 1)
            near = hi >= 0 and lo < REL_MAX_DIST
            ref[0, r0:r0 + step, band] = entries(dist, h, limit) if near else far_entries(dist, limit)

    no_limit = 1 << 30
    for j in range(NSA_HPG):
        h = g * NSA_HPG + j
        band = slice(j * tq, (j + 1) * tq)
        fill(cbt_ref, band, lambda u, q: q - (CMP_BLOCK - 1) - CMP_STRIDE * (u - N_CMP_PAD),
             2 * N_CMP_PAD, no_limit, h)
        fill(tslc_ref, band, lambda u, q: q - u + tq, 2 * tq, no_limit, h)
        fill(twin_ref, band, lambda u, q: q - u + WINDOW, WINDOW + tq, WINDOW, h)


def _bias_tables(rel_bias):
    tq = Q_TILE
    rows = NSA_HPG * tq
    buckets = _t5_bucket(jnp.arange(REL_MAX_DIST + 1))
    thr = jnp.sum(buckets[None, :] < jnp.arange(REL_BUCKETS)[:, None], axis=1).astype(jnp.int32)
    smem = pl.BlockSpec(memory_space=pltpu.SMEM)
    return pl.pallas_call(
        _table_kernel,
        out_shape=(jax.ShapeDtypeStruct((NSA_KV_GROUPS, 2 * N_CMP_PAD, rows), F32),
                   jax.ShapeDtypeStruct((NSA_KV_GROUPS, 2 * tq, rows), F32),
                   jax.ShapeDtypeStruct((NSA_KV_GROUPS, WINDOW + tq, rows), F32)),
        grid=(NSA_KV_GROUPS,),
        in_specs=[smem, smem],
        out_specs=(pl.BlockSpec((1, 2 * N_CMP_PAD, rows), lambda g: (g, 0, 0)),
                   pl.BlockSpec((1, 2 * tq, rows), lambda g: (g, 0, 0)),
                   pl.BlockSpec((1, WINDOW + tq, rows), lambda g: (g, 0, 0))),
        compiler_params=pltpu.CompilerParams(
            dimension_semantics=("arbitrary",),
            vmem_limit_bytes=_vmem_limit(40 * 1024 * 1024)),
        name="bias_tables",
    )(thr, rel_bias.astype(F32))


def _block_mask_rows(seq):
    pos = jnp.arange(seq, dtype=jnp.int32)[:, None]
    col = jnp.arange(N_SEL, dtype=jnp.int32)[None, :]
    onehot = jnp.where(col == pos // SEL_BLOCK, NEG, 0.0)
    return jnp.concatenate([onehot, jnp.full((Q_TILE, N_SEL), NEG)], axis=0).astype(BF16)


def _online_absorb(s, v1t, state):
    m = jnp.max(s, axis=0, keepdims=True)
    if state is not None:
        m = jnp.maximum(state[0], m)
    acc = _dot(v1t, jnp.exp2(s - m).astype(BF16))
    if state is not None:
        acc = jnp.exp2(state[0] - m) * state[1] + acc
    return m, acc


def _normalise_t(acc):
    return acc[_SUM_ROWS:] * (1.0 / acc[0:1])


def _nsa_kernel(q_ref, kc_ref, vct_ref, slc_ref, slc1_ref, win_ref, win1_ref, bm_ref, gt_ref,
                cbt_ref, tslc_ref, twin_ref, o_ref, ps_scr, lhs_scr, m_scr, acc_scr, s_scr, ocmp_scr,
                owin_scr, smax_scr):
    tq = Q_TILE
    hpg = NSA_HPG
    dk = NSA_HEAD_DIM
    seq = slc_ref.shape[2]
    g = pl.program_id(1)
    i = pl.program_id(2)
    q0 = pl.multiple_of(i * tq, tq)
    bands = [slice(j * tq, (j + 1) * tq) for j in range(hpg)]

    for j in range(hpg):
        lhs_scr[0:dk, bands[j]] = q_ref[0, 0, j * dk:(j + 1) * dk, :]
    lhs_scr[dk:dk + _SUM_ROWS, :] = jnp.ones((_SUM_ROWS, hpg * tq), BF16)
    lhs_scr[dk + _SUM_ROWS:_PAIR, :] = jnp.zeros((_PAIR - dk - _SUM_ROWS, hpg * tq), BF16)
    qs_t = lhs_scr[0:_PAIR, :]

    n_prev = WINDOW // tq
    mask_lane = lax.broadcasted_iota(jnp.int32, (1, _PAIR), 1) == dk

    def win_logits(r):
        tile = jnp.maximum(i - r, 0)
        keys = win_ref[0, 0, pl.ds(pl.multiple_of(tile * tq, tq), tq), :]
        if r:
            keys = keys + jnp.where(jnp.logical_and(mask_lane, i < r), NEG, 0.0).astype(BF16)
        rows = slice((n_prev - r) * tq, (n_prev - r + 1) * tq)
        return _dot(keys, qs_t) + twin_ref[0, rows, :]

    def win_absorb(r, s, state):
        return _online_absorb(s, win1_ref[0, 0, jnp.maximum(i - r, 0)], state)

    def compress_and_select(n_cmp):
        n_sel = n_cmp * CMP_STRIDE // SEL_BLOCK
        st = _dot(kc_ref[0, 0, 0:n_cmp, :], qs_t)
        st = st + cbt_ref[0, pl.ds(pl.multiple_of(N_CMP_PAD - (tq // CMP_STRIDE) * i, 8), n_cmp), :]
        mx = jnp.maximum(jnp.max(st, axis=0, keepdims=True), 0.1 * NEG)
        e = jnp.exp2(st - mx)
        den = jnp.sum(e, axis=0, keepdims=True)
        inv = jnp.where(den > 0.0, 1.0 / den, 0.0)
        ocmp_scr[...] = _dot(vct_ref[0, 0, :, 0:n_cmp], e.astype(BF16)) * inv
        psum = e[:, bands[0]] * inv[:, bands[0]]
        for j in range(1, hpg):
            psum = psum + e[:, bands[j]] * inv[:, bands[j]]

        ratio = SEL_BLOCK // CMP_STRIDE
        imp_tiles = []
        for c in range(tq // V7X_LANES):
            ps_scr[c, 0:8, :] = jnp.zeros((8, V7X_LANES), F32)
            ps_scr[c, 8:8 + n_cmp, :] = psum[:, c * V7X_LANES:(c + 1) * V7X_LANES]
            taps = [ps_scr[c, pl.ds(7 + r, n_sel, stride=ratio), :] for r in range(5)]
            imp_tiles.append(0.5 * taps[0] + taps[1] + taps[2] + taps[3] + 0.5 * taps[4])
        imp = jnp.concatenate(imp_tiles, axis=1)

        jrow = lax.broadcasted_iota(jnp.int32, (n_sel, tq), 0)
        cur = (q0 + lax.broadcasted_iota(jnp.int32, (n_sel, tq), 1)) // SEL_BLOCK
        taken = -3.0e38
        imp = jnp.where(jrow <= cur, imp, -SEL_FORCE)
        for forced_block in (0, cur, cur - 1):
            imp = jnp.where(jrow == forced_block, taken, imp)
        n_rounds = SEL_TOPK - 3
        n_tiles = n_prev + 1
        s_win = [win_logits(0), win_logits(1)]
        win_state, absorbed = None, 0
        for rnd in range(n_rounds):
            top = jnp.max(imp, axis=0, keepdims=True)
            first = jnp.min(jnp.where(imp == top, jrow, n_sel), axis=0, keepdims=True)
            imp = jnp.where(jrow == first, taken, imp)
            while absorbed < (rnd + 1) * n_tiles // n_rounds:
                if absorbed + 2 < n_tiles:
                    s_win.append(win_logits(absorbed + 2))
                win_state = win_absorb(absorbed, s_win[absorbed], win_state)
                absorbed += 1
        owin_scr[...] = _normalise_t(win_state[1])
        not_sel = jnp.where(imp < 0.5 * taken, 0.0, 1.0).astype(BF16)
        for j in range(hpg):
            lhs_scr[_PAIR:_PAIR + n_sel, bands[j]] = not_sel
        if n_sel < N_SEL:
            lhs_scr[_PAIR + n_sel:2 * _PAIR, :] = jnp.ones((N_SEL - n_sel, hpg * tq), BF16)

    tiles_per_variant = (seq // tq) // CMP_VARIANTS
    for v in range(CMP_VARIANTS):
        pl.when(i // tiles_per_variant == v)(
            functools.partial(compress_and_select, N_CMP_PAD * (v + 1) // CMP_VARIANTS))
    lhs = lhs_scr[...]

    def key_tile(kv_ref, tile, mask_tile):
        return jnp.concatenate([kv_ref[0, 0, pl.ds(pl.multiple_of(tile * tq, tq), tq), :],
                                bm_ref[pl.ds(pl.multiple_of(mask_tile * tq, tq), tq), :]], axis=1)

    all_masked = seq // tq

    prev = jnp.maximum(i - 1, 0)
    s_own = _dot(key_tile(slc_ref, i, i), lhs) + tslc_ref[0, tq:2 * tq, :]
    s_prev = (_dot(key_tile(slc_ref, prev, jnp.where(i == 0, all_masked, i - 1)), lhs)
              + tslc_ref[0, 0:tq, :])

    n_far = jnp.maximum(i - 1, 0)
    n_groups = (n_far + FAR_GROUP - 1) // FAR_GROUP

    def tile_or_none(kt):
        valid = kt < n_far
        return jnp.where(valid, kt, 0), jnp.where(valid, kt, all_masked)

    def stage_logits(slot, kg):
        tiles = [key_tile(slc_ref, *tile_or_none(FAR_GROUP * kg + t)) for t in range(FAR_GROUP)]
        s = _dot(jnp.concatenate(tiles, axis=0), lhs_scr[...])
        s_scr[slot] = s
        smax_scr[8 * slot:8 * slot + 1, :] = jnp.max(s, axis=0, keepdims=True)

    def absorb(slot, kg):
        m_old = m_scr[0:1, :]
        m_new = jnp.maximum(m_old, smax_scr[8 * slot:8 * slot + 1, :])
        p = jnp.exp2(s_scr[slot] - m_new)
        v1t = jnp.concatenate([slc1_ref[0, 0, tile_or_none(FAR_GROUP * kg + t)[0]]
                               for t in range(FAR_GROUP)], axis=1)
        acc_scr[...] = jnp.exp2(m_old - m_new) * acc_scr[...] + _dot(v1t, p.astype(BF16))
        m_scr[0:1, :] = m_new

    def far_trip(base, groups):
        for t in range(groups):
            stage_logits((t + 1) % 2, base + t + 1)
            absorb(t % 2, base + t)

    stage_logits(0, 0)
    state = _online_absorb(s_own, slc1_ref[0, 0, i], None)
    m, acc = _online_absorb(s_prev, slc1_ref[0, 0, prev], state)
    m_scr[0:1, :] = m
    acc_scr[...] = acc
    n_long = n_groups // 4

    def long_body(k, carry):
        far_trip(4 * k, 4)
        return carry

    lax.fori_loop(0, n_long, long_body, 0)
    done = 4 * n_long
    pair = (n_groups - done) // 2

    @pl.when(pair > 0)
    def _():
        far_trip(done, 2)

    @pl.when(n_groups - done - 2 * pair > 0)
    def _():
        far_trip(done + 2 * pair, 1)

    o_slc_t = _normalise_t(acc_scr[...])

    merged = []
    for j in range(hpg):
        h = g * hpg + j
        gate = [gt_ref[0, 0, pl.ds(br * NSA_HEADS + h, 1), :] for br in range(3)]
        merged.append(gate[0] * ocmp_scr[:, bands[j]] + gate[1] * o_slc_t[:, bands[j]]
                      + gate[2] * owin_scr[:, bands[j]])
    o_ref[0] = jnp.concatenate(merged, axis=0).T.astype(BF16)


def _nsa_attention(q, kc, vct, slc, slc1, win, win1, block_mask, gates, cbt, tslc, twin):
    batch, _, seq, _ = slc.shape
    grp = NSA_KV_GROUPS
    tq = Q_TILE
    rows = NSA_HPG * tq
    per_group = lambda b, g, i: (b, g, 0, 0)
    per_group_t = lambda b, g, i: (b, g, 0, 0, 0)
    return pl.pallas_call(
        _nsa_kernel,
        out_shape=jax.ShapeDtypeStruct((batch, seq, _QW), BF16),
        grid=(batch, grp, seq // tq),
        in_specs=[
            pl.BlockSpec((1, 1, NSA_HPG * NSA_HEAD_DIM, tq), lambda b, g, i: (b, i, g, 0)),
            pl.BlockSpec((1, 1, N_CMP_PAD, _PAIR), per_group),
            pl.BlockSpec((1, 1, NSA_HEAD_DIM, N_CMP_PAD), per_group),
            pl.BlockSpec((1, 1, seq, _PAIR), per_group),
            pl.BlockSpec((1, 1, seq // tq, _VT_ROWS, tq), per_group_t),
            pl.BlockSpec((1, 1, seq, _PAIR), per_group),
            pl.BlockSpec((1, 1, seq // tq, _VT_ROWS, tq), per_group_t),
            _resident((seq + tq, N_SEL), lambda b, g, i: (0, 0)),
            pl.BlockSpec((1, 1, V7X_LANES, tq), lambda b, g, i: (b, i, 0, 0)),
            pl.BlockSpec((1, 2 * N_CMP_PAD, rows), lambda b, g, i: (g, 0, 0)),
            pl.BlockSpec((1, 2 * tq, rows), lambda b, g, i: (g, 0, 0)),
            pl.BlockSpec((1, WINDOW + tq, rows), lambda b, g, i: (g, 0, 0)),
        ],
        out_specs=pl.BlockSpec((1, tq, NSA_HPG * NSA_HEAD_DIM), lambda b, g, i: (b, i, g)),
        scratch_shapes=[pltpu.VMEM((tq // V7X_LANES, N_CMP_PAD + 16, V7X_LANES), F32),
                        pltpu.VMEM((2 * _PAIR, rows), BF16),
                        pltpu.VMEM((8, rows), F32),
                        pltpu.VMEM((_VT_ROWS, rows), F32),
                        pltpu.VMEM((2, FAR_GROUP * tq, rows), F32),
                        pltpu.VMEM((NSA_HEAD_DIM, rows), F32),
                        pltpu.VMEM((NSA_HEAD_DIM, rows), F32),
                        pltpu.VMEM((16, rows), F32)],
        compiler_params=pltpu.CompilerParams(
            dimension_semantics=("arbitrary", "arbitrary", "arbitrary"),
            vmem_limit_bytes=_vmem_limit(52 * 1024 * 1024)),
        name="nsa_attention",
    )(q, kc, vct, slc, slc1, win, win1, block_mask, gates, cbt, tslc, twin)


def _nsa_weight(w):
    d = w.shape[0]
    q = w[:, :_QW]
    kv = w[:, _QW:_QW + 6 * _KVW].reshape(d, 6, NSA_KV_GROUPS, NSA_HEAD_DIM)
    cmp_raw = kv[:, 0:2].reshape(d, 2 * _KVW)
    slc = jnp.stack([kv[:, 2], kv[:, 3]], axis=2).reshape(d, NSA_KV_GROUPS * _PAIR)
    win = jnp.stack([kv[:, 4], kv[:, 5]], axis=2).reshape(d, NSA_KV_GROUPS * _PAIR)
    gates = w[:, _QW + 6 * _KVW:_QW + 6 * _KVW + 3 * NSA_HEADS]
    gates = jnp.pad(gates, ((0, 0), (0, V7X_LANES - 3 * NSA_HEADS)))
    xq = w[:, _QW + 6 * _KVW + 3 * NSA_HEADS:]
    return jnp.concatenate([q, cmp_raw, slc, win, gates, xq], axis=1).astype(BF16)


def kernel(x, mem, norm_mix_g, norm_mem_g, norm_ffn_g, norm_final_g, w_in_conv, conv_w, w_in_nsa,
           cmp_pos_k, cmp_w1_k, cmp_w2_k, cmp_pos_v, cmp_w1_v, cmp_w2_v, rel_bias, w_mem_kv, w_out,
           w_ffn_in, w_ffn_out):
    seq = x.shape[1]
    assert seq % ROW_TILE == 0 and seq % Q_TILE == 0 and x.shape[2] == D_MODEL
    assert seq == CMP_STRIDE * _CHUNKS and seq == SEL_BLOCK * N_SEL and Q_TILE == KV_TILE
    memkv = _memkv(mem, norm_mem_g, w_mem_kv.astype(BF16))
    w_out_b = w_out.astype(BF16)
    w_ffn_in_b = w_ffn_in.astype(BF16)
    w_ffn_out_b = w_ffn_out.astype(BF16)

    tok, xq = _conv_front(x, norm_mix_g[0], w_in_conv[0].astype(BF16), conv_w[0])
    h = _mixout(x, tok, xq, memkv, w_out_b, 0)
    h = _ffn(h, norm_ffn_g[0], w_ffn_in_b, w_ffn_out_b, norm_final_g, 0, final=False)

    q, cmp_raw, slc, slc1, win, win1, gates, xq = _nsa_proj(h, norm_mix_g[1], _nsa_weight(w_in_nsa[0]))
    pos4 = jnp.stack([jnp.tile(cmp_pos_k[0], (1, NSA_KV_GROUPS)),
                      jnp.tile(cmp_pos_v[0], (1, NSA_KV_GROUPS))])
    w1 = jnp.stack([cmp_w1_k[0], cmp_w1_v[0]]).astype(BF16)
    kc = _compress(cmp_raw, pos4, w1, cmp_w2_k[0].astype(BF16)[None], 0, transposed=False)
    vct = _compress(cmp_raw, pos4, w1, cmp_w2_v[0].T.astype(BF16)[None], 1, transposed=True)
    tok = _nsa_attention(q, kc, vct, slc, slc1, win, win1, _block_mask_rows(seq), gates,
                         *_bias_tables(rel_bias))
    h = _mixout(h, tok, xq, memkv, w_out_b, 1)
    return _ffn(h, norm_ffn_g[1], w_ffn_in_b, w_ffn_out_b, norm_final_g, 1, final=True)
```

```python
import functools

import numpy as np
import jax
import jax.numpy as jnp
from jax import lax
from jax.experimental import pallas as pl
from jax.experimental.pallas import tpu as pltpu

F32 = jnp.float32
BF16 = jnp.bfloat16

D_MODEL = 1024
N_MEM = 256
CONV_WIDTH = 3
NSA_HEADS = 16
NSA_HEAD_DIM = 64
NSA_KV_GROUPS = 4
NSA_HPG = NSA_HEADS // NSA_KV_GROUPS
CMP_BLOCK = 32
CMP_STRIDE = 16
CMP_HIDDEN = 2 * NSA_HEAD_DIM
SEL_BLOCK = 64
SEL_TOPK = 16
SEL_FORCE = 1.0e4
WINDOW = 512
XATTN_HEADS = 4
XATTN_HEAD_DIM = 128
XATTN_WIDTH = XATTN_HEADS * XATTN_HEAD_DIM
FFN_HIDDEN = 2816
REL_BUCKETS = 32
REL_MAX_DIST = 128
RMS_EPS = 1e-6
NEG = -1e30
LOG2E = 1.4426950408889634

V7X_LANES = 128
V7X_VMEM_BYTES = 64 * 1024 * 1024

ROW_TILE = 512
CONV_COLS = 256
FFN_ROW_TILE = 512
FFN_CHUNK = 256
Q_TILE = 256
KV_TILE = 256
FAR_GROUP = 2
CMP_VARIANTS = 8
N_CMP_PAD = 512
N_SEL = 128

_NT = (((1,), (1,)), ((), ()))


def _vmem_limit(nbytes):
    return int(min(max(nbytes, 16 * 1024 * 1024), V7X_VMEM_BYTES - 8 * 1024 * 1024))


def _rms(x, g):
    ms = jnp.mean(x * x, axis=-1, keepdims=True)
    return x * lax.rsqrt(ms + RMS_EPS) * g


def _dot(a, b):
    return jnp.dot(a, b, preferred_element_type=F32)


def _dot_nt(a, b):
    return lax.dot_general(a, b, _NT, preferred_element_type=F32)


def _sigmoid(x):
    return 1.0 / (1.0 + jnp.exp(-x))


def _resident(block_shape, index_map):
    return pl.BlockSpec(block_shape, index_map, pipeline_mode=pl.Buffered(1))


def _memkv_kernel(mem_ref, g_ref, w_ref, o_ref):
    xn = _rms(mem_ref[0], g_ref[0]).astype(BF16)
    o_ref[0, 0] = _dot(xn, w_ref[0]).astype(BF16)


def _memkv(mem, gains, w):
    depth, batch = w.shape[0], mem.shape[0]
    return pl.pallas_call(
        _memkv_kernel,
        out_shape=jax.ShapeDtypeStruct((depth, batch, N_MEM, 2 * XATTN_WIDTH), BF16),
        grid=(depth, batch),
        in_specs=[
            pl.BlockSpec((1, N_MEM, D_MODEL), lambda l, b: (b, 0, 0)),
            pl.BlockSpec((1, 1, D_MODEL), lambda l, b: (l, 0, 0)),
            pl.BlockSpec((1, D_MODEL, 2 * XATTN_WIDTH), lambda l, b: (l, 0, 0)),
        ],
        out_specs=pl.BlockSpec((1, 1, N_MEM, 2 * XATTN_WIDTH), lambda l, b: (l, b, 0, 0)),
        compiler_params=pltpu.CompilerParams(
            dimension_semantics=("arbitrary", "arbitrary"),
            vmem_limit_bytes=_vmem_limit(24 * 1024 * 1024)),
        name="memkv",
    )(mem, gains.reshape(depth, 1, D_MODEL), w)


def _conv_kernel(h_ref, hp_ref, g_ref, w_ref, cw_ref, tok_ref, xq_ref):
    d = D_MODEL
    i = pl.program_id(1)
    g = g_ref[...]
    hn = _rms(h_ref[0], g).astype(BF16)
    hpn = _rms(hp_ref[0], g).astype(BF16)
    cw = cw_ref[...]
    row = lax.broadcasted_iota(jnp.int32, (h_ref.shape[1], CONV_COLS), 0)

    def projections(c):
        lo = c * CONV_COLS
        col = lambda part: w_ref[:, part * d + lo:part * d + lo + CONV_COLS]
        vp = jnp.where(i > 0, _dot(hpn, col(1)) * _dot(hpn, col(2)), 0.0)
        return _dot(hn, col(0)), _dot(hn, col(1)) * _dot(hn, col(2)), vp

    n_chunks = d // CONV_COLS
    ready = projections(0)
    for c in range(n_chunks):
        u_b, v, vp = ready
        if c + 1 < n_chunks:
            ready = projections(c + 1)
        else:
            xq_ref[0] = _dot(hn, w_ref[:, 3 * d:]).astype(BF16)
        cols = slice(c * CONV_COLS, (c + 1) * CONV_COLS)
        v1 = jnp.where(row == 0, vp[7:8], pltpu.roll(v, 1, axis=0))
        v2 = jnp.where(row == 0, vp[6:7], jnp.where(row == 1, vp[7:8], pltpu.roll(v, 2, axis=0)))
        conv = v2 * cw[0:1, cols] + v1 * cw[1:2, cols] + v * cw[2:3, cols]
        tok_ref[0, :, cols] = (u_b * conv).astype(BF16)


def _conv_front(h, g, w, cw):
    batch, seq, d = h.shape
    tm = ROW_TILE
    width = w.shape[1]
    halo = 8
    return pl.pallas_call(
        _conv_kernel,
        out_shape=(jax.ShapeDtypeStruct((batch, seq, d), BF16),
                   jax.ShapeDtypeStruct((batch, seq, XATTN_WIDTH), BF16)),
        grid=(batch, seq // tm),
        in_specs=[
            pl.BlockSpec((1, tm, d), lambda b, i: (b, i, 0)),
            pl.BlockSpec((1, halo, d), lambda b, i: (b, jnp.maximum(i * (tm // halo) - 1, 0), 0)),
            pl.BlockSpec((1, d), lambda b, i: (0, 0)),
            pl.BlockSpec((d, width), lambda b, i: (0, 0)),
            pl.BlockSpec((CONV_WIDTH, d), lambda b, i: (0, 0)),
        ],
        out_specs=(pl.BlockSpec((1, tm, d), lambda b, i: (b, i, 0)),
                   pl.BlockSpec((1, tm, XATTN_WIDTH), lambda b, i: (b, i, 0))),
        compiler_params=pltpu.CompilerParams(
            dimension_semantics=("arbitrary", "arbitrary"),
            vmem_limit_bytes=_vmem_limit(48 * 1024 * 1024)),
        name="conv_front",
    )(h, h, g.reshape(1, d), w, cw)


def _mixout_kernel(h_ref, tok_ref, xq_ref, kv_ref, w_ref, o_ref):
    d = D_MODEL
    xq = xq_ref[0]
    kv = kv_ref[0, 0]
    scale = XATTN_HEAD_DIM ** -0.5 * LOG2E
    heads = [slice(hh * XATTN_HEAD_DIM, (hh + 1) * XATTN_HEAD_DIM) for hh in range(XATTN_HEADS)]
    logits = [_dot_nt(xq[:, hd], kv[:, hd]) for hd in heads]
    acc = h_ref[0] + _dot(tok_ref[0], w_ref[0, 0:d])
    outs = []
    for hd, s in zip(heads, logits):
        s = s * scale
        e = jnp.exp2(s - jnp.max(s, axis=-1, keepdims=True))
        p = e / jnp.sum(e, axis=-1, keepdims=True)
        outs.append(_dot(p.astype(BF16), kv[:, XATTN_WIDTH + hd.start:XATTN_WIDTH + hd.stop]))
    mo = jnp.concatenate(outs, axis=-1).astype(BF16)
    o_ref[0] = acc + _dot(mo, w_ref[0, d:])


def _mixout(h, tok, xq, kv, w, layer):
    batch, seq, d = h.shape
    tm = ROW_TILE
    return pl.pallas_call(
        _mixout_kernel,
        out_shape=jax.ShapeDtypeStruct((batch, seq, d), F32),
        grid=(batch, seq // tm),
        in_specs=[
            pl.BlockSpec((1, tm, d), lambda b, i: (b, i, 0)),
            pl.BlockSpec((1, tm, d), lambda b, i: (b, i, 0)),
            pl.BlockSpec((1, tm, XATTN_WIDTH), lambda b, i: (b, i, 0)),
            pl.BlockSpec((1, 1, N_MEM, 2 * XATTN_WIDTH), lambda b, i: (layer, b, 0, 0)),
            pl.BlockSpec((1, d + XATTN_WIDTH, d), lambda b, i: (layer, 0, 0)),
        ],
        out_specs=pl.BlockSpec((1, tm, d), lambda b, i: (b, i, 0)),
        compiler_params=pltpu.CompilerParams(
            dimension_semantics=("arbitrary", "arbitrary"),
            vmem_limit_bytes=_vmem_limit(40 * 1024 * 1024)),
        name="mixout",
    )(h, tok, xq, kv, w)


def _ffn_kernel(h_ref, g_ref, w1_ref, w2_ref, gf_ref, o_ref, *, final):
    h = h_ref[0]
    hn = _rms(h, g_ref[...]).astype(BF16)
    acc = h
    chunks = [c * FFN_CHUNK for c in range(FFN_HIDDEN // FFN_CHUNK)]
    gate_up = [(_dot(hn, w1_ref[0, :, lo:lo + FFN_CHUNK]),
                _dot(hn, w1_ref[0, :, FFN_HIDDEN + lo:FFN_HIDDEN + lo + FFN_CHUNK]))
               for lo in chunks[:1]]
    for c, lo in enumerate(chunks):
        if c + 1 < len(chunks):
            nxt = chunks[c + 1]
            gate_up.append((_dot(hn, w1_ref[0, :, nxt:nxt + FFN_CHUNK]),
                            _dot(hn, w1_ref[0, :, FFN_HIDDEN + nxt:FFN_HIDDEN + nxt + FFN_CHUNK])))
        a, u = gate_up[c]
        act = (a * _sigmoid(a) * u).astype(BF16)
        acc = acc + _dot(act, w2_ref[0, lo:lo + FFN_CHUNK])
    if final:
        acc = _rms(acc, gf_ref[...])
    o_ref[0] = acc


def _ffn(h, g, w1, w2, gf, layer, final):
    batch, seq, d = h.shape
    tm = FFN_ROW_TILE
    return pl.pallas_call(
        functools.partial(_ffn_kernel, final=final),
        out_shape=jax.ShapeDtypeStruct((batch, seq, d), F32),
        grid=(batch, seq // tm),
        in_specs=[
            pl.BlockSpec((1, tm, d), lambda b, i: (b, i, 0)),
            pl.BlockSpec((1, d), lambda b, i: (0, 0)),
            pl.BlockSpec((1, d, 2 * FFN_HIDDEN), lambda b, i: (layer, 0, 0)),
            pl.BlockSpec((1, FFN_HIDDEN, d), lambda b, i: (layer, 0, 0)),
            pl.BlockSpec((1, d), lambda b, i: (0, 0)),
        ],
        out_specs=pl.BlockSpec((1, tm, d), lambda b, i: (b, i, 0)),
        compiler_params=pltpu.CompilerParams(
            dimension_semantics=("arbitrary", "arbitrary"),
            vmem_limit_bytes=_vmem_limit(52 * 1024 * 1024)),
        name="ffn",
    )(h, g.reshape(1, d), w1, w2, gf.reshape(1, d))


_QW = NSA_HEADS * NSA_HEAD_DIM
_KVW = NSA_KV_GROUPS * NSA_HEAD_DIM
_PAIR = 2 * NSA_HEAD_DIM
_C_Q = 0
_C_CMP = _C_Q + _QW
_C_SLC = _C_CMP + 2 * _KVW
_C_WIN = _C_SLC + NSA_KV_GROUPS * _PAIR
_C_GATE = _C_WIN + NSA_KV_GROUPS * _PAIR
_C_XQ = _C_GATE + V7X_LANES
_C_END = _C_XQ + XATTN_WIDTH


_SUM_ROWS = 16
_VT_ROWS = _SUM_ROWS + NSA_HEAD_DIM


def _ones_and_vt(kv):
    v_t = kv.T[NSA_HEAD_DIM:]
    return jnp.concatenate([jnp.ones((_SUM_ROWS, kv.shape[0]), F32), v_t], axis=0).astype(BF16)


def _nsa_proj_kernel(h_ref, g_ref, w_ref, q_ref, cr_ref, slc_ref, slc1_ref, win_ref, win1_ref,
                     gt_ref, xq_ref):
    hn = _rms(h_ref[0], g_ref[...]).astype(BF16)
    q = _dot(hn, w_ref[:, _C_Q:_C_CMP]) * (NSA_HEAD_DIM ** -0.5 * LOG2E)
    gates = _sigmoid(_dot(hn, w_ref[:, _C_GATE:_C_XQ]))
    s = _dot(hn, w_ref[:, _C_SLC:_C_WIN])
    for c in range(ROW_TILE // Q_TILE):
        rows = slice(c * Q_TILE, (c + 1) * Q_TILE)
        q_ref[0, c] = q[rows].T.astype(BF16)
        gt_ref[0, c] = gates[rows].T
    w = _dot(hn, w_ref[:, _C_WIN:_C_GATE])
    cr = _dot(hn, w_ref[:, _C_CMP:_C_SLC])
    for p in range(2 * _KVW // V7X_LANES):
        cr_ref[0, p] = cr[:, p * V7X_LANES:(p + 1) * V7X_LANES]
    xq_ref[0] = _dot(hn, w_ref[:, _C_XQ:_C_END]).astype(BF16)
    k_lanes = lax.broadcasted_iota(jnp.int32, (s.shape[0], _PAIR), 1) < NSA_HEAD_DIM
    for g in range(NSA_KV_GROUPS):
        sg = s[:, g * _PAIR:(g + 1) * _PAIR]
        wg = w[:, g * _PAIR:(g + 1) * _PAIR]
        slc_ref[0, g] = jnp.where(k_lanes, sg, 0.0).astype(BF16)
        win_ref[0, g] = jnp.where(k_lanes, wg, 0.0).astype(BF16)
        for c in range(ROW_TILE // KV_TILE):
            rows = slice(c * KV_TILE, (c + 1) * KV_TILE)
            slc1_ref[0, g, c] = _ones_and_vt(sg[rows])
            win1_ref[0, g, c] = _ones_and_vt(wg[rows])


def _nsa_proj(h, g, w):
    batch, seq, d = h.shape
    tm = ROW_TILE
    grp = NSA_KV_GROUPS
    row = lambda b, i: (b, i, 0)
    pair_shape = jax.ShapeDtypeStruct((batch, grp, seq, _PAIR), BF16)
    pair_spec = pl.BlockSpec((1, grp, tm, _PAIR), lambda b, i: (b, 0, i, 0))
    vt_shape = jax.ShapeDtypeStruct((batch, grp, seq // KV_TILE, _VT_ROWS, KV_TILE), BF16)
    vt_spec = pl.BlockSpec((1, grp, tm // KV_TILE, _VT_ROWS, KV_TILE), lambda b, i: (b, 0, i, 0, 0))
    return pl.pallas_call(
        _nsa_proj_kernel,
        out_shape=(jax.ShapeDtypeStruct((batch, seq // Q_TILE, _QW, Q_TILE), BF16),
                   jax.ShapeDtypeStruct((batch, 2 * _KVW // V7X_LANES, seq, V7X_LANES), F32),
                   pair_shape, vt_shape, pair_shape, vt_shape,
                   jax.ShapeDtypeStruct((batch, seq // Q_TILE, V7X_LANES, Q_TILE), F32),
                   jax.ShapeDtypeStruct((batch, seq, XATTN_WIDTH), BF16)),
        grid=(batch, seq // tm),
        in_specs=[
            pl.BlockSpec((1, tm, d), row),
            pl.BlockSpec((1, d), lambda b, i: (0, 0)),
            pl.BlockSpec((d, _C_END), lambda b, i: (0, 0)),
        ],
        out_specs=(pl.BlockSpec((1, tm // Q_TILE, _QW, Q_TILE), lambda b, i: (b, i, 0, 0)),
                   pl.BlockSpec((1, 2 * _KVW // V7X_LANES, tm, V7X_LANES), lambda b, i: (b, 0, i, 0)),
                   pair_spec, vt_spec, pair_spec, vt_spec,
                   pl.BlockSpec((1, tm // Q_TILE, V7X_LANES, Q_TILE), lambda b, i: (b, i, 0, 0)),
                   pl.BlockSpec((1, tm, XATTN_WIDTH), row)),
        compiler_params=pltpu.CompilerParams(
            dimension_semantics=("arbitrary", "arbitrary"),
            vmem_limit_bytes=_vmem_limit(48 * 1024 * 1024)),
        name="nsa_proj",
    )(h, g.reshape(1, d), w)


_CHUNKS = 512


def _compress_kernel(raw_ref, pos_ref, w1_ref, w2_ref, o_ref, h2_scr, *, transposed):
    dk = NSA_HEAD_DIM
    half = CMP_BLOCK // 2
    h1 = [jnp.zeros((_CHUNKS, CMP_HIDDEN), F32) for _ in range(NSA_KV_GROUPS)]
    h2 = [jnp.zeros((_CHUNKS, CMP_HIDDEN), F32) for _ in range(NSA_KV_GROUPS)]
    gpl = V7X_LANES // dk
    for l in range(half):
        for p in range(NSA_KV_GROUPS // gpl):
            lanes = slice(p * V7X_LANES, (p + 1) * V7X_LANES)
            rows = raw_ref[0, p, pl.ds(l, _CHUNKS, stride=CMP_STRIDE), :]
            a1 = (rows + pos_ref[0, l:l + 1, lanes]).astype(BF16)
            a2 = (rows + pos_ref[0, half + l:half + l + 1, lanes]).astype(BF16)
            for gg in range(gpl):
                g = p * gpl + gg
                h1[g] = h1[g] + _dot(a1[:, gg * dk:(gg + 1) * dk], w1_ref[0, l * dk:(l + 1) * dk, :])
                h2[g] = h2[g] + _dot(a2[:, gg * dk:(gg + 1) * dk],
                                     w1_ref[0, (half + l) * dk:(half + l + 1) * dk, :])
    h2_scr[_CHUNKS:_CHUNKS + 8, :] = jnp.zeros((8, CMP_HIDDEN), F32)
    for g in range(NSA_KV_GROUPS):
        h2_scr[0:_CHUNKS, :] = h2[g]
        hm = h1[g] + h2_scr[pl.ds(1, _CHUNKS), :]
        hm = (hm * _sigmoid(hm)).astype(BF16)
        if transposed:
            o_ref[0, g] = _dot_nt(w2_ref[0], hm).astype(BF16)
        else:
            out = _dot(hm, w2_ref[0])
            o_ref[0, g] = jnp.concatenate([out, jnp.zeros_like(out)], axis=1).astype(BF16)


def _compress(raw, pos4, w1, w2, which, transposed):
    batch, _, seq, _ = raw.shape
    grp = NSA_KV_GROUPS
    tiles = _KVW // V7X_LANES
    if transposed:
        out_shape = (batch, grp, NSA_HEAD_DIM, N_CMP_PAD)
        w2_block = (1, NSA_HEAD_DIM, CMP_HIDDEN)
    else:
        out_shape = (batch, grp, N_CMP_PAD, _PAIR)
        w2_block = (1, CMP_HIDDEN, NSA_HEAD_DIM)
    return pl.pallas_call(
        functools.partial(_compress_kernel, transposed=transposed),
        out_shape=jax.ShapeDtypeStruct(out_shape, BF16),
        grid=(batch,),
        in_specs=[
            pl.BlockSpec((1, tiles, seq, V7X_LANES), lambda b: (b, which, 0, 0)),
            pl.BlockSpec((1, CMP_BLOCK, _KVW), lambda b: (which, 0, 0)),
            pl.BlockSpec((1, CMP_BLOCK * NSA_HEAD_DIM, CMP_HIDDEN), lambda b: (which, 0, 0)),
            pl.BlockSpec(w2_block, lambda b: (0, 0, 0)),
        ],
        out_specs=pl.BlockSpec((1,) + out_shape[1:], lambda b: (b, 0, 0, 0)),
        scratch_shapes=[pltpu.VMEM((_CHUNKS + 8, CMP_HIDDEN), F32)],
        compiler_params=pltpu.CompilerParams(
            dimension_semantics=("arbitrary",),
            vmem_limit_bytes=_vmem_limit(40 * 1024 * 1024)),
        name="compress_v" if transposed else "compress_k",
    )(raw, pos4, w1, w2)


def _t5_bucket(dist):
    n = jnp.maximum(dist, 0)
    max_exact = REL_BUCKETS // 2
    nf = jnp.maximum(n, 1).astype(F32)
    large = max_exact + (jnp.log(nf / max_exact) / np.log(REL_MAX_DIST / max_exact)
                         * (REL_BUCKETS - max_exact)).astype(jnp.int32)
    large = jnp.minimum(large, REL_BUCKETS - 1)
    return jnp.where(n < max_exact, n, large)


def _table_kernel(thr_ref, rb_ref, cbt_ref, tslc_ref, twin_ref):
    tq = Q_TILE
    g = pl.program_id(0)

    def entries(dist, h, limit):
        val = jnp.full(dist.shape, rb_ref[0, h], F32)
        for k in range(1, REL_BUCKETS):
            val = jnp.where(dist >= thr_ref[k], rb_ref[k, h], val)
        val = (val - rb_ref[REL_BUCKETS - 1, h]) * LOG2E
        return jnp.where(jnp.where(dist >= 0, dist, limit) < limit, val, NEG)

    def far_entries(dist, limit):
        return jnp.where(jnp.where(dist >= 0, dist, limit) < limit, 0.0, NEG)

    def fill(ref, band, dist_of_rows, n_rows, limit, h):
        step = 64
        for r0 in range(0, n_rows, step):
            shape = (step, tq)
            dist = dist_of_rows(r0 + lax.broadcasted_iota(jnp.int32, shape, 0),
                                lax.broadcasted_iota(jnp.int32, shape, 1))
            lo, hi = dist_of_rows(r0 + step - 1, 0), dist_of_rows(r0, tq - 1)
            near = hi >= 0 and lo < REL_MAX_DIST
            if near and dist_of_rows(r0 + 1, 0) - dist_of_rows(r0, 0) == -1:
                width = 2 * tq
                z = lax.broadcasted_iota(jnp.int32, (1, width), 1)
                distinct = entries(dist_of_rows(r0, 0) + z - step, h, limit)
                tile = pltpu.roll(jnp.broadcast_to(distinct, (step, width)), width - step, 1,
                                  stride=1, stride_axis=0)
                ref[0, r0:r0 + step, band] = tile[:, 0:tq]
            else:
                ref[0, r0:r0 + step, band] = entries(dist, h, limit) if near else far_entries(dist, limit)

    no_limit = 1 << 30
    for j in range(NSA_HPG):
        h = g * NSA_HPG + j
        band = slice(j * tq, (j + 1) * tq)
        fill(cbt_ref, band, lambda u, q: q - (CMP_BLOCK - 1) - CMP_STRIDE * (u - N_CMP_PAD),
             2 * N_CMP_PAD, no_limit, h)
        fill(tslc_ref, band, lambda u, q: q - u + tq, 2 * tq, no_limit, h)
        fill(twin_ref, band, lambda u, q: q - u + WINDOW, WINDOW + tq, WINDOW, h)


def _bias_tables(rel_bias):
    tq = Q_TILE
    rows = NSA_HPG * tq
    buckets = _t5_bucket(jnp.arange(REL_MAX_DIST + 1))
    thr = jnp.sum(buckets[None, :] < jnp.arange(REL_BUCKETS)[:, None], axis=1).astype(jnp.int32)
    smem = pl.BlockSpec(memory_space=pltpu.SMEM)
    return pl.pallas_call(
        _table_kernel,
        out_shape=(jax.ShapeDtypeStruct((NSA_KV_GROUPS, 2 * N_CMP_PAD, rows), F32),
                   jax.ShapeDtypeStruct((NSA_KV_GROUPS, 2 * tq, rows), F32),
                   jax.ShapeDtypeStruct((NSA_KV_GROUPS, WINDOW + tq, rows), F32)),
        grid=(NSA_KV_GROUPS,),
        in_specs=[smem, smem],
        out_specs=(pl.BlockSpec((1, 2 * N_CMP_PAD, rows), lambda g: (g, 0, 0)),
                   pl.BlockSpec((1, 2 * tq, rows), lambda g: (g, 0, 0)),
                   pl.BlockSpec((1, WINDOW + tq, rows), lambda g: (g, 0, 0))),
        compiler_params=pltpu.CompilerParams(
            dimension_semantics=("arbitrary",),
            vmem_limit_bytes=_vmem_limit(40 * 1024 * 1024)),
        name="bias_tables",
    )(thr, rel_bias.astype(F32))


def _block_mask_rows(seq):
    pos = jnp.arange(seq, dtype=jnp.int32)[:, None]
    col = jnp.arange(N_SEL, dtype=jnp.int32)[None, :]
    onehot = jnp.where(col == pos // SEL_BLOCK, NEG, 0.0)
    return jnp.concatenate([onehot, jnp.full((Q_TILE, N_SEL), NEG)], axis=0).astype(BF16)


def _online_absorb(s, v1t, state):
    m = jnp.max(s, axis=0, keepdims=True)
    if state is not None:
        m = jnp.maximum(state[0], m)
    acc = _dot(v1t, jnp.exp2(s - m).astype(BF16))
    if state is not None:
        acc = jnp.exp2(state[0] - m) * state[1] + acc
    return m, acc


def _normalise_t(acc):
    return acc[_SUM_ROWS:] * (1.0 / acc[0:1])


def _nsa_kernel(q_ref, kc_ref, vct_ref, slc_ref, slc1_ref, win_ref, win1_ref, bm_ref, gt_ref,
                cbt_ref, tslc_ref, twin_ref, o_ref, ps_scr, lhs_scr, m_scr, acc_scr, s_scr, ocmp_scr,
                owin_scr, smax_scr):
    tq = Q_TILE
    hpg = NSA_HPG
    dk = NSA_HEAD_DIM
    seq = slc_ref.shape[2]
    g = pl.program_id(1)
    i = pl.program_id(2)
    q0 = pl.multiple_of(i * tq, tq)
    bands = [slice(j * tq, (j + 1) * tq) for j in range(hpg)]

    for j in range(hpg):
        lhs_scr[0:dk, bands[j]] = q_ref[0, 0, j * dk:(j + 1) * dk, :]
    lhs_scr[dk:dk + _SUM_ROWS, :] = jnp.ones((_SUM_ROWS, hpg * tq), BF16)
    lhs_scr[dk + _SUM_ROWS:_PAIR, :] = jnp.zeros((_PAIR - dk - _SUM_ROWS, hpg * tq), BF16)
    qs_t = lhs_scr[0:_PAIR, :]

    n_prev = WINDOW // tq
    mask_lane = lax.broadcasted_iota(jnp.int32, (1, _PAIR), 1) == dk

    def win_logits(r):
        tile = jnp.maximum(i - r, 0)
        keys = win_ref[0, 0, pl.ds(pl.multiple_of(tile * tq, tq), tq), :]
        if r:
            keys = keys + jnp.where(jnp.logical_and(mask_lane, i < r), NEG, 0.0).astype(BF16)
        rows = slice((n_prev - r) * tq, (n_prev - r + 1) * tq)
        return _dot(keys, qs_t) + twin_ref[0, rows, :]

    def win_absorb(r, s, state):
        return _online_absorb(s, win1_ref[0, 0, jnp.maximum(i - r, 0)], state)

    def compress_and_select(n_cmp):
        n_sel = n_cmp * CMP_STRIDE // SEL_BLOCK
        st = _dot(kc_ref[0, 0, 0:n_cmp, :], qs_t)
        st = st + cbt_ref[0, pl.ds(pl.multiple_of(N_CMP_PAD - (tq // CMP_STRIDE) * i, 8), n_cmp), :]
        mx = jnp.maximum(jnp.max(st, axis=0, keepdims=True), 0.1 * NEG)
        e = jnp.exp2(st - mx)
        den = jnp.sum(e, axis=0, keepdims=True)
        inv = jnp.where(den > 0.0, 1.0 / den, 0.0)
        ocmp_scr[...] = _dot(vct_ref[0, 0, :, 0:n_cmp], e.astype(BF16)) * inv
        psum = e[:, bands[0]] * inv[:, bands[0]]
        for j in range(1, hpg):
            psum = psum + e[:, bands[j]] * inv[:, bands[j]]

        ratio = SEL_BLOCK // CMP_STRIDE
        imp_tiles = []
        for c in range(tq // V7X_LANES):
            ps_scr[c, 0:8, :] = jnp.zeros((8, V7X_LANES), F32)
            ps_scr[c, 8:8 + n_cmp, :] = psum[:, c * V7X_LANES:(c + 1) * V7X_LANES]
            taps = [ps_scr[c, pl.ds(7 + r, n_sel, stride=ratio), :] for r in range(5)]
            imp_tiles.append(0.5 * taps[0] + taps[1] + taps[2] + taps[3] + 0.5 * taps[4])
        imp = jnp.concatenate(imp_tiles, axis=1)

        jrow = lax.broadcasted_iota(jnp.int32, (n_sel, tq), 0)
        cur = (q0 + lax.broadcasted_iota(jnp.int32, (n_sel, tq), 1)) // SEL_BLOCK
        taken = -3.0e38
        imp = jnp.where(jrow <= cur, imp, -SEL_FORCE)
        for forced_block in (0, cur, cur - 1):
            imp = jnp.where(jrow == forced_block, taken, imp)
        n_rounds = SEL_TOPK - 3
        n_tiles = n_prev + 1
        s_win = [win_logits(0), win_logits(1)]
        win_state, absorbed = None, 0
        for rnd in range(n_rounds):
            top = jnp.max(imp, axis=0, keepdims=True)
            first = jnp.min(jnp.where(imp == top, jrow, n_sel), axis=0, keepdims=True)
            imp = jnp.where(jrow == first, taken, imp)
            while absorbed < (rnd + 1) * n_tiles // n_rounds:
                if absorbed + 2 < n_tiles:
                    s_win.append(win_logits(absorbed + 2))
                win_state = win_absorb(absorbed, s_win[absorbed], win_state)
                absorbed += 1
        owin_scr[...] = _normalise_t(win_state[1])
        not_sel = jnp.where(imp < 0.5 * taken, 0.0, 1.0).astype(BF16)
        for j in range(hpg):
            lhs_scr[_PAIR:_PAIR + n_sel, bands[j]] = not_sel
        if n_sel < N_SEL:
            lhs_scr[_PAIR + n_sel:2 * _PAIR, :] = jnp.ones((N_SEL - n_sel, hpg * tq), BF16)

    tiles_per_variant = (seq // tq) // CMP_VARIANTS
    for v in range(CMP_VARIANTS):
        pl.when(i // tiles_per_variant == v)(
            functools.partial(compress_and_select, N_CMP_PAD * (v + 1) // CMP_VARIANTS))
    lhs = lhs_scr[...]

    def key_tile(kv_ref, tile, mask_tile):
        return jnp.concatenate([kv_ref[0, 0, pl.ds(pl.multiple_of(tile * tq, tq), tq), :],
                                bm_ref[pl.ds(pl.multiple_of(mask_tile * tq, tq), tq), :]], axis=1)

    all_masked = seq // tq

    prev = jnp.maximum(i - 1, 0)
    s_own = _dot(key_tile(slc_ref, i, i), lhs) + tslc_ref[0, tq:2 * tq, :]
    s_prev = (_dot(key_tile(slc_ref, prev, jnp.where(i == 0, all_masked, i - 1)), lhs)
              + tslc_ref[0, 0:tq, :])

    n_far = jnp.maximum(i - 1, 0)
    n_groups = (n_far + FAR_GROUP - 1) // FAR_GROUP

    def tile_or_none(kt):
        valid = kt < n_far
        return jnp.where(valid, kt, 0), jnp.where(valid, kt, all_masked)

    def stage_logits(slot, kg):
        tiles = [key_tile(slc_ref, *tile_or_none(FAR_GROUP * kg + t)) for t in range(FAR_GROUP)]
        s = _dot(jnp.concatenate(tiles, axis=0), lhs_scr[...])
        s_scr[slot] = s
        smax_scr[8 * slot:8 * slot + 1, :] = jnp.max(s, axis=0, keepdims=True)

    def absorb(slot, kg):
        m_old = m_scr[0:1, :]
        m_new = jnp.maximum(m_old, smax_scr[8 * slot:8 * slot + 1, :])
        p = jnp.exp2(s_scr[slot] - m_new)
        v1t = jnp.concatenate([slc1_ref[0, 0, tile_or_none(FAR_GROUP * kg + t)[0]]
                               for t in range(FAR_GROUP)], axis=1)
        acc_scr[...] = jnp.exp2(m_old - m_new) * acc_scr[...] + _dot(v1t, p.astype(BF16))
        m_scr[0:1, :] = m_new

    def far_trip(base, groups):
        for t in range(groups):
            stage_logits((t + 1) % 2, base + t + 1)
            absorb(t % 2, base + t)

    stage_logits(0, 0)
    state = _online_absorb(s_own, slc1_ref[0, 0, i], None)
    m, acc = _online_absorb(s_prev, slc1_ref[0, 0, prev], state)
    m_scr[0:1, :] = m
    acc_scr[...] = acc
    n_long = n_groups // 4

    def long_body(k, carry):
        far_trip(4 * k, 4)
        return carry

    lax.fori_loop(0, n_long, long_body, 0)
    done = 4 * n_long
    pair = (n_groups - done) // 2

    @pl.when(pair > 0)
    def _():
        far_trip(done, 2)

    @pl.when(n_groups - done - 2 * pair > 0)
    def _():
        far_trip(done + 2 * pair, 1)

    o_slc_t = _normalise_t(acc_scr[...])

    merged = []
    for j in range(hpg):
        h = g * hpg + j
        gate = [gt_ref[0, 0, pl.ds(br * NSA_HEADS + h, 1), :] for br in range(3)]
        merged.append(gate[0] * ocmp_scr[:, bands[j]] + gate[1] * o_slc_t[:, bands[j]]
                      + gate[2] * owin_scr[:, bands[j]])
    o_ref[0] = jnp.concatenate(merged, axis=0).T.astype(BF16)


def _nsa_attention(q, kc, vct, slc, slc1, win, win1, block_mask, gates, cbt, tslc, twin):
    batch, _, seq, _ = slc.shape
    grp = NSA_KV_GROUPS
    tq = Q_TILE
    rows = NSA_HPG * tq
    per_group = lambda b, g, i: (b, g, 0, 0)
    per_group_t = lambda b, g, i: (b, g, 0, 0, 0)
    return pl.pallas_call(
        _nsa_kernel,
        out_shape=jax.ShapeDtypeStruct((batch, seq, _QW), BF16),
        grid=(batch, grp, seq // tq),
        in_specs=[
            pl.BlockSpec((1, 1, NSA_HPG * NSA_HEAD_DIM, tq), lambda b, g, i: (b, i, g, 0)),
            pl.BlockSpec((1, 1, N_CMP_PAD, _PAIR), per_group),
            pl.BlockSpec((1, 1, NSA_HEAD_DIM, N_CMP_PAD), per_group),
            pl.BlockSpec((1, 1, seq, _PAIR), per_group),
            pl.BlockSpec((1, 1, seq // tq, _VT_ROWS, tq), per_group_t),
            pl.BlockSpec((1, 1, seq, _PAIR), per_group),
            pl.BlockSpec((1, 1, seq // tq, _VT_ROWS, tq), per_group_t),
            _resident((seq + tq, N_SEL), lambda b, g, i: (0, 0)),
            pl.BlockSpec((1, 1, V7X_LANES, tq), lambda b, g, i: (b, i, 0, 0)),
            pl.BlockSpec((1, 2 * N_CMP_PAD, rows), lambda b, g, i: (g, 0, 0)),
            pl.BlockSpec((1, 2 * tq, rows), lambda b, g, i: (g, 0, 0)),
            pl.BlockSpec((1, WINDOW + tq, rows), lambda b, g, i: (g, 0, 0)),
        ],
        out_specs=pl.BlockSpec((1, tq, NSA_HPG * NSA_HEAD_DIM), lambda b, g, i: (b, i, g)),
        scratch_shapes=[pltpu.VMEM((tq // V7X_LANES, N_CMP_PAD + 16, V7X_LANES), F32),
                        pltpu.VMEM((2 * _PAIR, rows), BF16),
                        pltpu.VMEM((8, rows), F32),
                        pltpu.VMEM((_VT_ROWS, rows), F32),
                        pltpu.VMEM((2, FAR_GROUP * tq, rows), F32),
                        pltpu.VMEM((NSA_HEAD_DIM, rows), F32),
                        pltpu.VMEM((NSA_HEAD_DIM, rows), F32),
                        pltpu.VMEM((16, rows), F32)],
        compiler_params=pltpu.CompilerParams(
            dimension_semantics=("arbitrary", "arbitrary", "arbitrary"),
            vmem_limit_bytes=_vmem_limit(52 * 1024 * 1024)),
        name="nsa_attention",
    )(q, kc, vct, slc, slc1, win, win1, block_mask, gates, cbt, tslc, twin)


def _nsa_weight(w):
    d = w.shape[0]
    q = w[:, :_QW]
    kv = w[:, _QW:_QW + 6 * _KVW].reshape(d, 6, NSA_KV_GROUPS, NSA_HEAD_DIM)
    cmp_raw = kv[:, 0:2].reshape(d, 2 * _KVW)
    slc = jnp.stack([kv[:, 2], kv[:, 3]], axis=2).reshape(d, NSA_KV_GROUPS * _PAIR)
    win = jnp.stack([kv[:, 4], kv[:, 5]], axis=2).reshape(d, NSA_KV_GROUPS * _PAIR)
    gates = w[:, _QW + 6 * _KVW:_QW + 6 * _KVW + 3 * NSA_HEADS]
    gates = jnp.pad(gates, ((0, 0), (0, V7X_LANES - 3 * NSA_HEADS)))
    xq = w[:, _QW + 6 * _KVW + 3 * NSA_HEADS:]
    return jnp.concatenate([q, cmp_raw, slc, win, gates, xq], axis=1).astype(BF16)


def kernel(x, mem, norm_mix_g, norm_mem_g, norm_ffn_g, norm_final_g, w_in_conv, conv_w, w_in_nsa,
           cmp_pos_k, cmp_w1_k, cmp_w2_k, cmp_pos_v, cmp_w1_v, cmp_w2_v, rel_bias, w_mem_kv, w_out,
           w_ffn_in, w_ffn_out):
    seq = x.shape[1]
    assert seq % ROW_TILE == 0 and seq % Q_TILE == 0 and x.shape[2] == D_MODEL
    assert seq == CMP_STRIDE * _CHUNKS and seq == SEL_BLOCK * N_SEL and Q_TILE == KV_TILE
    memkv = _memkv(mem, norm_mem_g, w_mem_kv.astype(BF16))
    w_out_b = w_out.astype(BF16)
    w_ffn_in_b = w_ffn_in.astype(BF16)
    w_ffn_out_b = w_ffn_out.astype(BF16)

    tok, xq = _conv_front(x, norm_mix_g[0], w_in_conv[0].astype(BF16), conv_w[0])
    h = _mixout(x, tok, xq, memkv, w_out_b, 0)
    h = _ffn(h, norm_ffn_g[0], w_ffn_in_b, w_ffn_out_b, norm_final_g, 0, final=False)

    q, cmp_raw, slc, slc1, win, win1, gates, xq = _nsa_proj(h, norm_mix_g[1], _nsa_weight(w_in_nsa[0]))
    pos4 = jnp.stack([jnp.tile(cmp_pos_k[0], (1, NSA_KV_GROUPS)),
                      jnp.tile(cmp_pos_v[0], (1, NSA_KV_GROUPS))])
    w1 = jnp.stack([cmp_w1_k[0], cmp_w1_v[0]]).astype(BF16)
    kc = _compress(cmp_raw, pos4, w1, cmp_w2_k[0].astype(BF16)[None], 0, transposed=False)
    vct = _compress(cmp_raw, pos4, w1, cmp_w2_v[0].T.astype(BF16)[None], 1, transposed=True)
    tok = _nsa_attention(q, kc, vct, slc, slc1, win, win1, _block_mask_rows(seq), gates,
                         *_bias_tables(rel_bias))
    h = _mixout(h, tok, xq, memkv, w_out_b, 1)
    return _ffn(h, norm_ffn_g[1], w_ffn_in_b, w_ffn_out_b, norm_final_g, 1, final=True)
```

```python
import functools

import numpy as np
import jax
import jax.numpy as jnp
from jax import lax
from jax.experimental import pallas as pl
from jax.experimental.pallas import tpu as pltpu

F32 = jnp.float32
BF16 = jnp.bfloat16

D_MODEL = 1024
N_MEM = 256
CONV_WIDTH = 3
NSA_HEADS = 16
NSA_HEAD_DIM = 64
NSA_KV_GROUPS = 4
NSA_HPG = NSA_HEADS // NSA_KV_GROUPS
CMP_BLOCK = 32
CMP_STRIDE = 16
CMP_HIDDEN = 2 * NSA_HEAD_DIM
SEL_BLOCK = 64
SEL_TOPK = 16
SEL_FORCE = 1.0e4
WINDOW = 512
XATTN_HEADS = 4
XATTN_HEAD_DIM = 128
XATTN_WIDTH = XATTN_HEADS * XATTN_HEAD_DIM
FFN_HIDDEN = 2816
REL_BUCKETS = 32
REL_MAX_DIST = 128
RMS_EPS = 1e-6
NEG = -1e30
LOG2E = 1.4426950408889634

V7X_LANES = 128
V7X_VMEM_BYTES = 64 * 1024 * 1024

ROW_TILE = 512
CONV_COLS = 256
FFN_ROW_TILE = 512
FFN_CHUNK = 256
Q_TILE = 256
KV_TILE = 256
FAR_GROUP = 2
CMP_VARIANTS = 8
N_CMP_PAD = 512
N_SEL = 128

_NT = (((1,), (1,)), ((), ()))


def _vmem_limit(nbytes):
    return int(min(max(nbytes, 16 * 1024 * 1024), V7X_VMEM_BYTES - 8 * 1024 * 1024))


def _rms(x, g):
    ms = jnp.mean(x * x, axis=-1, keepdims=True)
    return x * lax.rsqrt(ms + RMS_EPS) * g


def _dot(a, b):
    return jnp.dot(a, b, preferred_element_type=F32)


def _dot_nt(a, b):
    return lax.dot_general(a, b, _NT, preferred_element_type=F32)


def _sigmoid(x):
    return 1.0 / (1.0 + jnp.exp(-x))


def _resident(block_shape, index_map):
    return pl.BlockSpec(block_shape, index_map, pipeline_mode=pl.Buffered(1))


def _memkv_kernel(mem_ref, g_ref, w_ref, o_ref):
    xn = _rms(mem_ref[0], g_ref[0]).astype(BF16)
    o_ref[0, 0] = _dot(xn, w_ref[0]).astype(BF16)


def _memkv(mem, gains, w):
    depth, batch = w.shape[0], mem.shape[0]
    return pl.pallas_call(
        _memkv_kernel,
        out_shape=jax.ShapeDtypeStruct((depth, batch, N_MEM, 2 * XATTN_WIDTH), BF16),
        grid=(depth, batch),
        in_specs=[
            pl.BlockSpec((1, N_MEM, D_MODEL), lambda l, b: (b, 0, 0)),
            pl.BlockSpec((1, 1, D_MODEL), lambda l, b: (l, 0, 0)),
            pl.BlockSpec((1, D_MODEL, 2 * XATTN_WIDTH), lambda l, b: (l, 0, 0)),
        ],
        out_specs=pl.BlockSpec((1, 1, N_MEM, 2 * XATTN_WIDTH), lambda l, b: (l, b, 0, 0)),
        compiler_params=pltpu.CompilerParams(
            dimension_semantics=("arbitrary", "arbitrary"),
            vmem_limit_bytes=_vmem_limit(24 * 1024 * 1024)),
        name="memkv",
    )(mem, gains.reshape(depth, 1, D_MODEL), w)


def _conv_kernel(h_ref, hp_ref, g_ref, w_ref, cw_ref, tok_ref, xq_ref):
    d = D_MODEL
    i = pl.program_id(1)
    g = g_ref[...]
    hn = _rms(h_ref[0], g).astype(BF16)
    hpn = _rms(hp_ref[0], g).astype(BF16)
    cw = cw_ref[...]
    row = lax.broadcasted_iota(jnp.int32, (h_ref.shape[1], CONV_COLS), 0)

    def projections(c):
        lo = c * CONV_COLS
        col = lambda part: w_ref[:, part * d + lo:part * d + lo + CONV_COLS]
        vp = jnp.where(i > 0, _dot(hpn, col(1)) * _dot(hpn, col(2)), 0.0)
        return _dot(hn, col(0)), _dot(hn, col(1)) * _dot(hn, col(2)), vp

    n_chunks = d // CONV_COLS
    ready = projections(0)
    for c in range(n_chunks):
        u_b, v, vp = ready
        if c + 1 < n_chunks:
            ready = projections(c + 1)
        else:
            xq_ref[0] = _dot(hn, w_ref[:, 3 * d:]).astype(BF16)
        cols = slice(c * CONV_COLS, (c + 1) * CONV_COLS)
        v1 = jnp.where(row == 0, vp[7:8], pltpu.roll(v, 1, axis=0))
        v2 = jnp.where(row == 0, vp[6:7], jnp.where(row == 1, vp[7:8], pltpu.roll(v, 2, axis=0)))
        conv = v2 * cw[0:1, cols] + v1 * cw[1:2, cols] + v * cw[2:3, cols]
        tok_ref[0, :, cols] = (u_b * conv).astype(BF16)


def _conv_front(h, g, w, cw):
    batch, seq, d = h.shape
    tm = ROW_TILE
    width = w.shape[1]
    halo = 8
    return pl.pallas_call(
        _conv_kernel,
        out_shape=(jax.ShapeDtypeStruct((batch, seq, d), BF16),
                   jax.ShapeDtypeStruct((batch, seq, XATTN_WIDTH), BF16)),
        grid=(batch, seq // tm),
        in_specs=[
            pl.BlockSpec((1, tm, d), lambda b, i: (b, i, 0)),
            pl.BlockSpec((1, halo, d), lambda b, i: (b, jnp.maximum(i * (tm // halo) - 1, 0), 0)),
            pl.BlockSpec((1, d), lambda b, i: (0, 0)),
            pl.BlockSpec((d, width), lambda b, i: (0, 0)),
            pl.BlockSpec((CONV_WIDTH, d), lambda b, i: (0, 0)),
        ],
        out_specs=(pl.BlockSpec((1, tm, d), lambda b, i: (b, i, 0)),
                   pl.BlockSpec((1, tm, XATTN_WIDTH), lambda b, i: (b, i, 0))),
        compiler_params=pltpu.CompilerParams(
            dimension_semantics=("arbitrary", "arbitrary"),
            vmem_limit_bytes=_vmem_limit(48 * 1024 * 1024)),
        name="conv_front",
    )(h, h, g.reshape(1, d), w, cw)


def _mixout_kernel(h_ref, tok_ref, xq_ref, kv_ref, w_ref, o_ref):
    d = D_MODEL
    xq = xq_ref[0]
    kv = kv_ref[0, 0]
    scale = XATTN_HEAD_DIM ** -0.5 * LOG2E
    heads = [slice(hh * XATTN_HEAD_DIM, (hh + 1) * XATTN_HEAD_DIM) for hh in range(XATTN_HEADS)]
    logits = [_dot_nt(xq[:, hd], kv[:, hd]) for hd in heads]
    acc = h_ref[0] + _dot(tok_ref[0], w_ref[0, 0:d])
    outs = []
    for hd, s in zip(heads, logits):
        s = s * scale
        e = jnp.exp2(s - jnp.max(s, axis=-1, keepdims=True))
        p = e / jnp.sum(e, axis=-1, keepdims=True)
        outs.append(_dot(p.astype(BF16), kv[:, XATTN_WIDTH + hd.start:XATTN_WIDTH + hd.stop]))
    mo = jnp.concatenate(outs, axis=-1).astype(BF16)
    o_ref[0] = acc + _dot(mo, w_ref[0, d:])


def _mixout(h, tok, xq, kv, w, layer):
    batch, seq, d = h.shape
    tm = ROW_TILE
    return pl.pallas_call(
        _mixout_kernel,
        out_shape=jax.ShapeDtypeStruct((batch, seq, d), F32),
        grid=(batch, seq // tm),
        in_specs=[
            pl.BlockSpec((1, tm, d), lambda b, i: (b, i, 0)),
            pl.BlockSpec((1, tm, d), lambda b, i: (b, i, 0)),
            pl.BlockSpec((1, tm, XATTN_WIDTH), lambda b, i: (b, i, 0)),
            pl.BlockSpec((1, 1, N_MEM, 2 * XATTN_WIDTH), lambda b, i: (layer, b, 0, 0)),
            pl.BlockSpec((1, d + XATTN_WIDTH, d), lambda b, i: (layer, 0, 0)),
        ],
        out_specs=pl.BlockSpec((1, tm, d), lambda b, i: (b, i, 0)),
        compiler_params=pltpu.CompilerParams(
            dimension_semantics=("arbitrary", "arbitrary"),
            vmem_limit_bytes=_vmem_limit(40 * 1024 * 1024)),
        name="mixout",
    )(h, tok, xq, kv, w)


def _ffn_kernel(h_ref, g_ref, w1_ref, w2_ref, gf_ref, o_ref, *, final):
    h = h_ref[0]
    hn = _rms(h, g_ref[...]).astype(BF16)
    acc = h
    chunks = [c * FFN_CHUNK for c in range(FFN_HIDDEN // FFN_CHUNK)]
    gate_up = [(_dot(hn, w1_ref[0, :, lo:lo + FFN_CHUNK]),
                _dot(hn, w1_ref[0, :, FFN_HIDDEN + lo:FFN_HIDDEN + lo + FFN_CHUNK]))
               for lo in chunks[:1]]
    for c, lo in enumerate(chunks):
        if c + 1 < len(chunks):
            nxt = chunks[c + 1]
            gate_up.append((_dot(hn, w1_ref[0, :, nxt:nxt + FFN_CHUNK]),
                            _dot(hn, w1_ref[0, :, FFN_HIDDEN + nxt:FFN_HIDDEN + nxt + FFN_CHUNK])))
        a, u = gate_up[c]
        act = (a * _sigmoid(a) * u).astype(BF16)
        acc = acc + _dot(act, w2_ref[0, lo:lo + FFN_CHUNK])
    if final:
        acc = _rms(acc, gf_ref[...])
    o_ref[0] = acc


def _ffn(h, g, w1, w2, gf, layer, final):
    batch, seq, d = h.shape
    tm = FFN_ROW_TILE
    return pl.pallas_call(
        functools.partial(_ffn_kernel, final=final),
        out_shape=jax.ShapeDtypeStruct((batch, seq, d), F32),
        grid=(batch, seq // tm),
        in_specs=[
            pl.BlockSpec((1, tm, d), lambda b, i: (b, i, 0)),
            pl.BlockSpec((1, d), lambda b, i: (0, 0)),
            pl.BlockSpec((1, d, 2 * FFN_HIDDEN), lambda b, i: (layer, 0, 0)),
            pl.BlockSpec((1, FFN_HIDDEN, d), lambda b, i: (layer, 0, 0)),
            pl.BlockSpec((1, d), lambda b, i: (0, 0)),
        ],
        out_specs=pl.BlockSpec((1, tm, d), lambda b, i: (b, i, 0)),
        compiler_params=pltpu.CompilerParams(
            dimension_semantics=("arbitrary", "arbitrary"),
            vmem_limit_bytes=_vmem_limit(52 * 1024 * 1024)),
        name="ffn",
    )(h, g.reshape(1, d), w1, w2, gf.reshape(1, d))


_QW = NSA_HEADS * NSA_HEAD_DIM
_KVW = NSA_KV_GROUPS * NSA_HEAD_DIM
_PAIR = 2 * NSA_HEAD_DIM
_C_Q = 0
_C_CMP = _C_Q + _QW
_C_SLC = _C_CMP + 2 * _KVW
_C_WIN = _C_SLC + NSA_KV_GROUPS * _PAIR
_C_GATE = _C_WIN + NSA_KV_GROUPS * _PAIR
_C_XQ = _C_GATE + V7X_LANES
_C_END = _C_XQ + XATTN_WIDTH


_SUM_ROWS = 16
_VT_ROWS = _SUM_ROWS + NSA_HEAD_DIM


def _ones_and_vt(kv):
    v_t = kv.T[NSA_HEAD_DIM:]
    return jnp.concatenate([jnp.ones((_SUM_ROWS, kv.shape[0]), F32), v_t], axis=0).astype(BF16)


def _nsa_proj_kernel(h_ref, g_ref, w_ref, q_ref, cr_ref, slc_ref, slc1_ref, win_ref, win1_ref,
                     gt_ref, xq_ref):
    hn = _rms(h_ref[0], g_ref[...]).astype(BF16)
    q = _dot(hn, w_ref[:, _C_Q:_C_CMP]) * (NSA_HEAD_DIM ** -0.5 * LOG2E)
    gates = _sigmoid(_dot(hn, w_ref[:, _C_GATE:_C_XQ]))
    s = _dot(hn, w_ref[:, _C_SLC:_C_WIN])
    for c in range(ROW_TILE // Q_TILE):
        rows = slice(c * Q_TILE, (c + 1) * Q_TILE)
        q_ref[0, c] = q[rows].T.astype(BF16)
        gt_ref[0, c] = gates[rows].T
    w = _dot(hn, w_ref[:, _C_WIN:_C_GATE])
    cr = _dot(hn, w_ref[:, _C_CMP:_C_SLC])
    for p in range(2 * _KVW // V7X_LANES):
        cr_ref[0, p] = cr[:, p * V7X_LANES:(p + 1) * V7X_LANES]
    xq_ref[0] = _dot(hn, w_ref[:, _C_XQ:_C_END]).astype(BF16)
    k_lanes = lax.broadcasted_iota(jnp.int32, (s.shape[0], _PAIR), 1) < NSA_HEAD_DIM
    for g in range(NSA_KV_GROUPS):
        sg = s[:, g * _PAIR:(g + 1) * _PAIR]
        wg = w[:, g * _PAIR:(g + 1) * _PAIR]
        slc_ref[0, g] = jnp.where(k_lanes, sg, 0.0).astype(BF16)
        win_ref[0, g] = jnp.where(k_lanes, wg, 0.0).astype(BF16)
        for c in range(ROW_TILE // KV_TILE):
            rows = slice(c * KV_TILE, (c + 1) * KV_TILE)
            slc1_ref[0, g, c] = _ones_and_vt(sg[rows])
            win1_ref[0, g, c] = _ones_and_vt(wg[rows])


def _nsa_proj(h, g, w):
    batch, seq, d = h.shape
    tm = ROW_TILE
    grp = NSA_KV_GROUPS
    row = lambda b, i: (b, i, 0)
    pair_shape = jax.ShapeDtypeStruct((batch, grp, seq, _PAIR), BF16)
    pair_spec = pl.BlockSpec((1, grp, tm, _PAIR), lambda b, i: (b, 0, i, 0))
    vt_shape = jax.ShapeDtypeStruct((batch, grp, seq // KV_TILE, _VT_ROWS, KV_TILE), BF16)
    vt_spec = pl.BlockSpec((1, grp, tm // KV_TILE, _VT_ROWS, KV_TILE), lambda b, i: (b, 0, i, 0, 0))
    return pl.pallas_call(
        _nsa_proj_kernel,
        out_shape=(jax.ShapeDtypeStruct((batch, seq // Q_TILE, _QW, Q_TILE), BF16),
                   jax.ShapeDtypeStruct((batch, 2 * _KVW // V7X_LANES, seq, V7X_LANES), F32),
                   pair_shape, vt_shape, pair_shape, vt_shape,
                   jax.ShapeDtypeStruct((batch, seq // Q_TILE, V7X_LANES, Q_TILE), F32),
                   jax.ShapeDtypeStruct((batch, seq, XATTN_WIDTH), BF16)),
        grid=(batch, seq // tm),
        in_specs=[
            pl.BlockSpec((1, tm, d), row),
            pl.BlockSpec((1, d), lambda b, i: (0, 0)),
            pl.BlockSpec((d, _C_END), lambda b, i: (0, 0)),
        ],
        out_specs=(pl.BlockSpec((1, tm // Q_TILE, _QW, Q_TILE), lambda b, i: (b, i, 0, 0)),
                   pl.BlockSpec((1, 2 * _KVW // V7X_LANES, tm, V7X_LANES), lambda b, i: (b, 0, i, 0)),
                   pair_spec, vt_spec, pair_spec, vt_spec,
                   pl.BlockSpec((1, tm // Q_TILE, V7X_LANES, Q_TILE), lambda b, i: (b, i, 0, 0)),
                   pl.BlockSpec((1, tm, XATTN_WIDTH), row)),
        compiler_params=pltpu.CompilerParams(
            dimension_semantics=("arbitrary", "arbitrary"),
            vmem_limit_bytes=_vmem_limit(48 * 1024 * 1024)),
        name="nsa_proj",
    )(h, g.reshape(1, d), w)


_CHUNKS = 512


def _compress_kernel(raw_ref, pos_ref, w1_ref, w2_ref, o_ref, h2_scr, *, transposed):
    dk = NSA_HEAD_DIM
    half = CMP_BLOCK // 2
    h1 = [jnp.zeros((_CHUNKS, CMP_HIDDEN), F32) for _ in range(NSA_KV_GROUPS)]
    h2 = [jnp.zeros((_CHUNKS, CMP_HIDDEN), F32) for _ in range(NSA_KV_GROUPS)]
    gpl = V7X_LANES // dk
    for l in range(half):
        for p in range(NSA_KV_GROUPS // gpl):
            lanes = slice(p * V7X_LANES, (p + 1) * V7X_LANES)
            rows = raw_ref[0, p, pl.ds(l, _CHUNKS, stride=CMP_STRIDE), :]
            a1 = (rows + pos_ref[0, l:l + 1, lanes]).astype(BF16)
            a2 = (rows + pos_ref[0, half + l:half + l + 1, lanes]).astype(BF16)
            for gg in range(gpl):
                g = p * gpl + gg
                h1[g] = h1[g] + _dot(a1[:, gg * dk:(gg + 1) * dk], w1_ref[0, l * dk:(l + 1) * dk, :])
                h2[g] = h2[g] + _dot(a2[:, gg * dk:(gg + 1) * dk],
                                     w1_ref[0, (half + l) * dk:(half + l + 1) * dk, :])
    h2_scr[_CHUNKS:_CHUNKS + 8, :] = jnp.zeros((8, CMP_HIDDEN), F32)
    for g in range(NSA_KV_GROUPS):
        h2_scr[0:_CHUNKS, :] = h2[g]
        hm = h1[g] + h2_scr[pl.ds(1, _CHUNKS), :]
        hm = (hm * _sigmoid(hm)).astype(BF16)
        if transposed:
            o_ref[0, g] = _dot_nt(w2_ref[0], hm).astype(BF16)
        else:
            out = _dot(hm, w2_ref[0])
            o_ref[0, g] = jnp.concatenate([out, jnp.zeros_like(out)], axis=1).astype(BF16)


def _compress(raw, pos4, w1, w2, which, transposed):
    batch, _, seq, _ = raw.shape
    grp = NSA_KV_GROUPS
    tiles = _KVW // V7X_LANES
    if transposed:
        out_shape = (batch, grp, NSA_HEAD_DIM, N_CMP_PAD)
        w2_block = (1, NSA_HEAD_DIM, CMP_HIDDEN)
    else:
        out_shape = (batch, grp, N_CMP_PAD, _PAIR)
        w2_block = (1, CMP_HIDDEN, NSA_HEAD_DIM)
    return pl.pallas_call(
        functools.partial(_compress_kernel, transposed=transposed),
        out_shape=jax.ShapeDtypeStruct(out_shape, BF16),
        grid=(batch,),
        in_specs=[
            pl.BlockSpec((1, tiles, seq, V7X_LANES), lambda b: (b, which, 0, 0)),
            pl.BlockSpec((1, CMP_BLOCK, _KVW), lambda b: (which, 0, 0)),
            pl.BlockSpec((1, CMP_BLOCK * NSA_HEAD_DIM, CMP_HIDDEN), lambda b: (which, 0, 0)),
            pl.BlockSpec(w2_block, lambda b: (0, 0, 0)),
        ],
        out_specs=pl.BlockSpec((1,) + out_shape[1:], lambda b: (b, 0, 0, 0)),
        scratch_shapes=[pltpu.VMEM((_CHUNKS + 8, CMP_HIDDEN), F32)],
        compiler_params=pltpu.CompilerParams(
            dimension_semantics=("arbitrary",),
            vmem_limit_bytes=_vmem_limit(40 * 1024 * 1024)),
        name="compress_v" if transposed else "compress_k",
    )(raw, pos4, w1, w2)


def _t5_bucket(dist):
    n = jnp.maximum(dist, 0)
    max_exact = REL_BUCKETS // 2
    nf = jnp.maximum(n, 1).astype(F32)
    large = max_exact + (jnp.log(nf / max_exact) / np.log(REL_MAX_DIST / max_exact)
                         * (REL_BUCKETS - max_exact)).astype(jnp.int32)
    large = jnp.minimum(large, REL_BUCKETS - 1)
    return jnp.where(n < max_exact, n, large)


def _table_kernel(thr_ref, rb_ref, cbt_ref, tslc_ref, twin_ref):
    tq = Q_TILE
    g = pl.program_id(0)

    def entries(dist, h, limit):
        val = jnp.full(dist.shape, rb_ref[0, h], F32)
        for k in range(1, REL_BUCKETS):
            val = jnp.where(dist >= thr_ref[k], rb_ref[k, h], val)
        val = (val - rb_ref[REL_BUCKETS - 1, h]) * LOG2E
        return jnp.where(jnp.where(dist >= 0, dist, limit) < limit, val, NEG)

    def far_entries(dist, limit):
        return jnp.where(jnp.where(dist >= 0, dist, limit) < limit, 0.0, NEG)

    def fill(ref, band, dist_of_rows, n_rows, limit, h):
        step = 64
        for r0 in range(0, n_rows, step):
            shape = (step, tq)
            dist = dist_of_rows(r0 + lax.broadcasted_iota(jnp.int32, shape, 0),
                                lax.broadcasted_iota(jnp.int32, shape, 1))
            lo, hi = dist_of_rows(r0 + step - 1, 0), dist_of_rows(r0, tq - 1)
            near = hi >= 0 and lo < REL_MAX_DIST
            if near and dist_of_rows(r0 + 1, 0) - dist_of_rows(r0, 0) == -1:
                width = 2 * tq
                z = lax.broadcasted_iota(jnp.int32, (1, width), 1)
                distinct = entries(dist_of_rows(r0, 0) + z - step, h, limit)
                tile = pltpu.roll(jnp.broadcast_to(distinct, (step, width)), width - step, 1,
                                  stride=1, stride_axis=0)
                ref[0, r0:r0 + step, band] = tile[:, 0:tq]
            else:
                ref[0, r0:r0 + step, band] = entries(dist, h, limit) if near else far_entries(dist, limit)

    no_limit = 1 << 30
    for j in range(NSA_HPG):
        h = g * NSA_HPG + j
        band = slice(j * tq, (j + 1) * tq)
        fill(cbt_ref, band, lambda u, q: q - (CMP_BLOCK - 1) - CMP_STRIDE * (u - N_CMP_PAD),
             2 * N_CMP_PAD, no_limit, h)
        fill(tslc_ref, band, lambda u, q: q - u + tq, 2 * tq, no_limit, h)
        fill(twin_ref, band, lambda u, q: q - u + WINDOW, WINDOW + tq, WINDOW, h)


def _bias_tables(rel_bias):
    tq = Q_TILE
    rows = NSA_HPG * tq
    buckets = _t5_bucket(jnp.arange(REL_MAX_DIST + 1))
    thr = jnp.sum(buckets[None, :] < jnp.arange(REL_BUCKETS)[:, None], axis=1).astype(jnp.int32)
    smem = pl.BlockSpec(memory_space=pltpu.SMEM)
    return pl.pallas_call(
        _table_kernel,
        out_shape=(jax.ShapeDtypeStruct((NSA_KV_GROUPS, 2 * N_CMP_PAD, rows), F32),
                   jax.ShapeDtypeStruct((NSA_KV_GROUPS, 2 * tq, rows), F32),
                   jax.ShapeDtypeStruct((NSA_KV_GROUPS, WINDOW + tq, rows), F32)),
        grid=(NSA_KV_GROUPS,),
        in_specs=[smem, smem],
        out_specs=(pl.BlockSpec((1, 2 * N_CMP_PAD, rows), lambda g: (g, 0, 0)),
                   pl.BlockSpec((1, 2 * tq, rows), lambda g: (g, 0, 0)),
                   pl.BlockSpec((1, WINDOW + tq, rows), lambda g: (g, 0, 0))),
        compiler_params=pltpu.CompilerParams(
            dimension_semantics=("arbitrary",),
            vmem_limit_bytes=_vmem_limit(40 * 1024 * 1024)),
        name="bias_tables",
    )(thr, rel_bias.astype(F32))


def _block_mask_rows(seq):
    pos = jnp.arange(seq, dtype=jnp.int32)[:, None]
    col = jnp.arange(N_SEL, dtype=jnp.int32)[None, :]
    onehot = jnp.where(col == pos // SEL_BLOCK, NEG, 0.0)
    return jnp.concatenate([onehot, jnp.full((Q_TILE, N_SEL), NEG)], axis=0).astype(BF16)


def _online_absorb(s, v1t, state):
    m = jnp.max(s, axis=0, keepdims=True)
    if state is not None:
        m = jnp.maximum(state[0], m)
    acc = _dot(v1t, jnp.exp2(s - m).astype(BF16))
    if state is not None:
        acc = jnp.exp2(state[0] - m) * state[1] + acc
    return m, acc


def _normalise_t(acc):
    return acc[_SUM_ROWS:] * (1.0 / acc[0:1])


def _nsa_kernel(q_ref, kc_ref, vct_ref, slc_ref, slc1_ref, win_ref, win1_ref, bm_ref, gt_ref,
                cbt_ref, tslc_ref, twin_ref, o_ref, ps_scr, lhs_scr, m_scr, acc_scr, s_scr, ocmp_scr,
                owin_scr, smax_scr):
    tq = Q_TILE
    hpg = NSA_HPG
    dk = NSA_HEAD_DIM
    seq = slc_ref.shape[2]
    g = pl.program_id(1)
    i = pl.program_id(2)
    q0 = pl.multiple_of(i * tq, tq)
    bands = [slice(j * tq, (j + 1) * tq) for j in range(hpg)]

    for j in range(hpg):
        lhs_scr[0:dk, bands[j]] = q_ref[0, 0, j * dk:(j + 1) * dk, :]
    lhs_scr[dk:dk + _SUM_ROWS, :] = jnp.ones((_SUM_ROWS, hpg * tq), BF16)
    lhs_scr[dk + _SUM_ROWS:_PAIR, :] = jnp.zeros((_PAIR - dk - _SUM_ROWS, hpg * tq), BF16)
    qs_t = lhs_scr[0:_PAIR, :]

    n_prev = WINDOW // tq
    mask_lane = lax.broadcasted_iota(jnp.int32, (1, _PAIR), 1) == dk

    def win_logits(r):
        tile = jnp.maximum(i - r, 0)
        keys = win_ref[0, 0, pl.ds(pl.multiple_of(tile * tq, tq), tq), :]
        if r:
            keys = keys + jnp.where(jnp.logical_and(mask_lane, i < r), NEG, 0.0).astype(BF16)
        rows = slice((n_prev - r) * tq, (n_prev - r + 1) * tq)
        return _dot(keys, qs_t) + twin_ref[0, rows, :]

    def win_absorb(r, s, state):
        return _online_absorb(s, win1_ref[0, 0, jnp.maximum(i - r, 0)], state)

    def compress_and_select(n_cmp):
        n_sel = n_cmp * CMP_STRIDE // SEL_BLOCK
        st = _dot(kc_ref[0, 0, 0:n_cmp, :], qs_t)
        st = st + cbt_ref[0, pl.ds(pl.multiple_of(N_CMP_PAD - (tq // CMP_STRIDE) * i, 8), n_cmp), :]
        mx = jnp.maximum(jnp.max(st, axis=0, keepdims=True), 0.1 * NEG)
        e = jnp.exp2(st - mx)
        den = jnp.sum(e, axis=0, keepdims=True)
        inv = jnp.where(den > 0.0, 1.0 / den, 0.0)
        ocmp_scr[...] = _dot(vct_ref[0, 0, :, 0:n_cmp], e.astype(BF16)) * inv
        psum = e[:, bands[0]] * inv[:, bands[0]]
        for j in range(1, hpg):
            psum = psum + e[:, bands[j]] * inv[:, bands[j]]

        ratio = SEL_BLOCK // CMP_STRIDE
        imp_tiles = []
        for c in range(tq // V7X_LANES):
            ps_scr[c, 0:8, :] = jnp.zeros((8, V7X_LANES), F32)
            ps_scr[c, 8:8 + n_cmp, :] = psum[:, c * V7X_LANES:(c + 1) * V7X_LANES]
            taps = [ps_scr[c, pl.ds(7 + r, n_sel, stride=ratio), :] for r in range(5)]
            imp_tiles.append(0.5 * taps[0] + taps[1] + taps[2] + taps[3] + 0.5 * taps[4])
        imp = jnp.concatenate(imp_tiles, axis=1)

        jrow = lax.broadcasted_iota(jnp.int32, (n_sel, tq), 0)
        cur = (q0 + lax.broadcasted_iota(jnp.int32, (n_sel, tq), 1)) // SEL_BLOCK
        taken = -3.0e38
        imp = jnp.where(jrow <= cur, imp, -SEL_FORCE)
        for forced_block in (0, cur, cur - 1):
            imp = jnp.where(jrow == forced_block, taken, imp)
        n_rounds = SEL_TOPK - 3
        n_tiles = n_prev + 1
        s_win = [win_logits(0), win_logits(1)]
        win_state, absorbed = None, 0
        for rnd in range(n_rounds):
            top = jnp.max(imp, axis=0, keepdims=True)
            first = jnp.min(jnp.where(imp == top, jrow, n_sel), axis=0, keepdims=True)
            imp = jnp.where(jrow == first, taken, imp)
            while absorbed < (rnd + 1) * n_tiles // n_rounds:
                if absorbed + 2 < n_tiles:
                    s_win.append(win_logits(absorbed + 2))
                win_state = win_absorb(absorbed, s_win[absorbed], win_state)
                absorbed += 1
        owin_scr[...] = _normalise_t(win_state[1])
        not_sel = jnp.where(imp < 0.5 * taken, 0.0, 1.0).astype(BF16)
        for j in range(hpg):
            lhs_scr[_PAIR:_PAIR + n_sel, bands[j]] = not_sel
        if n_sel < N_SEL:
            lhs_scr[_PAIR + n_sel:2 * _PAIR, :] = jnp.ones((N_SEL - n_sel, hpg * tq), BF16)

    tiles_per_variant = (seq // tq) // CMP_VARIANTS
    for v in range(CMP_VARIANTS):
        pl.when(i // tiles_per_variant == v)(
            functools.partial(compress_and_select, N_CMP_PAD * (v + 1) // CMP_VARIANTS))
    lhs = lhs_scr[...]

    def key_tile(kv_ref, tile, mask_tile):
        return jnp.concatenate([kv_ref[0, 0, pl.ds(pl.multiple_of(tile * tq, tq), tq), :],
                                bm_ref[pl.ds(pl.multiple_of(mask_tile * tq, tq), tq), :]], axis=1)

    all_masked = seq // tq

    prev = jnp.maximum(i - 1, 0)
    s_own = _dot(key_tile(slc_ref, i, i), lhs) + tslc_ref[0, tq:2 * tq, :]
    s_prev = (_dot(key_tile(slc_ref, prev, jnp.where(i == 0, all_masked, i - 1)), lhs)
              + tslc_ref[0, 0:tq, :])

    n_far = jnp.maximum(i - 1, 0)
    n_groups = (n_far + FAR_GROUP - 1) // FAR_GROUP

    def tile_or_none(kt):
        valid = kt < n_far
        return jnp.where(valid, kt, 0), jnp.where(valid, kt, all_masked)

    def stage_logits(slot, kg):
        tiles = [key_tile(slc_ref, *tile_or_none(FAR_GROUP * kg + t)) for t in range(FAR_GROUP)]
        s = _dot(jnp.concatenate(tiles, axis=0), lhs_scr[...])
        s_scr[slot] = s
        smax_scr[8 * slot:8 * slot + 1, :] = jnp.max(s, axis=0, keepdims=True)

    def absorb(slot, kg):
        m_old = m_scr[0:1, :]
        m_new = jnp.maximum(m_old, smax_scr[8 * slot:8 * slot + 1, :])
        p = jnp.exp2(s_scr[slot] - m_new)
        v1t = jnp.concatenate([slc1_ref[0, 0, tile_or_none(FAR_GROUP * kg + t)[0]]
                               for t in range(FAR_GROUP)], axis=1)
        acc_scr[...] = jnp.exp2(m_old - m_new) * acc_scr[...] + _dot(v1t, p.astype(BF16))
        m_scr[0:1, :] = m_new

    def far_trip(base, groups):
        for t in range(groups):
            stage_logits((t + 1) % 2, base + t + 1)
            absorb(t % 2, base + t)

    stage_logits(0, 0)
    state = _online_absorb(s_own, slc1_ref[0, 0, i], None)
    m, acc = _online_absorb(s_prev, slc1_ref[0, 0, prev], state)
    m_scr[0:1, :] = m
    acc_scr[...] = acc
    n_long = n_groups // 8

    def long_body(k, carry):
        far_trip(8 * k, 8)
        return carry

    lax.fori_loop(0, n_long, long_body, 0)
    done = 8 * n_long
    for size in (4, 2, 1):
        take = (n_groups - done) // size
        pl.when(take > 0)(functools.partial(far_trip, done, size))
        done = done + size * take

    o_slc_t = _normalise_t(acc_scr[...])

    merged = []
    for j in range(hpg):
        h = g * hpg + j
        gate = [gt_ref[0, 0, pl.ds(br * NSA_HEADS + h, 1), :] for br in range(3)]
        merged.append(gate[0] * ocmp_scr[:, bands[j]] + gate[1] * o_slc_t[:, bands[j]]
                      + gate[2] * owin_scr[:, bands[j]])
    o_ref[0] = jnp.concatenate(merged, axis=0).T.astype(BF16)


def _nsa_attention(q, kc, vct, slc, slc1, win, win1, block_mask, gates, cbt, tslc, twin):
    batch, _, seq, _ = slc.shape
    grp = NSA_KV_GROUPS
    tq = Q_TILE
    rows = NSA_HPG * tq
    per_group = lambda b, g, i: (b, g, 0, 0)
    per_group_t = lambda b, g, i: (b, g, 0, 0, 0)
    return pl.pallas_call(
        _nsa_kernel,
        out_shape=jax.ShapeDtypeStruct((batch, seq, _QW), BF16),
        grid=(batch, grp, seq // tq),
        in_specs=[
            pl.BlockSpec((1, 1, NSA_HPG * NSA_HEAD_DIM, tq), lambda b, g, i: (b, i, g, 0)),
            pl.BlockSpec((1, 1, N_CMP_PAD, _PAIR), per_group),
            pl.BlockSpec((1, 1, NSA_HEAD_DIM, N_CMP_PAD), per_group),
            pl.BlockSpec((1, 1, seq, _PAIR), per_group),
            pl.BlockSpec((1, 1, seq // tq, _VT_ROWS, tq), per_group_t),
            pl.BlockSpec((1, 1, seq, _PAIR), per_group),
            pl.BlockSpec((1, 1, seq // tq, _VT_ROWS, tq), per_group_t),
            _resident((seq + tq, N_SEL), lambda b, g, i: (0, 0)),
            pl.BlockSpec((1, 1, V7X_LANES, tq), lambda b, g, i: (b, i, 0, 0)),
            pl.BlockSpec((1, 2 * N_CMP_PAD, rows), lambda b, g, i: (g, 0, 0)),
            pl.BlockSpec((1, 2 * tq, rows), lambda b, g, i: (g, 0, 0)),
            pl.BlockSpec((1, WINDOW + tq, rows), lambda b, g, i: (g, 0, 0)),
        ],
        out_specs=pl.BlockSpec((1, tq, NSA_HPG * NSA_HEAD_DIM), lambda b, g, i: (b, i, g)),
        scratch_shapes=[pltpu.VMEM((tq // V7X_LANES, N_CMP_PAD + 16, V7X_LANES), F32),
                        pltpu.VMEM((2 * _PAIR, rows), BF16),
                        pltpu.VMEM((8, rows), F32),
                        pltpu.VMEM((_VT_ROWS, rows), F32),
                        pltpu.VMEM((2, FAR_GROUP * tq, rows), F32),
                        pltpu.VMEM((NSA_HEAD_DIM, rows), F32),
                        pltpu.VMEM((NSA_HEAD_DIM, rows), F32),
                        pltpu.VMEM((16, rows), F32)],
        compiler_params=pltpu.CompilerParams(
            dimension_semantics=("arbitrary", "arbitrary", "arbitrary"),
            vmem_limit_bytes=_vmem_limit(52 * 1024 * 1024)),
        name="nsa_attention",
    )(q, kc, vct, slc, slc1, win, win1, block_mask, gates, cbt, tslc, twin)


def _nsa_weight(w):
    d = w.shape[0]
    q = w[:, :_QW]
    kv = w[:, _QW:_QW + 6 * _KVW].reshape(d, 6, NSA_KV_GROUPS, NSA_HEAD_DIM)
    cmp_raw = kv[:, 0:2].reshape(d, 2 * _KVW)
    slc = jnp.stack([kv[:, 2], kv[:, 3]], axis=2).reshape(d, NSA_KV_GROUPS * _PAIR)
    win = jnp.stack([kv[:, 4], kv[:, 5]], axis=2).reshape(d, NSA_KV_GROUPS * _PAIR)
    gates = w[:, _QW + 6 * _KVW:_QW + 6 * _KVW + 3 * NSA_HEADS]
    gates = jnp.pad(gates, ((0, 0), (0, V7X_LANES - 3 * NSA_HEADS)))
    xq = w[:, _QW + 6 * _KVW + 3 * NSA_HEADS:]
    return jnp.concatenate([q, cmp_raw, slc, win, gates, xq], axis=1).astype(BF16)


def kernel(x, mem, norm_mix_g, norm_mem_g, norm_ffn_g, norm_final_g, w_in_conv, conv_w, w_in_nsa,
           cmp_pos_k, cmp_w1_k, cmp_w2_k, cmp_pos_v, cmp_w1_v, cmp_w2_v, rel_bias, w_mem_kv, w_out,
           w_ffn_in, w_ffn_out):
    seq = x.shape[1]
    assert seq % ROW_TILE == 0 and seq % Q_TILE == 0 and x.shape[2] == D_MODEL
    assert seq == CMP_STRIDE * _CHUNKS and seq == SEL_BLOCK * N_SEL and Q_TILE == KV_TILE
    memkv = _memkv(mem, norm_mem_g, w_mem_kv.astype(BF16))
    w_out_b = w_out.astype(BF16)
    w_ffn_in_b = w_ffn_in.astype(BF16)
    w_ffn_out_b = w_ffn_out.astype(BF16)

    tok, xq = _conv_front(x, norm_mix_g[0], w_in_conv[0].astype(BF16), conv_w[0])
    h = _mixout(x, tok, xq, memkv, w_out_b, 0)
    h = _ffn(h, norm_ffn_g[0], w_ffn_in_b, w_ffn_out_b, norm_final_g, 0, final=False)

    q, cmp_raw, slc, slc1, win, win1, gates, xq = _nsa_proj(h, norm_mix_g[1], _nsa_weight(w_in_nsa[0]))
    pos4 = jnp.stack([jnp.tile(cmp_pos_k[0], (1, NSA_KV_GROUPS)),
                      jnp.tile(cmp_pos_v[0], (1, NSA_KV_GROUPS))])
    w1 = jnp.stack([cmp_w1_k[0], cmp_w1_v[0]]).astype(BF16)
    kc = _compress(cmp_raw, pos4, w1, cmp_w2_k[0].astype(BF16)[None], 0, transposed=False)
    vct = _compress(cmp_raw, pos4, w1, cmp_w2_v[0].T.astype(BF16)[None], 1, transposed=True)
    tok = _nsa_attention(q, kc, vct, slc, slc1, win, win1, _block_mask_rows(seq), gates,
                         *_bias_tables(rel_bias))
    h = _mixout(h, tok, xq, memkv, w_out_b, 1)
    return _ffn(h, norm_ffn_g[1], w_ffn_in_b, w_ffn_out_b, norm_final_g, 1, final=True)
```
